```python
import math
import jax
import jax.numpy as jnp
from jax import lax
import numpy as np

D_MODEL = 1024
BATCH = 2
SEQ = 8192
DEPTH = 2

MLA_HEADS = 8
MLA_Q_RANK = 256
MLA_KV_RANK = 128
MLA_NOPE_DIM = 64
MLA_ROPE_DIM = 32
MLA_V_DIM = 64
MLA_QK_DIM = MLA_NOPE_DIM + MLA_ROPE_DIM
ROPE_THETA = 10000.0
SWA_HEADS = 8
SWA_KV_HEADS = 2
SWA_REP = SWA_HEADS // SWA_KV_HEADS
SWA_HEAD_DIM = 64
WINDOW = 128
BLOCK = 128
REL_BUCKETS = 32
REL_MAX_DIST = 128
A_COLS = MLA_Q_RANK + MLA_KV_RANK + MLA_ROPE_DIM
B_COLS = (SWA_HEADS + 2 * SWA_KV_HEADS) * SWA_HEAD_DIM
IN0_COLS = A_COLS + B_COLS
MIX0_WIDTH = MLA_HEADS * MLA_V_DIM + SWA_HEADS * SWA_HEAD_DIM
LRU_WIDTH = D_MODEL
LRU_BLOCKS = 8
LRU_BW = LRU_WIDTH // LRU_BLOCKS
LRU_C = 8.0
CONV_WIDTH = 4
CONV_LEFT = 2
N_EXPERTS = 16
N_GROUPS = 4
EXPERTS_PER_GROUP = N_EXPERTS // N_GROUPS
TOP_K = 2
GROUP_SCORE_K = 2
D_EXPERT = 512
ALPHA = (2.0 * DEPTH) ** 0.25
BETA = (8.0 * DEPTH) ** -0.25
LN_EPS = 1e-5
RMS_EPS = 1e-6

kernel_name = 'hybrid_mla_swa_rglru_grouped_moe_encoder'


def layer_norm(x, g, b):
    xf = x.astype(jnp.float32)
    mu = jnp.mean(xf, -1, keepdims=True)
    var = jnp.mean(jnp.square(xf - mu), -1, keepdims=True)
    return ((xf - mu) * lax.rsqrt(var + LN_EPS) * g.astype(jnp.float32) + b.astype(jnp.float32)).astype(x.dtype)


def rms_norm(x, g):
    xf = x.astype(jnp.float32)
    return (xf * lax.rsqrt(jnp.mean(jnp.square(xf), -1, keepdims=True) + RMS_EPS) * g.astype(jnp.float32)).astype(x.dtype)


def rotary_tables(seq, dtype):
    half = MLA_ROPE_DIM // 2
    inv_freq = ROPE_THETA ** (-jnp.arange(half, dtype=jnp.float32) / half)
    ang = jnp.arange(seq, dtype=jnp.float32)[:, None] * inv_freq[None, :]
    return jnp.cos(ang).astype(dtype), jnp.sin(ang).astype(dtype)


def apply_rope(x, cos, sin):
    x1, x2 = jnp.split(x, 2, axis=-1)
    return jnp.concatenate([x1 * cos - x2 * sin, x1 * sin + x2 * cos], axis=-1)


def mla_attention(q_lat, kv_lat, k_rope, q_norm, w_uq, kv_norm, w_ukv, cos, sin):
    bsz, seq, _ = q_lat.shape
    q = (rms_norm(q_lat, q_norm) @ w_uq).reshape(bsz, seq, MLA_HEADS, MLA_QK_DIM)
    q = jnp.concatenate([q[..., :MLA_NOPE_DIM],
                         apply_rope(q[..., MLA_NOPE_DIM:], cos[:, None, :], sin[:, None, :])], axis=-1)
    kv = (rms_norm(kv_lat, kv_norm) @ w_ukv).reshape(bsz, seq, MLA_HEADS, MLA_NOPE_DIM + MLA_V_DIM)
    k_nope, v = kv[..., :MLA_NOPE_DIM], kv[..., MLA_NOPE_DIM:]
    k_pe = apply_rope(k_rope, cos, sin)
    k = jnp.concatenate([k_nope, jnp.broadcast_to(k_pe[:, :, None, :], (bsz, seq, MLA_HEADS, MLA_ROPE_DIM))], axis=-1)
    scale = MLA_QK_DIM ** -0.5
    n_blk = seq // BLOCK
    q_blocks = jnp.moveaxis(q.reshape(bsz, n_blk, BLOCK, MLA_HEADS, MLA_QK_DIM), 1, 0)

    def attend(qb):
        s = jnp.einsum('bqhd,bkhd->bhqk', qb, k).astype(jnp.float32) * scale
        p = jax.nn.softmax(s, axis=-1).astype(v.dtype)
        return jnp.einsum('bhqk,bkhd->bqhd', p, v)

    o = lax.map(attend, q_blocks)
    return jnp.moveaxis(o, 0, 1).reshape(bsz, seq, MLA_HEADS * MLA_V_DIM)


def t5_bucket(rel):
    n_side = REL_BUCKETS // 2
    max_exact = n_side // 2
    dist = jnp.abs(rel)
    far = max_exact + (jnp.log(jnp.maximum(dist, 1).astype(jnp.float32) / max_exact)
                       / math.log(REL_MAX_DIST / max_exact) * (n_side - max_exact)).astype(jnp.int32)
    far = jnp.minimum(far, n_side - 1)
    return jnp.where(rel > 0, n_side, 0) + jnp.where(dist < max_exact, dist, far)


def windowed_gqa(q, k, v, sinks, rel_bias):
    bsz, seq, _ = q.shape
    n_blk = seq // BLOCK
    qb = q.reshape(bsz, n_blk, BLOCK, SWA_KV_HEADS, SWA_REP, SWA_HEAD_DIM)

    def band(t):
        tp = jnp.pad(t.reshape(bsz, seq, SWA_KV_HEADS, SWA_HEAD_DIM), ((0, 0), (BLOCK, BLOCK), (0, 0), (0, 0)))
        tp = tp.reshape(bsz, n_blk + 2, BLOCK, SWA_KV_HEADS, SWA_HEAD_DIM)
        return jnp.concatenate([tp[:, :-2], tp[:, 1:-1], tp[:, 2:]], axis=2)

    kb, vb = band(k), band(v)
    s = jnp.einsum('bnqgrd,bnkgd->bgrnqk', qb, kb).astype(jnp.float32) * SWA_HEAD_DIM ** -0.5
    q_off = jnp.arange(BLOCK)
    k_off = jnp.arange(3 * BLOCK)
    rel = k_off[None, :] - BLOCK - q_off[:, None]
    bias = rel_bias.astype(jnp.float32)[t5_bucket(rel)]
    bias = jnp.transpose(bias, (2, 0, 1)).reshape(SWA_KV_HEADS, SWA_REP, 1, BLOCK, 3 * BLOCK)
    key_pos = (jnp.arange(n_blk)[:, None] - 1) * BLOCK + k_off[None, :]
    valid = (jnp.abs(rel) <= WINDOW)[None] & ((key_pos >= 0) & (key_pos < seq))[:, None, :]
    s = jnp.where(valid, s + bias, -jnp.inf)
    sink = jnp.broadcast_to(sinks.astype(jnp.float32).reshape(1, SWA_KV_HEADS, SWA_REP, 1, 1, 1), s.shape[:-1] + (1,))
    p = jax.nn.softmax(jnp.concatenate([s, sink], axis=-1), axis=-1)[..., :-1].astype(v.dtype)
    o = jnp.einsum('bgrnqk,bnkgd->bnqgrd', p, vb)
    return o.reshape(bsz, seq, SWA_HEADS * SWA_HEAD_DIM)


def mixer_attention_pair(x, w_in, q_norm, w_uq, kv_norm, w_ukv, sinks, w_out, rel_bias, cos, sin):
    proj = x @ w_in
    offs = np.cumsum([MLA_Q_RANK, MLA_KV_RANK, MLA_ROPE_DIM,
                      SWA_HEADS * SWA_HEAD_DIM, SWA_KV_HEADS * SWA_HEAD_DIM])
    q_lat, kv_lat, k_rope, q_b, k_b, v_b = jnp.split(proj, [int(o) for o in offs], axis=-1)
    out_a = mla_attention(q_lat, kv_lat, k_rope, q_norm, w_uq, kv_norm, w_ukv, cos, sin)
    out_b = windowed_gqa(q_b, k_b, v_b, sinks, rel_bias)
    return jnp.concatenate([out_a, out_b], axis=-1) @ w_out


def block_diag_linear(x, w, b):
    xb = x.reshape(x.shape[:-1] + (LRU_BLOCKS, LRU_BW))
    return (jnp.einsum('bsnc,ncd->bsnd', xb, w) + b).reshape(x.shape)


def rg_lru(xc, w_a, b_a, w_x, b_x, lam, reverse):
    f32 = jnp.float32
    r = jax.nn.sigmoid(block_diag_linear(xc, w_a.astype(f32), b_a.astype(f32)))
    i = jax.nn.sigmoid(block_diag_linear(xc, w_x.astype(f32), b_x.astype(f32)))
    log_a = LRU_C * r * jax.nn.log_sigmoid(lam.astype(f32))
    a = jnp.exp(log_a)
    b = jnp.sqrt(-jnp.expm1(2.0 * log_a)) * (i * xc)

    def combine(c1, c2):
        a1, b1 = c1
        a2, b2 = c2
        return a1 * a2, a2 * b1 + b2

    _, h = lax.associative_scan(combine, (a, b), reverse=reverse, axis=1)
    return h


def mixer_rglru(x, w_in, conv_w, conv_b, wa_f, ba_f, wx_f, bx_f, lam_f,
                wa_b, ba_b, wx_b, bx_b, lam_b, w_out):
    seq = x.shape[1]
    gate, xr = jnp.split(x @ w_in, 2, axis=-1)
    xp = jnp.pad(xr, ((0, 0), (CONV_LEFT, CONV_WIDTH - 1 - CONV_LEFT), (0, 0)))
    xc = conv_b + sum(xp[:, k:k + seq] * conv_w[k] for k in range(CONV_WIDTH))
    xc = xc.astype(jnp.float32)
    h = (rg_lru(xc, wa_f, ba_f, wx_f, bx_f, lam_f, False)
         + rg_lru(xc, wa_b, ba_b, wx_b, bx_b, lam_b, True))
    y = h.astype(x.dtype) * jax.nn.gelu(gate, approximate=True)
    return y @ w_out


def grouped_moe(x, router_w, router_bias, w1, w3, w2):
    f32 = jnp.float32
    bsz, seq, d = x.shape
    t = x.reshape(bsz * seq, d)
    scores = jax.nn.sigmoid(t.astype(f32) @ router_w.astype(f32))
    biased = scores + router_bias.astype(f32)
    group_score = lax.top_k(biased.reshape(-1, N_GROUPS, EXPERTS_PER_GROUP), GROUP_SCORE_K)[0].sum(-1)
    group_mask = jnp.argmax(group_score, axis=-1)[:, None] == jnp.arange(N_GROUPS)[None, :]
    expert_mask = jnp.repeat(group_mask, EXPERTS_PER_GROUP, axis=-1)
    _, idx = lax.top_k(jnp.where(expert_mask, biased, -jnp.inf), TOP_K)
    w = jnp.take_along_axis(scores, idx, axis=-1)
    w = w / jnp.sum(w, axis=-1, keepdims=True)
    gates = jnp.einsum('tk,tke->te', w, jax.nn.one_hot(idx, N_EXPERTS, dtype=f32)).astype(x.dtype)
    y = jnp.zeros_like(t)
    for e in range(N_EXPERTS):
        h = jax.nn.silu(t @ w1[e]) * (t @ w3[e])
        y = y + gates[:, e:e + 1] * (h @ w2[e])
    return y.reshape(bsz, seq, d)


def setup_inputs(seed: int = 0) -> dict:
    key = jax.random.key(seed)
    ks = iter(jax.random.split(key, 48))
    f32 = jnp.float32
    D = D_MODEL

    def nrm(shape, scale):
        return jax.random.normal(next(ks), shape, f32) * scale

    def gain(n):
        return 1.0 + nrm((n,), 0.01)

    def lru_lambda():
        u = jax.random.uniform(next(ks), (LRU_WIDTH,), f32, minval=0.9, maxval=0.999)
        a = u ** (1.0 / LRU_C)
        return jnp.log(a) - jnp.log1p(-a)

    inp = {}
    inp['x'] = nrm((BATCH, SEQ, D), 1.0)
    inp['rel_bias'] = nrm((REL_BUCKETS, SWA_HEADS), 0.2)
    inp['router_w'] = nrm((D, N_EXPERTS), D ** -0.5)
    inp['router_bias'] = nrm((N_EXPERTS,), 0.01)
    inp['l0_w_in'] = nrm((D, IN0_COLS), D ** -0.5)
    inp['l0_q_norm'] = gain(MLA_Q_RANK)
    inp['l0_w_uq'] = nrm((MLA_Q_RANK, MLA_HEADS * MLA_QK_DIM), MLA_Q_RANK ** -0.5)
    inp['l0_kv_norm'] = gain(MLA_KV_RANK)
    inp['l0_w_ukv'] = nrm((MLA_KV_RANK, MLA_HEADS * (MLA_NOPE_DIM + MLA_V_DIM)), MLA_KV_RANK ** -0.5)
    inp['l0_sinks'] = nrm((SWA_HEADS,), 0.5)
    inp['l0_w_out'] = nrm((MIX0_WIDTH, D), MIX0_WIDTH ** -0.5 * BETA)
    inp['l0_ln1_g'] = gain(D)
    inp['l0_ln1_b'] = nrm((D,), 0.01)
    inp['l0_w1'] = nrm((N_EXPERTS, D, D_EXPERT), D ** -0.5)
    inp['l0_w3'] = nrm((N_EXPERTS, D, D_EXPERT), D ** -0.5)
    inp['l0_w2'] = nrm((N_EXPERTS, D_EXPERT, D), D_EXPERT ** -0.5 * BETA)
    inp['l0_ln2_g'] = gain(D)
    inp['l0_ln2_b'] = nrm((D,), 0.01)
    inp['l1_w_in'] = nrm((D, 2 * LRU_WIDTH), D ** -0.5)
    inp['l1_conv_w'] = nrm((CONV_WIDTH, LRU_WIDTH), CONV_WIDTH ** -0.5)
    inp['l1_conv_b'] = nrm((LRU_WIDTH,), 0.01)
    inp['l1_wa_f'] = nrm((LRU_BLOCKS, LRU_BW, LRU_BW), LRU_BW ** -0.5)
    inp['l1_ba_f'] = nrm((LRU_BLOCKS, LRU_BW), 0.01)
    inp['l1_wx_f'] = nrm((LRU_BLOCKS, LRU_BW, LRU_BW), LRU_BW ** -0.5)
    inp['l1_bx_f'] = nrm((LRU_BLOCKS, LRU_BW), 0.01)
    inp['l1_lam_f'] = lru_lambda()
    inp['l1_wa_b'] = nrm((LRU_BLOCKS, LRU_BW, LRU_BW), LRU_BW ** -0.5)
    inp['l1_ba_b'] = nrm((LRU_BLOCKS, LRU_BW), 0.01)
    inp['l1_wx_b'] = nrm((LRU_BLOCKS, LRU_BW, LRU_BW), LRU_BW ** -0.5)
    inp['l1_bx_b'] = nrm((LRU_BLOCKS, LRU_BW), 0.01)
    inp['l1_lam_b'] = lru_lambda()
    inp['l1_w_out'] = nrm((LRU_WIDTH, D), LRU_WIDTH ** -0.5 * BETA)
    inp['l1_ln1_g'] = gain(D)
    inp['l1_ln1_b'] = nrm((D,), 0.01)
    inp['l1_w1'] = nrm((N_EXPERTS, D, D_EXPERT), D ** -0.5)
    inp['l1_w3'] = nrm((N_EXPERTS, D, D_EXPERT), D ** -0.5)
    inp['l1_w2'] = nrm((N_EXPERTS, D_EXPERT, D), D_EXPERT ** -0.5 * BETA)
    inp['l1_ln2_g'] = gain(D)
    inp['l1_ln2_b'] = nrm((D,), 0.01)
    return inp


def reference(x, rel_bias, router_w, router_bias,
              l0_w_in, l0_q_norm, l0_w_uq, l0_kv_norm, l0_w_ukv, l0_sinks, l0_w_out,
              l0_ln1_g, l0_ln1_b, l0_w1, l0_w3, l0_w2, l0_ln2_g, l0_ln2_b,
              l1_w_in, l1_conv_w, l1_conv_b, l1_wa_f, l1_ba_f, l1_wx_f, l1_bx_f, l1_lam_f,
              l1_wa_b, l1_ba_b, l1_wx_b, l1_bx_b, l1_lam_b, l1_w_out,
              l1_ln1_g, l1_ln1_b, l1_w1, l1_w3, l1_w2, l1_ln2_g, l1_ln2_b):
    cos, sin = rotary_tables(x.shape[1], x.dtype)
    mixer_params = [
        (l0_w_in, l0_q_norm, l0_w_uq, l0_kv_norm, l0_w_ukv, l0_sinks, l0_w_out),
        (l1_w_in, l1_conv_w, l1_conv_b, l1_wa_f, l1_ba_f, l1_wx_f, l1_bx_f, l1_lam_f,
         l1_wa_b, l1_ba_b, l1_wx_b, l1_bx_b, l1_lam_b, l1_w_out),
    ]
    norm1 = [(l0_ln1_g, l0_ln1_b), (l1_ln1_g, l1_ln1_b)]
    experts = [(l0_w1, l0_w3, l0_w2), (l1_w1, l1_w3, l1_w2)]
    norm2 = [(l0_ln2_g, l0_ln2_b), (l1_ln2_g, l1_ln2_b)]
    h = x
    for layer in range(DEPTH):
        if layer % 2 == 0:
            mixed = mixer_attention_pair(h, *mixer_params[layer], rel_bias, cos, sin)
        else:
            mixed = mixer_rglru(h, *mixer_params[layer])
        h = layer_norm(ALPHA * h + mixed, *norm1[layer])
        h = layer_norm(ALPHA * h + grouped_moe(h, router_w, router_bias, *experts[layer]), *norm2[layer])
    return h
```

```python
import functools
import math

import jax
import jax.numpy as jnp
import numpy as np
from jax import lax
from jax.experimental import pallas as pl
from jax.experimental.pallas import tpu as pltpu

MLA_HEADS = 8
MLA_Q_RANK = 256
MLA_KV_RANK = 128
MLA_NOPE = 64
MLA_ROPE = 32
MLA_V = 64
MLA_QK = MLA_NOPE + MLA_ROPE
ROPE_THETA = 10000.0
SWA_HEADS = 8
SWA_KV_HEADS = 2
SWA_REP = SWA_HEADS // SWA_KV_HEADS
SWA_D = 64
WINDOW = 128
BLOCK = 128
REL_BUCKETS = 32
REL_MAX_DIST = 128
LRU_BLOCKS = 8
LRU_BW = 128
LRU_C = 8.0
N_EXPERTS = 16
N_GROUPS = 4
EXPERTS_PER_GROUP = 4
N_CLASSES = N_GROUPS * 6
DEPTH = 2
ALPHA = (2.0 * DEPTH) ** 0.25
LN_EPS = 1e-5
RMS_EPS = 1e-6
NEG_BIG = -1e30

LANE = 128
HEAD_PAD = 128
GATE_LANES = 128
MOE_TM = 256

_F32 = jnp.float32
_BF16 = jnp.bfloat16
_NT_DIMS = (((1,), (1,)), ((), ()))
_TN_DIMS = (((0,), (0,)), ((), ()))


def _dot(a, b):
    return jnp.dot(a, b, preferred_element_type=_F32)


def _dot_nt(a, b):
    return lax.dot_general(a, b, _NT_DIMS, preferred_element_type=_F32)


def _rms(x, g):
    return x * lax.rsqrt(jnp.mean(jnp.square(x), -1, keepdims=True) + RMS_EPS) * g


def _layer_norm(x, g, b):
    mu = jnp.mean(x, -1, keepdims=True)
    xc = x - mu
    var = jnp.mean(jnp.square(xc), -1, keepdims=True)
    return xc * lax.rsqrt(var + LN_EPS) * g + b


def _attn_inproj_kernel(x_ref, w_ref, qn_ref, wuqt_ref, kvn_ref, wuk_ref, wuvt_ref,
                        cost_ref, sint_ref, ccs_ref, sn_ref,
                        qt_ref, k_ref, vt_ref, qb_ref, kb_ref, vb_ref):
    xb = x_ref[...].astype(_BF16)
    proj = _dot(xb, w_ref[...])
    q_lat = proj[:, 0:256]
    kv_lat = proj[:, 256:384]
    kr = proj[:, 384:512]
    qn = _rms(q_lat, qn_ref[...]).astype(_BF16)
    qt = _dot_nt(wuqt_ref[...], qn) * (MLA_QK ** -0.5)
    cos_t = cost_ref[...]
    sin_t = sint_ref[...]
    tm = qt.shape[1]
    zpad = jnp.zeros((HEAD_PAD - MLA_QK, tm), _F32)
    for h in range(MLA_HEADS):
        r0 = h * HEAD_PAD
        x1 = qt[r0 + 64:r0 + 80, :]
        x2 = qt[r0 + 80:r0 + 96, :]
        blk = jnp.concatenate([qt[r0:r0 + 64, :], x1 * cos_t - x2 * sin_t, x1 * sin_t + x2 * cos_t, zpad], axis=0)
        qt_ref[0, r0:r0 + HEAD_PAD, :] = blk.astype(_BF16)
    kvn = _rms(kv_lat, kvn_ref[...]).astype(_BF16)
    kn = _dot(kvn, wuk_ref[...])
    lane = lax.broadcasted_iota(jnp.int32, kr.shape, 1)
    swapped = jnp.where(lane < 16, pltpu.roll(kr, 112, 1), pltpu.roll(kr, 16, 1))
    kpe = kr * ccs_ref[...] + swapped * sn_ref[...]
    kpe = pltpu.roll(kpe, 64, 1)
    k_ref[0] = (kn + jnp.concatenate([kpe] * MLA_HEADS, axis=1)).astype(_BF16)
    vt_ref[0] = _dot_nt(wuvt_ref[...], kvn).astype(_BF16)
    qb_ref[...] = (proj[:, 512:1024] * (SWA_D ** -0.5)).astype(_BF16)
    kb_ref[0] = proj[:, 1024:1280].astype(_BF16)
    vb_ref[0] = proj[:, 1280:1536].astype(_BF16)


def _attn_inproj(x2d, w, qn, wuqt, kvn, wuk, wuvt, cos_t, sin_t, ccs, sn, bsz, seq, tm):
    nst = seq // tm
    const = lambda shape: pl.BlockSpec(shape, lambda b, i: (0,) * len(shape))
    out_shape = (
        jax.ShapeDtypeStruct((bsz, MLA_HEADS * HEAD_PAD, seq), _BF16),
        jax.ShapeDtypeStruct((bsz, seq, MLA_HEADS * HEAD_PAD), _BF16),
        jax.ShapeDtypeStruct((bsz, MLA_HEADS * MLA_V, seq), _BF16),
        jax.ShapeDtypeStruct((bsz * seq, SWA_HEADS * SWA_D), _BF16),
        jax.ShapeDtypeStruct((bsz, seq, 256), _BF16),
        jax.ShapeDtypeStruct((bsz, seq, 256), _BF16),
    )
    return pl.pallas_call(
        _attn_inproj_kernel,
        out_shape=out_shape,
        grid=(bsz, nst),
        in_specs=[
            pl.BlockSpec((tm, x2d.shape[1]), lambda b, i: (b * nst + i, 0)),
            const(w.shape), const(qn.shape), const(wuqt.shape), const(kvn.shape), const(wuk.shape),
            const(wuvt.shape),
            pl.BlockSpec((16, tm), lambda b, i: (0, i)),
            pl.BlockSpec((16, tm), lambda b, i: (0, i)),
            pl.BlockSpec((tm, LANE), lambda b, i: (i, 0)),
            pl.BlockSpec((tm, LANE), lambda b, i: (i, 0)),
        ],
        out_specs=(
            pl.BlockSpec((1, MLA_HEADS * HEAD_PAD, tm), lambda b, i: (b, 0, i)),
            pl.BlockSpec((1, tm, MLA_HEADS * HEAD_PAD), lambda b, i: (b, i, 0)),
            pl.BlockSpec((1, MLA_HEADS * MLA_V, tm), lambda b, i: (b, 0, i)),
            pl.BlockSpec((tm, SWA_HEADS * SWA_D), lambda b, i: (b * nst + i, 0)),
            pl.BlockSpec((1, tm, 256), lambda b, i: (b, i, 0)),
            pl.BlockSpec((1, tm, 256), lambda b, i: (b, i, 0)),
        ),
        name="attn_inproj",
    )(x2d, w, qn, wuqt, kvn, wuk, wuvt, cos_t, sin_t, ccs, sn)


def _mla_flash_kernel(qt_ref, k_ref, vt_ref, ot_ref, *, tk):
    qt = qt_ref[0]
    tq = qt.shape[1]
    seq = k_ref.shape[1]

    def body(j, carry):
        m, l, acc = carry
        k0 = pl.multiple_of(j * tk, tk)
        kt = k_ref[0, pl.ds(k0, tk), :]
        st = _dot(kt, qt)
        m_new = jnp.maximum(m, jnp.max(st, axis=0, keepdims=True))
        alpha = jnp.exp(m - m_new)
        p = jnp.exp(st - m_new)
        l = alpha * l + jnp.sum(p, axis=0, keepdims=True)
        vt = vt_ref[0, :, pl.ds(k0, tk)]
        acc = alpha * acc + _dot(vt, p.astype(_BF16))
        return m_new, l, acc

    m0 = jnp.full((1, tq), -jnp.inf, _F32)
    l0 = jnp.zeros((1, tq), _F32)
    a0 = jnp.zeros((MLA_V, tq), _F32)
    m, l, acc = lax.fori_loop(0, seq // tk, body, (m0, l0, a0))
    ot_ref[0] = (acc / l).astype(ot_ref.dtype)


def _mla_flash(qt, k, vt, tq, tk):
    bsz, _, seq = qt.shape
    return pl.pallas_call(
        functools.partial(_mla_flash_kernel, tk=tk),
        out_shape=jax.ShapeDtypeStruct((bsz, MLA_HEADS * MLA_V, seq), _BF16),
        grid=(bsz, MLA_HEADS, seq // tq),
        in_specs=[
            pl.BlockSpec((1, HEAD_PAD, tq), lambda b, h, i: (b, h, i)),
            pl.BlockSpec((1, seq, HEAD_PAD), lambda b, h, i: (b, 0, h)),
            pl.BlockSpec((1, MLA_V, seq), lambda b, h, i: (b, h, 0)),
        ],
        out_specs=pl.BlockSpec((1, MLA_V, tq), lambda b, h, i: (b, h, i)),
        name="mla_flash",
    )(qt, k, vt)


def _swa_kernel(relb_ref, sinks_ref, q_ref, k_ref, v_ref, bucket_ref, o_ref, bias_sc, *, nblk, seq):
    first = jnp.logical_and(pl.program_id(0) == 0, pl.program_id(1) == 0)

    @pl.when(first)
    def _build_bias():
        bucket = bucket_ref[...]
        for h in range(SWA_HEADS):
            acc = jnp.full(bucket.shape, NEG_BIG, _F32)
            for bk in range(REL_BUCKETS):
                acc = jnp.where(bucket == bk, relb_ref[bk, h], acc)
            bias_sc[h] = acc

    j = pl.program_id(1)
    lo = lax.broadcasted_iota(jnp.int32, (BLOCK, LANE), 1) < SWA_D
    koff = lax.broadcasted_iota(jnp.int32, (1, 3 * BLOCK), 1)
    zero = jnp.zeros((BLOCK, LANE), _BF16)

    def block(u, c):
        n = j * nblk + u
        r0 = pl.multiple_of(u * BLOCK, BLOCK)
        w0 = pl.multiple_of(n * BLOCK, BLOCK)
        key_pos = n * BLOCK - BLOCK + koff
        emask = jnp.where(jnp.logical_and(key_pos >= 0, key_pos < seq), 0.0, NEG_BIG).astype(_F32)
        kw = k_ref[0, pl.ds(w0, 3 * BLOCK), :]
        vw = v_ref[0, pl.ds(w0, 3 * BLOCK), :]
        qblk = q_ref[pl.ds(r0, BLOCK), :]
        outs = []
        for g in range(SWA_KV_HEADS):
            k2 = kw[:, g * LANE:(g + 1) * LANE]
            v2 = vw[:, g * LANE:(g + 1) * LANE]
            p0 = qblk[:, g * 256:g * 256 + LANE]
            p1 = qblk[:, g * 256 + LANE:(g + 1) * 256]
            qs = jnp.concatenate([jnp.where(lo, p0, zero), jnp.where(lo, zero, p0),
                                  jnp.where(lo, p1, zero), jnp.where(lo, zero, p1)], axis=0)
            s = _dot_nt(qs, k2)
            bias = bias_sc[g * SWA_REP:(g + 1) * SWA_REP].reshape(SWA_REP * BLOCK, 3 * BLOCK)
            s = s + bias + emask
            sink = jnp.concatenate(
                [jnp.full((BLOCK, 1), sinks_ref[g * SWA_REP + r], _F32) for r in range(SWA_REP)], axis=0)
            m = jnp.maximum(jnp.max(s, axis=1, keepdims=True), sink)
            p = jnp.exp(s - m)
            den = jnp.sum(p, axis=1, keepdims=True) + jnp.exp(sink - m)
            o = _dot(p.astype(_BF16), v2) / den
            outs.append(jnp.where(lo, o[0:BLOCK], o[BLOCK:2 * BLOCK]))
            outs.append(jnp.where(lo, o[2 * BLOCK:3 * BLOCK], o[3 * BLOCK:4 * BLOCK]))
        o_ref[pl.ds(r0, BLOCK), :] = jnp.concatenate(outs, axis=1).astype(o_ref.dtype)
        return c

    lax.fori_loop(0, nblk, block, 0)


def _swa(qb, kpad, vpad, bucket, rel_bias, sinks, bsz, seq, nblk):
    nsteps = seq // (nblk * BLOCK)
    rows = nblk * BLOCK
    return pl.pallas_call(
        functools.partial(_swa_kernel, nblk=nblk, seq=seq),
        out_shape=jax.ShapeDtypeStruct((bsz * seq, SWA_HEADS * SWA_D), _BF16),
        grid=(bsz, nsteps),
        in_specs=[
            pl.BlockSpec(memory_space=pltpu.SMEM),
            pl.BlockSpec(memory_space=pltpu.SMEM),
            pl.BlockSpec((rows, SWA_HEADS * SWA_D), lambda b, j: (b * nsteps + j, 0)),
            pl.BlockSpec((1, seq + 2 * BLOCK, 256), lambda b, j: (b, 0, 0)),
            pl.BlockSpec((1, seq + 2 * BLOCK, 256), lambda b, j: (b, 0, 0)),
            pl.BlockSpec((BLOCK, 3 * BLOCK), lambda b, j: (0, 0)),
        ],
        out_specs=pl.BlockSpec((rows, SWA_HEADS * SWA_D), lambda b, j: (b * nsteps + j, 0)),
        scratch_shapes=[pltpu.VMEM((SWA_HEADS, BLOCK, 3 * BLOCK), _F32)],
        compiler_params=pltpu.CompilerParams(dimension_semantics=("arbitrary", "arbitrary")),
        name="swa",
    )(rel_bias, sinks, qb, kpad, vpad, bucket)


def _route(logits_t, rbias):
    sc = jax.nn.sigmoid(logits_t)
    bz = sc + rbias
    s_rows = [sc[e:e + 1, :] for e in range(N_EXPERTS)]
    b_rows = [bz[e:e + 1, :] for e in range(N_EXPERTS)]
    gsel = None
    best = None
    for g in range(N_GROUPS):
        r = b_rows[4 * g:4 * g + 4]
        gs = r[0] + r[1]
        for (i, k) in ((0, 2), (0, 3), (1, 2), (1, 3), (2, 3)):
            gs = jnp.maximum(gs, r[i] + r[k])
        if g == 0:
            gsel = jnp.zeros(gs.shape, jnp.int32)
            best = gs
        else:
            better = gs > best
            gsel = jnp.where(better, g, gsel)
            best = jnp.where(better, gs, best)

    def pick(rows, k):
        out = rows[12 + k]
        for g in (2, 1, 0):
            out = jnp.where(gsel == g, rows[4 * g + k], out)
        return out

    v = [pick(b_rows, k) for k in range(4)]
    s = [pick(s_rows, k) for k in range(4)]
    i1 = jnp.zeros(gsel.shape, jnp.int32)
    m1 = v[0]
    w1 = s[0]
    for k in range(1, 4):
        gt = v[k] > m1
        i1 = jnp.where(gt, k, i1)
        m1 = jnp.where(gt, v[k], m1)
        w1 = jnp.where(gt, s[k], w1)
    i2 = jnp.full(gsel.shape, -1, jnp.int32)
    m2 = jnp.full(m1.shape, -jnp.inf, _F32)
    w2 = jnp.zeros(m1.shape, _F32)
    for k in range(4):
        ok = jnp.logical_and(i1 != k, jnp.logical_or(i2 < 0, v[k] > m2))
        i2 = jnp.where(ok, k, i2)
        m2 = jnp.where(ok, v[k], m2)
        w2 = jnp.where(ok, s[k], w2)
    tot = w1 + w2
    g1 = w1 / tot
    g2 = w2 / tot
    first_lo = i1 < i2
    lo = jnp.where(first_lo, i1, i2)
    hi = jnp.where(first_lo, i2, i1)
    pair = jnp.where(lo == 0, hi - 1, jnp.where(lo == 1, hi + 1, 5))
    cls = gsel * 6 + pair
    return cls, jnp.where(first_lo, g1, g2), jnp.where(first_lo, g2, g1)


def _ln_router_store(z, g_ref, b_ref, rwt_ref, rb_ref, h_ref, cls_ref):
    h = _layer_norm(z, g_ref[...], b_ref[...])
    tm = h.shape[0]
    logits_t = lax.dot_general(rwt_ref[...], h, _NT_DIMS, precision=lax.Precision.HIGHEST,
                               preferred_element_type=_F32)
    cls, g_lo, g_hi = _route(logits_t, rb_ref[...])
    rows = jnp.concatenate([g_lo, g_hi, jnp.zeros((GATE_LANES - 2, tm), _F32)], axis=0)
    d = h.shape[1]
    h_ref[:, 0:d] = h
    h_ref[:, d:d + GATE_LANES] = rows.T
    cls_ref[0] = cls


def _outproj_ln_router_kernel(ot_ref, ob_ref, x_ref, wa_ref, wb_ref, g_ref, b_ref, rwt_ref, rb_ref,
                              h_ref, cls_ref):
    mixed = lax.dot_general(ot_ref[0], wa_ref[...], _TN_DIMS, preferred_element_type=_F32)
    mixed = mixed + _dot(ob_ref[...], wb_ref[...])
    z = ALPHA * x_ref[...] + mixed
    _ln_router_store(z, g_ref, b_ref, rwt_ref, rb_ref, h_ref, cls_ref)


def _outproj_ln_router(ot, ob, x2d, wa, wb, g, b, rwt, rb, bsz, seq, tm):
    nst = seq // tm
    d = x2d.shape[1]
    const = lambda shape: pl.BlockSpec(shape, lambda bb, i: (0,) * len(shape))
    return pl.pallas_call(
        _outproj_ln_router_kernel,
        out_shape=(jax.ShapeDtypeStruct((bsz * seq, d + GATE_LANES), _F32),
                   jax.ShapeDtypeStruct((bsz * nst, 1, tm), jnp.int32)),
        grid=(bsz, nst),
        in_specs=[
            pl.BlockSpec((1, ot.shape[1], tm), lambda bb, i: (bb, 0, i)),
            pl.BlockSpec((tm, ob.shape[1]), lambda bb, i: (bb * nst + i, 0)),
            pl.BlockSpec((tm, d), lambda bb, i: (bb * nst + i, 0)),
            const(wa.shape), const(wb.shape), const(g.shape), const(b.shape), const(rwt.shape), const(rb.shape),
        ],
        out_specs=(pl.BlockSpec((tm, d + GATE_LANES), lambda bb, i: (bb * nst + i, 0)),
                   pl.BlockSpec((1, 1, tm), lambda bb, i: (bb * nst + i, 0, 0))),
        name="outproj_ln_router",
    )(ot, ob, x2d, wa, wb, g, b, rwt, rb)


def _moe_kernel(ea_ref, eb_ref, nv_ref, src_ref, h_hbm, w1a, w3a, w2a, w1b, w3b, w2b, g_ref, b_ref,
                out_hbm, xbuf, obuf, sem_in, sem_out):
    i = pl.program_id(0)
    nv = nv_ref[i]
    tm, width = xbuf.shape
    d = obuf.shape[1]

    def row_in(r):
        return pltpu.make_async_copy(h_hbm.at[src_ref[0, 0, r]], xbuf.at[r], sem_in)

    def row_out(r):
        return pltpu.make_async_copy(obuf.at[r], out_hbm.at[src_ref[0, 0, r]], sem_out)

    @pl.when(nv > 0)
    def _tile():
        def start_in(r, c):
            row_in(r).start()
            return c

        def wait_in(r, c):
            row_in(r).wait()
            return c

        lax.fori_loop(0, tm, start_in, 0)
        lax.fori_loop(0, tm, wait_in, 0)
        x = xbuf[:, 0:d]
        gates = xbuf[:, d:d + GATE_LANES]
        ga = gates[:, 0:1]
        gb = gates[:, 1:2]
        xb = x.astype(_BF16)

        def expert(w1, w3, w2):
            hh = jax.nn.silu(_dot(xb, w1[...])) * _dot(xb, w3[...])
            return _dot(hh.astype(_BF16), w2[...])

        y = ga * expert(w1a, w3a, w2a) + gb * expert(w1b, w3b, w2b)
        obuf[...] = _layer_norm(ALPHA * x + y, g_ref[...], b_ref[...])

        def start_out(r, c):
            row_out(r).start()
            return c

        def wait_out(r, c):
            row_out(r).wait()
            return c

        lax.fori_loop(0, nv, start_out, 0)
        lax.fori_loop(0, nv, wait_out, 0)


def _moe(he, ea, eb, nvalid, src, w1, w3, w2, g, b, n_tokens, d):
    ntiles = ea.shape[0]
    tm = MOE_TM
    dff = w1.shape[2]
    wspec_a = lambda shape: pl.BlockSpec((None,) + shape, lambda i, ea_r, eb_r, nv_r: (ea_r[i], 0, 0))
    wspec_b = lambda shape: pl.BlockSpec((None,) + shape, lambda i, ea_r, eb_r, nv_r: (eb_r[i], 0, 0))
    grid_spec = pltpu.PrefetchScalarGridSpec(
        num_scalar_prefetch=3,
        grid=(ntiles,),
        in_specs=[
            pl.BlockSpec((1, 1, tm), lambda i, *_: (i, 0, 0), memory_space=pltpu.SMEM),
            pl.BlockSpec(memory_space=pl.ANY),
            wspec_a((d, dff)), wspec_a((d, dff)), wspec_a((dff, d)),
            wspec_b((d, dff)), wspec_b((d, dff)), wspec_b((dff, d)),
            pl.BlockSpec((1, d), lambda i, *_: (0, 0)),
            pl.BlockSpec((1, d), lambda i, *_: (0, 0)),
        ],
        out_specs=pl.BlockSpec(memory_space=pl.ANY),
        scratch_shapes=[
            pltpu.VMEM((tm, d + GATE_LANES), _F32),
            pltpu.VMEM((tm, d), _F32),
            pltpu.SemaphoreType.DMA,
            pltpu.SemaphoreType.DMA,
        ],
    )
    return pl.pallas_call(
        _moe_kernel,
        out_shape=jax.ShapeDtypeStruct((n_tokens, d), _F32),
        grid_spec=grid_spec,
        compiler_params=pltpu.CompilerParams(dimension_semantics=("arbitrary",)),
        name="moe",
    )(ea, eb, nvalid, src, he, w1, w3, w2, w1, w3, w2, g, b)


def _moe_plan(cls, n_tokens):
    tm = MOE_TM
    ntiles = n_tokens // tm + N_CLASSES
    cls = cls.reshape(n_tokens)
    order = jnp.argsort(cls, stable=True).astype(jnp.int32)
    counts = jnp.sum((cls[:, None] == jnp.arange(N_CLASSES, dtype=jnp.int32)[None, :]).astype(jnp.int32), axis=0)
    cstart = jnp.cumsum(counts) - counts
    ptiles = (counts + tm - 1) // tm
    tend = jnp.cumsum(ptiles)
    tstart = tend - ptiles
    tile = jnp.arange(ntiles, dtype=jnp.int32)
    tcls = jnp.minimum(jnp.sum((tile[:, None] >= tend[None, :]).astype(jnp.int32), axis=1), N_CLASSES - 1)
    used = tile < tend[-1]
    nvalid = jnp.where(used, jnp.clip(counts[tcls] - (tile - tstart[tcls]) * tm, 0, tm), 0).astype(jnp.int32)
    r = jnp.arange(tm, dtype=jnp.int32)[None, :]
    valid = r < nvalid[:, None]
    pos = jnp.clip((cstart[tcls] + (tile - tstart[tcls]) * tm)[:, None] + r, 0, n_tokens - 1)
    src = jnp.where(valid, order[pos], 0).astype(jnp.int32).reshape(ntiles, 1, tm)
    pair_lo = jnp.array([0, 0, 0, 1, 1, 2], jnp.int32)
    pair_hi = jnp.array([1, 2, 3, 2, 3, 3], jnp.int32)
    ea = (4 * (tcls // 6) + pair_lo[tcls % 6]).astype(jnp.int32)
    eb = (4 * (tcls // 6) + pair_hi[tcls % 6]).astype(jnp.int32)
    return ea, eb, nvalid, src


def _lru_inproj_kernel(x_ref, w_ref, gate_ref, xr_ref):
    u = _dot(x_ref[...].astype(_BF16), w_ref[...])
    half = gate_ref.shape[1]
    gate_ref[...] = u[:, 0:half]
    xr_ref[...] = u[:, half:2 * half]


def _lru_inproj(h2d, w, tm):
    n, d = h2d.shape
    half = w.shape[1] // 2
    return pl.pallas_call(
        _lru_inproj_kernel,
        out_shape=(jax.ShapeDtypeStruct((n, half), _F32), jax.ShapeDtypeStruct((n, half), _F32)),
        grid=(n // tm,),
        in_specs=[pl.BlockSpec((tm, d), lambda i: (i, 0)), pl.BlockSpec(w.shape, lambda i: (0, 0))],
        out_specs=(pl.BlockSpec((tm, half), lambda i: (i, 0)), pl.BlockSpec((tm, half), lambda i: (i, 0))),
        name="lru_inproj",
    )(h2d, w)


def _scan_block(a, b, h_in, reverse):
    tt = a.shape[0]
    sub = lax.broadcasted_iota(jnp.int32, a.shape, 0) % 8
    for s in (1, 2, 4):
        if reverse:
            keep = sub < 8 - s
            a_sh = pltpu.roll(a, tt - s, 0)
            b_sh = pltpu.roll(b, tt - s, 0)
        else:
            keep = sub >= s
            a_sh = pltpu.roll(a, s, 0)
            b_sh = pltpu.roll(b, s, 0)
        b = jnp.where(keep, a * b_sh + b, b)
        a = jnp.where(keep, a * a_sh, a)
    ngrp = tt // 8
    hb = jnp.broadcast_to(h_in, (8, a.shape[1]))
    outs = [None] * ngrp
    order = range(ngrp - 1, -1, -1) if reverse else range(ngrp)
    edge = 0 if reverse else 7
    for j in order:
        hj = a[8 * j:8 * j + 8, :] * hb + b[8 * j:8 * j + 8, :]
        outs[j] = hj
        hb = jnp.broadcast_to(hj[edge:edge + 1, :], hj.shape)
    return jnp.concatenate(outs, axis=0), hb[0:1, :]


def _rglru_kernel(*refs, reverse, final, ntile):
    if final:
        (xr_ref, prev_ref, next_ref, cw_ref, cb_ref, wa_ref, ba_ref, wx_ref, bx_ref, lam_ref,
         hf_ref, gate_ref, res_ref, wout_ref, g_ref, b_ref, rwt_ref, rb_ref,
         h_ref, cls_ref, xe_sc, carry_sc) = refs
    else:
        (xr_ref, prev_ref, next_ref, cw_ref, cb_ref, wa_ref, ba_ref, wx_ref, bx_ref, lam_ref,
         hout_ref, xe_sc, carry_sc) = refs
    i = pl.program_id(1)
    ti = (ntile - 1 - i) if reverse else i
    tt = xr_ref.shape[0]

    @pl.when(i == 0)
    def _reset():
        carry_sc[...] = jnp.zeros_like(carry_sc)

    xe_sc[0:8, :] = jnp.where(ti > 0, prev_ref[...], 0.0)
    xe_sc[8:8 + tt, :] = xr_ref[...]
    xe_sc[8 + tt:16 + tt, :] = jnp.where(ti < ntile - 1, next_ref[...], 0.0)
    cw = cw_ref[...]
    xc = (cb_ref[...] + cw[0:1, :] * xe_sc[6:6 + tt, :] + cw[1:2, :] * xe_sc[7:7 + tt, :]
          + cw[2:3, :] * xe_sc[8:8 + tt, :] + cw[3:4, :] * xe_sc[9:9 + tt, :])
    lam = lam_ref[...]
    log_sig = -jnp.log(1.0 + jnp.exp(-lam))
    ys = []
    for n in range(LRU_BLOCKS):
        sl = slice(n * LRU_BW, (n + 1) * LRU_BW)
        xcn = xc[:, sl]
        xb = xcn.astype(_BF16)
        r = jax.nn.sigmoid(_dot(xb, wa_ref[n]) + ba_ref[:, sl])
        ig = jax.nn.sigmoid(_dot(xb, wx_ref[n]) + bx_ref[:, sl])
        log_a = LRU_C * r * log_sig[:, sl]
        a = jnp.exp(log_a)
        th = jnp.tanh(log_a)
        bb = jnp.sqrt(-2.0 * th / (1.0 - th)) * (ig * xcn)
        h, last = _scan_block(a, bb, carry_sc[:, sl], reverse)
        carry_sc[:, sl] = last
        if final:
            ys.append((hf_ref[:, sl] + h) * jax.nn.gelu(gate_ref[:, sl], approximate=True))
        else:
            hout_ref[:, sl] = h
    if final:
        y = jnp.concatenate(ys, axis=1).astype(_BF16)
        z = ALPHA * res_ref[...] + _dot(y, wout_ref[...])
        _ln_router_store(z, g_ref, b_ref, rwt_ref, rb_ref, h_ref, cls_ref)


def _rglru(xr, cw, cb, wa, ba, wx, bx, lam, bsz, seq, tt, reverse, tail=None):
    ntile = seq // tt
    width = xr.shape[1]
    final = tail is not None
    tidx = (lambda i: ntile - 1 - i) if reverse else (lambda i: i)
    row = lambda b, i: (b * ntile + tidx(i), 0)
    g8 = tt // 8
    nb8 = seq // 8

    def prev_map(b, i):
        return (b * nb8 + jnp.maximum(tidx(i) * g8 - 1, 0), 0)

    def next_map(b, i):
        return (b * nb8 + jnp.minimum((tidx(i) + 1) * g8, nb8 - 1), 0)

    const = lambda shape: pl.BlockSpec(shape, lambda b, i: (0,) * len(shape))
    in_specs = [
        pl.BlockSpec((tt, width), row),
        pl.BlockSpec((8, width), prev_map),
        pl.BlockSpec((8, width), next_map),
        const(cw.shape), const(cb.shape), const(wa.shape), const(ba.shape), const(wx.shape), const(bx.shape),
        const(lam.shape),
    ]
    args = [xr, xr, xr, cw, cb, wa, ba, wx, bx, lam]
    scratch = [pltpu.VMEM((tt + 16, width), _F32), pltpu.VMEM((1, width), _F32)]
    if final:
        hf, gate, res, wout, g, b, rwt, rb = tail
        d = res.shape[1]
        in_specs += [pl.BlockSpec((tt, width), row), pl.BlockSpec((tt, width), row), pl.BlockSpec((tt, d), row),
                     const(wout.shape), const(g.shape), const(b.shape), const(rwt.shape), const(rb.shape)]
        args += [hf, gate, res, wout, g, b, rwt, rb]
        out_shape = (jax.ShapeDtypeStruct((bsz * seq, d + GATE_LANES), _F32),
                     jax.ShapeDtypeStruct((bsz * ntile, 1, tt), jnp.int32))
        out_specs = (pl.BlockSpec((tt, d + GATE_LANES), row),
                     pl.BlockSpec((1, 1, tt), lambda b, i: (b * ntile + tidx(i), 0, 0)))
    else:
        out_shape = jax.ShapeDtypeStruct((bsz * seq, width), _F32)
        out_specs = pl.BlockSpec((tt, width), row)
    return pl.pallas_call(
        functools.partial(_rglru_kernel, reverse=reverse, final=final, ntile=ntile),
        out_shape=out_shape,
        grid=(bsz, ntile),
        in_specs=in_specs,
        out_specs=out_specs,
        scratch_shapes=scratch,
        compiler_params=pltpu.CompilerParams(dimension_semantics=("arbitrary", "arbitrary")),
        name="rglru_bwd_tail" if final else "rglru_fwd",
    )(*args)


def _rotary_tables(seq):
    half = MLA_ROPE // 2
    inv_freq = ROPE_THETA ** (-jnp.arange(half, dtype=_F32) / half)
    ang = jnp.arange(seq, dtype=_F32)[:, None] * inv_freq[None, :]
    return jnp.cos(ang), jnp.sin(ang)


def _t5_bucket(rel):
    n_side = REL_BUCKETS // 2
    max_exact = n_side // 2
    dist = jnp.abs(rel)
    far = max_exact + (jnp.log(jnp.maximum(dist, 1).astype(_F32) / max_exact)
                       / math.log(REL_MAX_DIST / max_exact) * (n_side - max_exact)).astype(jnp.int32)
    far = jnp.minimum(far, n_side - 1)
    return jnp.where(rel > 0, n_side, 0) + jnp.where(dist < max_exact, dist, far)


def _band_buckets():
    rel = jnp.arange(3 * BLOCK)[None, :] - BLOCK - jnp.arange(BLOCK)[:, None]
    return jnp.where(jnp.abs(rel) <= WINDOW, _t5_bucket(rel), -1).astype(jnp.int32)


def _pick_tile(seq, want):
    t = min(want, seq)
    while seq % t:
        t //= 2
    return t


def kernel(x, rel_bias, router_w, router_bias, l0_w_in, l0_q_norm, l0_w_uq, l0_kv_norm, l0_w_ukv, l0_sinks, l0_w_out, l0_ln1_g, l0_ln1_b, l0_w1, l0_w3, l0_w2, l0_ln2_g, l0_ln2_b, l1_w_in, l1_conv_w, l1_conv_b, l1_wa_f, l1_ba_f, l1_wx_f, l1_bx_f, l1_lam_f, l1_wa_b, l1_ba_b, l1_wx_b, l1_bx_b, l1_lam_b, l1_w_out, l1_ln1_g, l1_ln1_b, l1_w1, l1_w3, l1_w2, l1_ln2_g, l1_ln2_b):
    bsz, seq, d = x.shape
    n_tok = bsz * seq
    assert seq % BLOCK == 0 and d == LRU_BLOCKS * LRU_BW and n_tok % MOE_TM == 0
    x2d = x.reshape(n_tok, d)
    row = lambda v: v.reshape(1, -1).astype(_F32)
    rwt = router_w.astype(_F32).T
    rb = router_bias.astype(_F32).reshape(N_EXPERTS, 1)

    o = np.cumsum([0, MLA_Q_RANK, MLA_KV_RANK, MLA_ROPE, SWA_HEADS * SWA_D, SWA_KV_HEADS * SWA_D,
                   SWA_KV_HEADS * SWA_D])
    w_q, w_kv, w_kr, w_qb, w_kb, w_vb = [l0_w_in[:, o[i]:o[i + 1]] for i in range(6)]
    dup = lambda w: jnp.concatenate([w[:, 0:64], w[:, 0:64], w[:, 64:128], w[:, 64:128]], axis=1)
    w_in0 = jnp.concatenate([w_q, w_kv, jnp.pad(w_kr, ((0, 0), (0, LANE - MLA_ROPE))), w_qb, dup(w_kb), dup(w_vb)],
                            axis=1).astype(_BF16)
    wuq = l0_w_uq.reshape(MLA_Q_RANK, MLA_HEADS, MLA_QK)
    wuqt = jnp.pad(wuq, ((0, 0), (0, 0), (0, HEAD_PAD - MLA_QK))).reshape(MLA_Q_RANK, MLA_HEADS * HEAD_PAD).T
    wukv = l0_w_ukv.reshape(MLA_KV_RANK, MLA_HEADS, MLA_NOPE + MLA_V)
    wuk = jnp.pad(wukv[:, :, :MLA_NOPE], ((0, 0), (0, 0), (0, HEAD_PAD - MLA_NOPE))).reshape(MLA_KV_RANK, -1)
    wuvt = wukv[:, :, MLA_NOPE:].reshape(MLA_KV_RANK, MLA_HEADS * MLA_V).T
    cos, sin = _rotary_tables(seq)
    ccs = jnp.pad(jnp.concatenate([cos, cos], axis=1), ((0, 0), (0, LANE - MLA_ROPE)))
    sn = jnp.pad(jnp.concatenate([-sin, sin], axis=1), ((0, 0), (0, LANE - MLA_ROPE)))

    tm = _pick_tile(seq, 512)
    qt, k, vt, qb, kb, vb = _attn_inproj(
        x2d, w_in0, row(l0_q_norm), wuqt.astype(_BF16), row(l0_kv_norm), wuk.astype(_BF16), wuvt.astype(_BF16),
        cos.T, sin.T, ccs, sn, bsz, seq, tm)
    ot = _mla_flash(qt, k, vt, _pick_tile(seq, 512), _pick_tile(seq, 512))
    pad = ((0, 0), (BLOCK, BLOCK), (0, 0))
    ob = _swa(qb, jnp.pad(kb, pad), jnp.pad(vb, pad), _band_buckets(), rel_bias.astype(_F32),
              l0_sinks.astype(_F32), bsz, seq, _pick_tile(seq // BLOCK, 8))
    n_a = MLA_HEADS * MLA_V
    he, cls = _outproj_ln_router(ot, ob, x2d, l0_w_out[:n_a].astype(_BF16), l0_w_out[n_a:].astype(_BF16),
                                 row(l0_ln1_g), row(l0_ln1_b), rwt, rb, bsz, seq, tm)
    ea, eb, nvalid, src = _moe_plan(cls, n_tok)
    h = _moe(he, ea, eb, nvalid, src, l0_w1.astype(_BF16), l0_w3.astype(_BF16), l0_w2.astype(_BF16),
             row(l0_ln2_g), row(l0_ln2_b), n_tok, d)

    gate, xr = _lru_inproj(h, l1_w_in.astype(_BF16), tm)
    tt = _pick_tile(seq, 512)
    cw = l1_conv_w.astype(_F32)
    cb = row(l1_conv_b)
    hf = _rglru(xr, cw, cb, l1_wa_f.astype(_BF16), row(l1_ba_f), l1_wx_f.astype(_BF16), row(l1_bx_f),
                row(l1_lam_f), bsz, seq, tt, reverse=False)
    he, cls = _rglru(xr, cw, cb, l1_wa_b.astype(_BF16), row(l1_ba_b), l1_wx_b.astype(_BF16), row(l1_bx_b),
                     row(l1_lam_b), bsz, seq, tt, reverse=True,
                     tail=(hf, gate, h, l1_w_out.astype(_BF16), row(l1_ln1_g), row(l1_ln1_b), rwt, rb))
    ea, eb, nvalid, src = _moe_plan(cls, n_tok)
    h = _moe(he, ea, eb, nvalid, src, l1_w1.astype(_BF16), l1_w3.astype(_BF16), l1_w2.astype(_BF16),
             row(l1_ln2_g), row(l1_ln2_b), n_tok, d)
    return h.reshape(bsz, seq, d)
```

```python
import functools
import math

import jax
import jax.numpy as jnp
import numpy as np
from jax import lax
from jax.experimental import pallas as pl
from jax.experimental.pallas import tpu as pltpu

MLA_HEADS = 8
MLA_Q_RANK = 256
MLA_KV_RANK = 128
MLA_NOPE = 64
MLA_ROPE = 32
MLA_V = 64
MLA_QK = MLA_NOPE + MLA_ROPE
ROPE_THETA = 10000.0
SWA_HEADS = 8
SWA_KV_HEADS = 2
SWA_REP = SWA_HEADS // SWA_KV_HEADS
SWA_D = 64
WINDOW = 128
BLOCK = 128
REL_BUCKETS = 32
REL_MAX_DIST = 128
LRU_BLOCKS = 8
LRU_BW = 128
LRU_C = 8.0
N_EXPERTS = 16
N_GROUPS = 4
EXPERTS_PER_GROUP = 4
N_CLASSES = N_GROUPS * 6
DEPTH = 2
ALPHA = (2.0 * DEPTH) ** 0.25
LN_EPS = 1e-5
RMS_EPS = 1e-6
NEG_BIG = -1e30

LANE = 128
HEAD_PAD = 128
GATE_LANES = 128
VT_ROWS = 80
_LOG2E = math.log2(math.e)
MOE_TM = 256

_F32 = jnp.float32
_BF16 = jnp.bfloat16
_NT_DIMS = (((1,), (1,)), ((), ()))
_TN_DIMS = (((0,), (0,)), ((), ()))


def _dot(a, b):
    return jnp.dot(a, b, preferred_element_type=_F32)


def _dot_nt(a, b):
    return lax.dot_general(a, b, _NT_DIMS, preferred_element_type=_F32)


def _sigmoid(x):
    return 0.5 * jnp.tanh(0.5 * x) + 0.5


def _rms(x, g):
    return x * lax.rsqrt(jnp.mean(jnp.square(x), -1, keepdims=True) + RMS_EPS) * g


def _layer_norm(x, g, b):
    mu = jnp.mean(x, -1, keepdims=True)
    xc = x - mu
    var = jnp.mean(jnp.square(xc), -1, keepdims=True)
    return xc * lax.rsqrt(var + LN_EPS) * g + b


def _attn_inproj_kernel(x_ref, w_ref, qn_ref, wuqt_ref, kvn_ref, wuk_ref, wuvt_ref,
                        cost_ref, sint_ref, ccs_ref, sn_ref,
                        qt_ref, k_ref, vt_ref, qb_ref, kb_ref, vb_ref):
    xb = x_ref[...].astype(_BF16)
    proj = _dot(xb, w_ref[...])
    q_lat = proj[:, 0:256]
    kv_lat = proj[:, 256:384]
    kr = proj[:, 384:512]
    qn = _rms(q_lat, qn_ref[...]).astype(_BF16)
    qt = _dot_nt(wuqt_ref[...], qn) * (MLA_QK ** -0.5 * _LOG2E)
    cos_t = cost_ref[...]
    sin_t = sint_ref[...]
    tm = qt.shape[1]
    zpad = jnp.zeros((HEAD_PAD - MLA_QK, tm), _F32)
    for h in range(MLA_HEADS):
        r0 = h * HEAD_PAD
        x1 = qt[r0 + 64:r0 + 80, :]
        x2 = qt[r0 + 80:r0 + 96, :]
        blk = jnp.concatenate([qt[r0:r0 + 64, :], x1 * cos_t - x2 * sin_t, x1 * sin_t + x2 * cos_t, zpad], axis=0)
        qt_ref[0, r0:r0 + HEAD_PAD, :] = blk.astype(_BF16)
    kvn = _rms(kv_lat, kvn_ref[...]).astype(_BF16)
    kn = _dot(kvn, wuk_ref[...])
    lane = lax.broadcasted_iota(jnp.int32, kr.shape, 1)
    swapped = jnp.where(lane < 16, pltpu.roll(kr, 112, 1), pltpu.roll(kr, 16, 1))
    kpe = kr * ccs_ref[...] + swapped * sn_ref[...]
    kpe = pltpu.roll(kpe, 64, 1)
    k_ref[0] = (kn + jnp.concatenate([kpe] * MLA_HEADS, axis=1)).astype(_BF16)
    vt = _dot_nt(wuvt_ref[...], kvn).astype(_BF16)
    extra = (lax.broadcasted_iota(jnp.int32, (VT_ROWS - MLA_V, tm), 0) == 0).astype(_BF16)
    for h in range(MLA_HEADS):
        vt_ref[0, h * VT_ROWS:h * VT_ROWS + MLA_V, :] = vt[h * MLA_V:(h + 1) * MLA_V, :]
        vt_ref[0, h * VT_ROWS + MLA_V:(h + 1) * VT_ROWS, :] = extra
    qb_ref[...] = (proj[:, 512:1024] * (SWA_D ** -0.5)).astype(_BF16)
    kb_ref[0] = proj[:, 1024:1280].astype(_BF16)
    vb_ref[0] = proj[:, 1280:1536].astype(_BF16)


def _attn_inproj(x2d, w, qn, wuqt, kvn, wuk, wuvt, cos_t, sin_t, ccs, sn, bsz, seq, tm):
    nst = seq // tm
    const = lambda shape: pl.BlockSpec(shape, lambda b, i: (0,) * len(shape))
    out_shape = (
        jax.ShapeDtypeStruct((bsz, MLA_HEADS * HEAD_PAD, seq), _BF16),
        jax.ShapeDtypeStruct((bsz, seq, MLA_HEADS * HEAD_PAD), _BF16),
        jax.ShapeDtypeStruct((bsz, MLA_HEADS * VT_ROWS, seq), _BF16),
        jax.ShapeDtypeStruct((bsz * seq, SWA_HEADS * SWA_D), _BF16),
        jax.ShapeDtypeStruct((bsz, seq, 256), _BF16),
        jax.ShapeDtypeStruct((bsz, seq, 256), _BF16),
    )
    return pl.pallas_call(
        _attn_inproj_kernel,
        out_shape=out_shape,
        grid=(bsz, nst),
        in_specs=[
            pl.BlockSpec((tm, x2d.shape[1]), lambda b, i: (b * nst + i, 0)),
            const(w.shape), const(qn.shape), const(wuqt.shape), const(kvn.shape), const(wuk.shape),
            const(wuvt.shape),
            pl.BlockSpec((16, tm), lambda b, i: (0, i)),
            pl.BlockSpec((16, tm), lambda b, i: (0, i)),
            pl.BlockSpec((tm, LANE), lambda b, i: (i, 0)),
            pl.BlockSpec((tm, LANE), lambda b, i: (i, 0)),
        ],
        out_specs=(
            pl.BlockSpec((1, MLA_HEADS * HEAD_PAD, tm), lambda b, i: (b, 0, i)),
            pl.BlockSpec((1, tm, MLA_HEADS * HEAD_PAD), lambda b, i: (b, i, 0)),
            pl.BlockSpec((1, MLA_HEADS * VT_ROWS, tm), lambda b, i: (b, 0, i)),
            pl.BlockSpec((tm, SWA_HEADS * SWA_D), lambda b, i: (b * nst + i, 0)),
            pl.BlockSpec((1, tm, 256), lambda b, i: (b, i, 0)),
            pl.BlockSpec((1, tm, 256), lambda b, i: (b, i, 0)),
        ),
        name="attn_inproj",
    )(x2d, w, qn, wuqt, kvn, wuk, wuvt, cos_t, sin_t, ccs, sn)


def _mla_flash_kernel(qt_ref, k_ref, vt_ref, ot_ref, st_sc, p_sc, *, tk):
    qt = qt_ref[0]
    tq = qt.shape[1]
    nkv = k_ref.shape[1] // tk
    assert nkv % 2 == 0

    def scores(j):
        k0 = pl.multiple_of(j * tk, tk)
        return _dot(k_ref[0, pl.ds(k0, tk), :], qt)

    def values(j):
        return vt_ref[0, :, pl.ds(pl.multiple_of(j * tk, tk), tk)]

    def phase(j, cur, m, acc):
        nxt = 1 - cur
        st_sc[nxt] = scores(jnp.minimum(j + 1, nkv - 1))
        pv = _dot(values(jnp.maximum(j - 1, 0)), p_sc[nxt])
        st = st_sc[cur]
        m_new = jnp.maximum(m, jnp.max(st, axis=0, keepdims=True))
        alpha = jnp.exp2(m - m_new)
        p_sc[cur] = jnp.exp2(st - m_new).astype(_BF16)
        return m_new, (acc + pv) * alpha

    def body(jj, carry):
        m, acc = carry
        m, acc = phase(2 * jj, 0, m, acc)
        m, acc = phase(2 * jj + 1, 1, m, acc)
        return m, acc

    st_sc[0] = scores(0)
    p_sc[1] = jnp.zeros(p_sc.shape[1:], _BF16)
    m0 = jnp.full((1, tq), -jnp.inf, _F32)
    a0 = jnp.zeros((VT_ROWS, tq), _F32)
    _, acc = lax.fori_loop(0, nkv // 2, body, (m0, a0))
    acc = acc + _dot(values(nkv - 1), p_sc[1])
    ot_ref[0] = (acc[0:MLA_V, :] / acc[MLA_V:MLA_V + 1, :]).astype(ot_ref.dtype)


def _mla_flash(qt, k, vt, tq, tk):
    bsz, _, seq = qt.shape
    return pl.pallas_call(
        functools.partial(_mla_flash_kernel, tk=tk),
        out_shape=jax.ShapeDtypeStruct((bsz, MLA_HEADS * MLA_V, seq), _BF16),
        grid=(bsz, MLA_HEADS, seq // tq),
        in_specs=[
            pl.BlockSpec((1, HEAD_PAD, tq), lambda b, h, i: (b, h, i)),
            pl.BlockSpec((1, seq, HEAD_PAD), lambda b, h, i: (b, 0, h)),
            pl.BlockSpec((1, VT_ROWS, seq), lambda b, h, i: (b, h, 0)),
        ],
        out_specs=pl.BlockSpec((1, MLA_V, tq), lambda b, h, i: (b, h, i)),
        scratch_shapes=[pltpu.VMEM((2, tk, tq), _F32), pltpu.VMEM((2, tk, tq), _BF16)],
        name="mla_flash",
    )(qt, k, vt)


def _swa_kernel(relb_ref, sinks_ref, q_ref, k_ref, v_ref, bucket_ref, o_ref, bias_sc, *, nblk, seq):
    first = jnp.logical_and(pl.program_id(0) == 0, pl.program_id(1) == 0)

    @pl.when(first)
    def _build_bias():
        bucket = bucket_ref[...]
        for h in range(SWA_HEADS):
            acc = jnp.full(bucket.shape, NEG_BIG, _F32)
            for bk in range(REL_BUCKETS):
                acc = jnp.where(bucket == bk, relb_ref[bk, h], acc)
            bias_sc[h] = acc

    j = pl.program_id(1)
    lo = lax.broadcasted_iota(jnp.int32, (BLOCK, LANE), 1) < SWA_D
    koff = lax.broadcasted_iota(jnp.int32, (1, 3 * BLOCK), 1)
    zero = jnp.zeros((BLOCK, LANE), _BF16)

    def block(u, c):
        n = j * nblk + u
        r0 = pl.multiple_of(u * BLOCK, BLOCK)
        w0 = pl.multiple_of(n * BLOCK, BLOCK)
        key_pos = n * BLOCK - BLOCK + koff
        emask = jnp.where(jnp.logical_and(key_pos >= 0, key_pos < seq), 0.0, NEG_BIG).astype(_F32)
        kw = k_ref[0, pl.ds(w0, 3 * BLOCK), :]
        vw = v_ref[0, pl.ds(w0, 3 * BLOCK), :]
        qblk = q_ref[pl.ds(r0, BLOCK), :]
        outs = []
        for g in range(SWA_KV_HEADS):
            k2 = kw[:, g * LANE:(g + 1) * LANE]
            v2 = vw[:, g * LANE:(g + 1) * LANE]
            p0 = qblk[:, g * 256:g * 256 + LANE]
            p1 = qblk[:, g * 256 + LANE:(g + 1) * 256]
            qs = jnp.concatenate([jnp.where(lo, p0, zero), jnp.where(lo, zero, p0),
                                  jnp.where(lo, p1, zero), jnp.where(lo, zero, p1)], axis=0)
            s = _dot_nt(qs, k2)
            bias = bias_sc[g * SWA_REP:(g + 1) * SWA_REP].reshape(SWA_REP * BLOCK, 3 * BLOCK)
            s = s + bias + emask
            sink = jnp.concatenate(
                [jnp.full((BLOCK, 1), sinks_ref[g * SWA_REP + r], _F32) for r in range(SWA_REP)], axis=0)
            m = jnp.maximum(jnp.max(s, axis=1, keepdims=True), sink)
            p = jnp.exp(s - m)
            den = jnp.sum(p, axis=1, keepdims=True) + jnp.exp(sink - m)
            o = _dot(p.astype(_BF16), v2) / den
            outs.append(jnp.where(lo, o[0:BLOCK], o[BLOCK:2 * BLOCK]))
            outs.append(jnp.where(lo, o[2 * BLOCK:3 * BLOCK], o[3 * BLOCK:4 * BLOCK]))
        o_ref[pl.ds(r0, BLOCK), :] = jnp.concatenate(outs, axis=1).astype(o_ref.dtype)
        return c

    lax.fori_loop(0, nblk, block, 0)


def _swa(qb, kpad, vpad, bucket, rel_bias, sinks, bsz, seq, nblk):
    nsteps = seq // (nblk * BLOCK)
    rows = nblk * BLOCK
    return pl.pallas_call(
        functools.partial(_swa_kernel, nblk=nblk, seq=seq),
        out_shape=jax.ShapeDtypeStruct((bsz * seq, SWA_HEADS * SWA_D), _BF16),
        grid=(bsz, nsteps),
        in_specs=[
            pl.BlockSpec(memory_space=pltpu.SMEM),
            pl.BlockSpec(memory_space=pltpu.SMEM),
            pl.BlockSpec((rows, SWA_HEADS * SWA_D), lambda b, j: (b * nsteps + j, 0)),
            pl.BlockSpec((1, seq + 2 * BLOCK, 256), lambda b, j: (b, 0, 0)),
            pl.BlockSpec((1, seq + 2 * BLOCK, 256), lambda b, j: (b, 0, 0)),
            pl.BlockSpec((BLOCK, 3 * BLOCK), lambda b, j: (0, 0)),
        ],
        out_specs=pl.BlockSpec((rows, SWA_HEADS * SWA_D), lambda b, j: (b * nsteps + j, 0)),
        scratch_shapes=[pltpu.VMEM((SWA_HEADS, BLOCK, 3 * BLOCK), _F32)],
        compiler_params=pltpu.CompilerParams(dimension_semantics=("arbitrary", "arbitrary")),
        name="swa",
    )(rel_bias, sinks, qb, kpad, vpad, bucket)


def _route(logits_t, rbias):
    sc = jax.nn.sigmoid(logits_t)
    bz = sc + rbias
    s_rows = [sc[e:e + 1, :] for e in range(N_EXPERTS)]
    b_rows = [bz[e:e + 1, :] for e in range(N_EXPERTS)]
    gsel = None
    best = None
    for g in range(N_GROUPS):
        r = b_rows[4 * g:4 * g + 4]
        gs = r[0] + r[1]
        for (i, k) in ((0, 2), (0, 3), (1, 2), (1, 3), (2, 3)):
            gs = jnp.maximum(gs, r[i] + r[k])
        if g == 0:
            gsel = jnp.zeros(gs.shape, jnp.int32)
            best = gs
        else:
            better = gs > best
            gsel = jnp.where(better, g, gsel)
            best = jnp.where(better, gs, best)

    def pick(rows, k):
        out = rows[12 + k]
        for g in (2, 1, 0):
            out = jnp.where(gsel == g, rows[4 * g + k], out)
        return out

    v = [pick(b_rows, k) for k in range(4)]
    s = [pick(s_rows, k) for k in range(4)]
    i1 = jnp.zeros(gsel.shape, jnp.int32)
    m1 = v[0]
    w1 = s[0]
    for k in range(1, 4):
        gt = v[k] > m1
        i1 = jnp.where(gt, k, i1)
        m1 = jnp.where(gt, v[k], m1)
        w1 = jnp.where(gt, s[k], w1)
    i2 = jnp.full(gsel.shape, -1, jnp.int32)
    m2 = jnp.full(m1.shape, -jnp.inf, _F32)
    w2 = jnp.zeros(m1.shape, _F32)
    for k in range(4):
        ok = jnp.logical_and(i1 != k, jnp.logical_or(i2 < 0, v[k] > m2))
        i2 = jnp.where(ok, k, i2)
        m2 = jnp.where(ok, v[k], m2)
        w2 = jnp.where(ok, s[k], w2)
    tot = w1 + w2
    g1 = w1 / tot
    g2 = w2 / tot
    first_lo = i1 < i2
    lo = jnp.where(first_lo, i1, i2)
    hi = jnp.where(first_lo, i2, i1)
    pair = jnp.where(lo == 0, hi - 1, jnp.where(lo == 1, hi + 1, 5))
    cls = gsel * 6 + pair
    return cls, jnp.where(first_lo, g1, g2), jnp.where(first_lo, g2, g1)


def _ln_router_store(z, g_ref, b_ref, rwt_ref, rb_ref, h_ref, cls_ref):
    h = _layer_norm(z, g_ref[...], b_ref[...])
    tm = h.shape[0]
    logits_t = lax.dot_general(rwt_ref[...], h, _NT_DIMS, precision=lax.Precision.HIGHEST,
                               preferred_element_type=_F32)
    cls, g_lo, g_hi = _route(logits_t, rb_ref[...])
    rows = jnp.concatenate([g_lo, g_hi, jnp.zeros((GATE_LANES - 2, tm), _F32)], axis=0)
    d = h.shape[1]
    h_ref[:, 0:d] = h
    h_ref[:, d:d + GATE_LANES] = rows.T
    cls_ref[0] = cls


def _outproj_ln_router_kernel(ot_ref, ob_ref, x_ref, wa_ref, wb_ref, g_ref, b_ref, rwt_ref, rb_ref,
                              h_ref, cls_ref):
    mixed = lax.dot_general(ot_ref[0], wa_ref[...], _TN_DIMS, preferred_element_type=_F32)
    mixed = mixed + _dot(ob_ref[...], wb_ref[...])
    z = ALPHA * x_ref[...] + mixed
    _ln_router_store(z, g_ref, b_ref, rwt_ref, rb_ref, h_ref, cls_ref)


def _outproj_ln_router(ot, ob, x2d, wa, wb, g, b, rwt, rb, bsz, seq, tm):
    nst = seq // tm
    d = x2d.shape[1]
    const = lambda shape: pl.BlockSpec(shape, lambda bb, i: (0,) * len(shape))
    return pl.pallas_call(
        _outproj_ln_router_kernel,
        out_shape=(jax.ShapeDtypeStruct((bsz * seq, d + GATE_LANES), _F32),
                   jax.ShapeDtypeStruct((bsz * nst, 1, tm), jnp.int32)),
        grid=(bsz, nst),
        in_specs=[
            pl.BlockSpec((1, ot.shape[1], tm), lambda bb, i: (bb, 0, i)),
            pl.BlockSpec((tm, ob.shape[1]), lambda bb, i: (bb * nst + i, 0)),
            pl.BlockSpec((tm, d), lambda bb, i: (bb * nst + i, 0)),
            const(wa.shape), const(wb.shape), const(g.shape), const(b.shape), const(rwt.shape), const(rb.shape),
        ],
        out_specs=(pl.BlockSpec((tm, d + GATE_LANES), lambda bb, i: (bb * nst + i, 0)),
                   pl.BlockSpec((1, 1, tm), lambda bb, i: (bb * nst + i, 0, 0))),
        name="outproj_ln_router",
    )(ot, ob, x2d, wa, wb, g, b, rwt, rb)


_DMA_UNROLL = 8


def _moe_kernel(ea_ref, eb_ref, nv_ref, src_ref, nsrc_ref, h_hbm, w1a, w3a, w2a, w1b, w3b, w2b, g_ref, b_ref,
                out_hbm, xbuf, obuf, sem_in, sem_out):
    i = pl.program_id(0)
    ntiles = pl.num_programs(0)
    tm = xbuf.shape[1]
    d = obuf.shape[1]
    nv = nv_ref[i]
    nv_prev = jnp.where(i > 0, nv_ref[jnp.maximum(i - 1, 0)], 0)
    nv_next = jnp.where(i + 1 < ntiles, nv_ref[jnp.minimum(i + 1, ntiles - 1)], 0)
    slot = lax.rem(i, 2)

    def gather_start(idx_ref, s):
        def group(gi, c):
            for u in range(_DMA_UNROLL):
                r = gi * _DMA_UNROLL + u
                pltpu.make_async_copy(h_hbm.at[idx_ref[0, 0, r]], xbuf.at[s, r], sem_in.at[s]).start()
            return c

        lax.fori_loop(0, tm // _DMA_UNROLL, group, 0)

    def scatter_row(r):
        return pltpu.make_async_copy(obuf.at[r], out_hbm.at[src_ref[0, 0, r]], sem_out)

    def scatter_wait(n):
        for bit in range(tm.bit_length()):
            rows = 1 << bit

            @pl.when(jnp.bitwise_and(n, rows) != 0)
            def _():
                pltpu.make_async_copy(obuf.at[pl.ds(0, rows)], out_hbm.at[pl.ds(0, rows)], sem_out).wait()

    @pl.when(jnp.logical_and(i == 0, nv > 0))
    def _prologue():
        gather_start(src_ref, 0)

    @pl.when(nv_next > 0)
    def _prefetch():
        gather_start(nsrc_ref, 1 - slot)

    @pl.when(nv > 0)
    def _tile():
        pltpu.make_async_copy(h_hbm.at[pl.ds(0, tm)], xbuf.at[slot], sem_in.at[slot]).wait()
        x = xbuf[slot, :, 0:d]
        gates = xbuf[slot, :, d:d + GATE_LANES]
        ga = gates[:, 0:1]
        gb = gates[:, 1:2]
        xb = x.astype(_BF16)

        def expert(w1, w3, w2):
            u = _dot(xb, w1[...])
            hh = u * _sigmoid(u) * _dot(xb, w3[...])
            return _dot(hh.astype(_BF16), w2[...])

        y = ga * expert(w1a, w3a, w2a) + gb * expert(w1b, w3b, w2b)
        z = _layer_norm(ALPHA * x + y, g_ref[...], b_ref[...])

        @pl.when(nv_prev > 0)
        def _drain_prev():
            scatter_wait(nv_prev)

        obuf[...] = z
        nfull = nv // _DMA_UNROLL

        def group(gi, c):
            for u in range(_DMA_UNROLL):
                scatter_row(gi * _DMA_UNROLL + u).start()
            return c

        def single(r, c):
            scatter_row(r).start()
            return c

        lax.fori_loop(0, nfull, group, 0)
        lax.fori_loop(nfull * _DMA_UNROLL, nv, single, 0)

        @pl.when(nv_next == 0)
        def _drain_last():
            scatter_wait(nv)


def _moe(he, ea, eb, nvalid, src, w1, w3, w2, g, b, n_tokens, d):
    ntiles = ea.shape[0]
    tm = MOE_TM
    dff = w1.shape[2]
    wspec_a = lambda shape: pl.BlockSpec((None,) + shape, lambda i, ea_r, eb_r, nv_r: (ea_r[i], 0, 0))
    wspec_b = lambda shape: pl.BlockSpec((None,) + shape, lambda i, ea_r, eb_r, nv_r: (eb_r[i], 0, 0))
    grid_spec = pltpu.PrefetchScalarGridSpec(
        num_scalar_prefetch=3,
        grid=(ntiles,),
        in_specs=[
            pl.BlockSpec((1, 1, tm), lambda i, *_: (i, 0, 0), memory_space=pltpu.SMEM),
            pl.BlockSpec((1, 1, tm), lambda i, *_: (jnp.minimum(i + 1, ntiles - 1), 0, 0), memory_space=pltpu.SMEM),
            pl.BlockSpec(memory_space=pl.ANY),
            wspec_a((d, dff)), wspec_a((d, dff)), wspec_a((dff, d)),
            wspec_b((d, dff)), wspec_b((d, dff)), wspec_b((dff, d)),
            pl.BlockSpec((1, d), lambda i, *_: (0, 0)),
            pl.BlockSpec((1, d), lambda i, *_: (0, 0)),
        ],
        out_specs=pl.BlockSpec(memory_space=pl.ANY),
        scratch_shapes=[
            pltpu.VMEM((2, tm, d + GATE_LANES), _F32),
            pltpu.VMEM((tm, d), _F32),
            pltpu.SemaphoreType.DMA((2,)),
            pltpu.SemaphoreType.DMA,
        ],
    )
    return pl.pallas_call(
        _moe_kernel,
        out_shape=jax.ShapeDtypeStruct((n_tokens, d), _F32),
        grid_spec=grid_spec,
        compiler_params=pltpu.CompilerParams(dimension_semantics=("arbitrary",)),
        name="moe",
    )(ea, eb, nvalid, src, src, he, w1, w3, w2, w1, w3, w2, g, b)


def _moe_plan(cls, n_tokens):
    tm = MOE_TM
    ntiles = n_tokens // tm + N_CLASSES
    cls = cls.reshape(n_tokens)
    order = jnp.argsort(cls, stable=True).astype(jnp.int32)
    counts = jnp.sum((cls[:, None] == jnp.arange(N_CLASSES, dtype=jnp.int32)[None, :]).astype(jnp.int32), axis=0)
    cstart = jnp.cumsum(counts) - counts
    ptiles = (counts + tm - 1) // tm
    tend = jnp.cumsum(ptiles)
    tstart = tend - ptiles
    tile = jnp.arange(ntiles, dtype=jnp.int32)
    tcls = jnp.minimum(jnp.sum((tile[:, None] >= tend[None, :]).astype(jnp.int32), axis=1), N_CLASSES - 1)
    used = tile < tend[-1]
    nvalid = jnp.where(used, jnp.clip(counts[tcls] - (tile - tstart[tcls]) * tm, 0, tm), 0).astype(jnp.int32)
    r = jnp.arange(tm, dtype=jnp.int32)[None, :]
    valid = r < nvalid[:, None]
    pos = jnp.clip((cstart[tcls] + (tile - tstart[tcls]) * tm)[:, None] + r, 0, n_tokens - 1)
    src = jnp.where(valid, order[pos], 0).astype(jnp.int32).reshape(ntiles, 1, tm)
    pair_lo = jnp.array([0, 0, 0, 1, 1, 2], jnp.int32)
    pair_hi = jnp.array([1, 2, 3, 2, 3, 3], jnp.int32)
    ea = (4 * (tcls // 6) + pair_lo[tcls % 6]).astype(jnp.int32)
    eb = (4 * (tcls // 6) + pair_hi[tcls % 6]).astype(jnp.int32)
    return ea, eb, nvalid, src


def _lru_inproj_kernel(x_ref, w_ref, gate_ref, xr_ref):
    u = _dot(x_ref[...].astype(_BF16), w_ref[...])
    half = gate_ref.shape[1]
    gate_ref[...] = u[:, 0:half]
    xr_ref[...] = u[:, half:2 * half]


def _lru_inproj(h2d, w, tm):
    n, d = h2d.shape
    half = w.shape[1] // 2
    return pl.pallas_call(
        _lru_inproj_kernel,
        out_shape=(jax.ShapeDtypeStruct((n, half), _F32), jax.ShapeDtypeStruct((n, half), _F32)),
        grid=(n // tm,),
        in_specs=[pl.BlockSpec((tm, d), lambda i: (i, 0)), pl.BlockSpec(w.shape, lambda i: (0, 0))],
        out_specs=(pl.BlockSpec((tm, half), lambda i: (i, 0)), pl.BlockSpec((tm, half), lambda i: (i, 0))),
        name="lru_inproj",
    )(h2d, w)


def _scan_block(a, b, h_in, reverse):
    tt = a.shape[0]
    sub = lax.broadcasted_iota(jnp.int32, a.shape, 0) % 8
    for s in (1, 2, 4):
        if reverse:
            keep = sub < 8 - s
            a_sh = pltpu.roll(a, tt - s, 0)
            b_sh = pltpu.roll(b, tt - s, 0)
        else:
            keep = sub >= s
            a_sh = pltpu.roll(a, s, 0)
            b_sh = pltpu.roll(b, s, 0)
        b = jnp.where(keep, a * b_sh + b, b)
        a = jnp.where(keep, a * a_sh, a)
    ngrp = tt // 8
    hb = jnp.broadcast_to(h_in, (8, a.shape[1]))
    outs = [None] * ngrp
    order = range(ngrp - 1, -1, -1) if reverse else range(ngrp)
    edge = 0 if reverse else 7
    for j in order:
        hj = a[8 * j:8 * j + 8, :] * hb + b[8 * j:8 * j + 8, :]
        outs[j] = hj
        hb = jnp.broadcast_to(hj[edge:edge + 1, :], hj.shape)
    return jnp.concatenate(outs, axis=0), hb[0:1, :]


def _rglru_kernel(*refs, reverse, final, ntile):
    if final:
        (xr_ref, prev_ref, next_ref, cw_ref, cb_ref, wa_ref, ba_ref, wx_ref, bx_ref, lam_ref,
         hf_ref, gate_ref, res_ref, wout_ref, g_ref, b_ref, rwt_ref, rb_ref,
         h_ref, cls_ref, xe_sc, carry_sc) = refs
    else:
        (xr_ref, prev_ref, next_ref, cw_ref, cb_ref, wa_ref, ba_ref, wx_ref, bx_ref, lam_ref,
         hout_ref, xe_sc, carry_sc) = refs
    i = pl.program_id(1)
    ti = (ntile - 1 - i) if reverse else i
    tt = xr_ref.shape[0]

    @pl.when(i == 0)
    def _reset():
        carry_sc[...] = jnp.zeros_like(carry_sc)

    xe_sc[0:8, :] = jnp.where(ti > 0, prev_ref[...], 0.0)
    xe_sc[8:8 + tt, :] = xr_ref[...]
    xe_sc[8 + tt:16 + tt, :] = jnp.where(ti < ntile - 1, next_ref[...], 0.0)
    cw = cw_ref[...]
    xc = (cb_ref[...] + cw[0:1, :] * xe_sc[6:6 + tt, :] + cw[1:2, :] * xe_sc[7:7 + tt, :]
          + cw[2:3, :] * xe_sc[8:8 + tt, :] + cw[3:4, :] * xe_sc[9:9 + tt, :])
    lam = lam_ref[...]
    log_sig = -jnp.log(1.0 + jnp.exp(-lam))
    ys = []
    for n in range(LRU_BLOCKS):
        sl = slice(n * LRU_BW, (n + 1) * LRU_BW)
        xcn = xc[:, sl]
        xb = xcn.astype(_BF16)
        r = _sigmoid(_dot(xb, wa_ref[n]) + ba_ref[:, sl])
        ig = _sigmoid(_dot(xb, wx_ref[n]) + bx_ref[:, sl])
        log_a = LRU_C * r * log_sig[:, sl]
        a = jnp.exp(log_a)
        th = jnp.tanh(log_a)
        bb = jnp.sqrt(-2.0 * th) * lax.rsqrt(1.0 - th) * (ig * xcn)
        h, last = _scan_block(a, bb, carry_sc[:, sl], reverse)
        carry_sc[:, sl] = last
        if final:
            ys.append((hf_ref[:, sl] + h) * jax.nn.gelu(gate_ref[:, sl], approximate=True))
        else:
            hout_ref[:, sl] = h
    if final:
        y = jnp.concatenate(ys, axis=1).astype(_BF16)
        z = ALPHA * res_ref[...] + _dot(y, wout_ref[...])
        _ln_router_store(z, g_ref, b_ref, rwt_ref, rb_ref, h_ref, cls_ref)


def _rglru(xr, cw, cb, wa, ba, wx, bx, lam, bsz, seq, tt, reverse, tail=None):
    ntile = seq // tt
    width = xr.shape[1]
    final = tail is not None
    tidx = (lambda i: ntile - 1 - i) if reverse else (lambda i: i)
    row = lambda b, i: (b * ntile + tidx(i), 0)
    g8 = tt // 8
    nb8 = seq // 8

    def prev_map(b, i):
        return (b * nb8 + jnp.maximum(tidx(i) * g8 - 1, 0), 0)

    def next_map(b, i):
        return (b * nb8 + jnp.minimum((tidx(i) + 1) * g8, nb8 - 1), 0)

    const = lambda shape: pl.BlockSpec(shape, lambda b, i: (0,) * len(shape))
    in_specs = [
        pl.BlockSpec((tt, width), row),
        pl.BlockSpec((8, width), prev_map),
        pl.BlockSpec((8, width), next_map),
        const(cw.shape), const(cb.shape), const(wa.shape), const(ba.shape), const(wx.shape), const(bx.shape),
        const(lam.shape),
    ]
    args = [xr, xr, xr, cw, cb, wa, ba, wx, bx, lam]
    scratch = [pltpu.VMEM((tt + 16, width), _F32), pltpu.VMEM((1, width), _F32)]
    if final:
        hf, gate, res, wout, g, b, rwt, rb = tail
        d = res.shape[1]
        in_specs += [pl.BlockSpec((tt, width), row), pl.BlockSpec((tt, width), row), pl.BlockSpec((tt, d), row),
                     const(wout.shape), const(g.shape), const(b.shape), const(rwt.shape), const(rb.shape)]
        args += [hf, gate, res, wout, g, b, rwt, rb]
        out_shape = (jax.ShapeDtypeStruct((bsz * seq, d + GATE_LANES), _F32),
                     jax.ShapeDtypeStruct((bsz * ntile, 1, tt), jnp.int32))
        out_specs = (pl.BlockSpec((tt, d + GATE_LANES), row),
                     pl.BlockSpec((1, 1, tt), lambda b, i: (b * ntile + tidx(i), 0, 0)))
    else:
        out_shape = jax.ShapeDtypeStruct((bsz * seq, width), _F32)
        out_specs = pl.BlockSpec((tt, width), row)
    return pl.pallas_call(
        functools.partial(_rglru_kernel, reverse=reverse, final=final, ntile=ntile),
        out_shape=out_shape,
        grid=(bsz, ntile),
        in_specs=in_specs,
        out_specs=out_specs,
        scratch_shapes=scratch,
        compiler_params=pltpu.CompilerParams(dimension_semantics=("arbitrary", "arbitrary")),
        name="rglru_bwd_tail" if final else "rglru_fwd",
    )(*args)


def _rotary_tables(seq):
    half = MLA_ROPE // 2
    inv_freq = ROPE_THETA ** (-jnp.arange(half, dtype=_F32) / half)
    ang = jnp.arange(seq, dtype=_F32)[:, None] * inv_freq[None, :]
    return jnp.cos(ang), jnp.sin(ang)


def _t5_bucket(rel):
    n_side = REL_BUCKETS // 2
    max_exact = n_side // 2
    dist = jnp.abs(rel)
    far = max_exact + (jnp.log(jnp.maximum(dist, 1).astype(_F32) / max_exact)
                       / math.log(REL_MAX_DIST / max_exact) * (n_side - max_exact)).astype(jnp.int32)
    far = jnp.minimum(far, n_side - 1)
    return jnp.where(rel > 0, n_side, 0) + jnp.where(dist < max_exact, dist, far)


def _band_buckets():
    rel = jnp.arange(3 * BLOCK)[None, :] - BLOCK - jnp.arange(BLOCK)[:, None]
    return jnp.where(jnp.abs(rel) <= WINDOW, _t5_bucket(rel), -1).astype(jnp.int32)


def _pick_tile(seq, want):
    t = min(want, seq)
    while seq % t:
        t //= 2
    return t


def kernel(x, rel_bias, router_w, router_bias, l0_w_in, l0_q_norm, l0_w_uq, l0_kv_norm, l0_w_ukv, l0_sinks, l0_w_out, l0_ln1_g, l0_ln1_b, l0_w1, l0_w3, l0_w2, l0_ln2_g, l0_ln2_b, l1_w_in, l1_conv_w, l1_conv_b, l1_wa_f, l1_ba_f, l1_wx_f, l1_bx_f, l1_lam_f, l1_wa_b, l1_ba_b, l1_wx_b, l1_bx_b, l1_lam_b, l1_w_out, l1_ln1_g, l1_ln1_b, l1_w1, l1_w3, l1_w2, l1_ln2_g, l1_ln2_b):
    bsz, seq, d = x.shape
    n_tok = bsz * seq
    assert seq % BLOCK == 0 and d == LRU_BLOCKS * LRU_BW and n_tok % MOE_TM == 0
    x2d = x.reshape(n_tok, d)
    row = lambda v: v.reshape(1, -1).astype(_F32)
    rwt = router_w.astype(_F32).T
    rb = router_bias.astype(_F32).reshape(N_EXPERTS, 1)

    o = np.cumsum([0, MLA_Q_RANK, MLA_KV_RANK, MLA_ROPE, SWA_HEADS * SWA_D, SWA_KV_HEADS * SWA_D,
                   SWA_KV_HEADS * SWA_D])
    w_q, w_kv, w_kr, w_qb, w_kb, w_vb = [l0_w_in[:, o[i]:o[i + 1]] for i in range(6)]
    dup = lambda w: jnp.concatenate([w[:, 0:64], w[:, 0:64], w[:, 64:128], w[:, 64:128]], axis=1)
    w_in0 = jnp.concatenate([w_q, w_kv, jnp.pad(w_kr, ((0, 0), (0, LANE - MLA_ROPE))), w_qb, dup(w_kb), dup(w_vb)],
                            axis=1).astype(_BF16)
    wuq = l0_w_uq.reshape(MLA_Q_RANK, MLA_HEADS, MLA_QK)
    wuqt = jnp.pad(wuq, ((0, 0), (0, 0), (0, HEAD_PAD - MLA_QK))).reshape(MLA_Q_RANK, MLA_HEADS * HEAD_PAD).T
    wukv = l0_w_ukv.reshape(MLA_KV_RANK, MLA_HEADS, MLA_NOPE + MLA_V)
    wuk = jnp.pad(wukv[:, :, :MLA_NOPE], ((0, 0), (0, 0), (0, HEAD_PAD - MLA_NOPE))).reshape(MLA_KV_RANK, -1)
    wuvt = wukv[:, :, MLA_NOPE:].reshape(MLA_KV_RANK, MLA_HEADS * MLA_V).T
    cos, sin = _rotary_tables(seq)
    ccs = jnp.pad(jnp.concatenate([cos, cos], axis=1), ((0, 0), (0, LANE - MLA_ROPE)))
    sn = jnp.pad(jnp.concatenate([-sin, sin], axis=1), ((0, 0), (0, LANE - MLA_ROPE)))

    tm = _pick_tile(seq, 512)
    qt, k, vt, qb, kb, vb = _attn_inproj(
        x2d, w_in0, row(l0_q_norm), wuqt.astype(_BF16), row(l0_kv_norm), wuk.astype(_BF16), wuvt.astype(_BF16),
        cos.T, sin.T, ccs, sn, bsz, seq, tm)
    ot = _mla_flash(qt, k, vt, _pick_tile(seq, 512), _pick_tile(seq // 2, 512))
    pad = ((0, 0), (BLOCK, BLOCK), (0, 0))
    ob = _swa(qb, jnp.pad(kb, pad), jnp.pad(vb, pad), _band_buckets(), rel_bias.astype(_F32),
              l0_sinks.astype(_F32), bsz, seq, _pick_tile(seq // BLOCK, 8))
    n_a = MLA_HEADS * MLA_V
    he, cls = _outproj_ln_router(ot, ob, x2d, l0_w_out[:n_a].astype(_BF16), l0_w_out[n_a:].astype(_BF16),
                                 row(l0_ln1_g), row(l0_ln1_b), rwt, rb, bsz, seq, tm)
    ea, eb, nvalid, src = _moe_plan(cls, n_tok)
    h = _moe(he, ea, eb, nvalid, src, l0_w1.astype(_BF16), l0_w3.astype(_BF16), l0_w2.astype(_BF16),
             row(l0_ln2_g), row(l0_ln2_b), n_tok, d)

    gate, xr = _lru_inproj(h, l1_w_in.astype(_BF16), tm)
    tt = _pick_tile(seq, 512)
    cw = l1_conv_w.astype(_F32)
    cb = row(l1_conv_b)
    hf = _rglru(xr, cw, cb, l1_wa_f.astype(_BF16), row(l1_ba_f), l1_wx_f.astype(_BF16), row(l1_bx_f),
                row(l1_lam_f), bsz, seq, tt, reverse=False)
    he, cls = _rglru(xr, cw, cb, l1_wa_b.astype(_BF16), row(l1_ba_b), l1_wx_b.astype(_BF16), row(l1_bx_b),
                     row(l1_lam_b), bsz, seq, tt, reverse=True,
                     tail=(hf, gate, h, l1_w_out.astype(_BF16), row(l1_ln1_g), row(l1_ln1_b), rwt, rb))
    ea, eb, nvalid, src = _moe_plan(cls, n_tok)
    h = _moe(he, ea, eb, nvalid, src, l1_w1.astype(_BF16), l1_w3.astype(_BF16), l1_w2.astype(_BF16),
             row(l1_ln2_g), row(l1_ln2_b), n_tok, d)
    return h.reshape(bsz, seq, d)
```

```python
import functools
import math

import jax
import jax.numpy as jnp
import numpy as np
from jax import lax
from jax.experimental import pallas as pl
from jax.experimental.pallas import tpu as pltpu

MLA_HEADS = 8
MLA_Q_RANK = 256
MLA_KV_RANK = 128
MLA_NOPE = 64
MLA_ROPE = 32
MLA_V = 64
MLA_QK = MLA_NOPE + MLA_ROPE
ROPE_THETA = 10000.0
SWA_HEADS = 8
SWA_KV_HEADS = 2
SWA_REP = SWA_HEADS // SWA_KV_HEADS
SWA_D = 64
WINDOW = 128
BLOCK = 128
REL_BUCKETS = 32
REL_MAX_DIST = 128
LRU_BLOCKS = 8
LRU_BW = 128
LRU_C = 8.0
N_EXPERTS = 16
N_GROUPS = 4
EXPERTS_PER_GROUP = 4
N_CLASSES = N_GROUPS * 6
DEPTH = 2
ALPHA = (2.0 * DEPTH) ** 0.25
LN_EPS = 1e-5
RMS_EPS = 1e-6
NEG_BIG = -1e30

LANE = 128
HEAD_PAD = 128
GATE_LANES = 128
VT_ROWS = 80
_LOG2E = math.log2(math.e)
MOE_TM = 256

_F32 = jnp.float32
_BF16 = jnp.bfloat16
_NT_DIMS = (((1,), (1,)), ((), ()))
_TN_DIMS = (((0,), (0,)), ((), ()))


def _dot(a, b):
    return jnp.dot(a, b, preferred_element_type=_F32)


def _dot_nt(a, b):
    return lax.dot_general(a, b, _NT_DIMS, preferred_element_type=_F32)


def _sigmoid(x):
    return 0.5 * jnp.tanh(0.5 * x) + 0.5


def _rms(x, g):
    return x * lax.rsqrt(jnp.mean(jnp.square(x), -1, keepdims=True) + RMS_EPS) * g


def _layer_norm(x, g, b):
    mu = jnp.mean(x, -1, keepdims=True)
    xc = x - mu
    var = jnp.mean(jnp.square(xc), -1, keepdims=True)
    return xc * lax.rsqrt(var + LN_EPS) * g + b


def _attn_inproj_kernel(x_ref, w_ref, qn_ref, wuqt_ref, kvn_ref, wuk_ref, wuvt_ref,
                        cost_ref, sint_ref, ccs_ref, sn_ref,
                        qt_ref, k_ref, vt_ref, qb_ref, kb_ref, vb_ref):
    xb = x_ref[...].astype(_BF16)
    proj = _dot(xb, w_ref[...])
    q_lat = proj[:, 0:256]
    kv_lat = proj[:, 256:384]
    kr = proj[:, 384:512]
    qn = _rms(q_lat, qn_ref[...]).astype(_BF16)
    qt = _dot_nt(wuqt_ref[...], qn) * (MLA_QK ** -0.5 * _LOG2E)
    cos_t = cost_ref[...]
    sin_t = sint_ref[...]
    tm = qt.shape[1]
    zpad = jnp.zeros((HEAD_PAD - MLA_QK, tm), _F32)
    for h in range(MLA_HEADS):
        r0 = h * HEAD_PAD
        x1 = qt[r0 + 64:r0 + 80, :]
        x2 = qt[r0 + 80:r0 + 96, :]
        blk = jnp.concatenate([qt[r0:r0 + 64, :], x1 * cos_t - x2 * sin_t, x1 * sin_t + x2 * cos_t, zpad], axis=0)
        qt_ref[0, r0:r0 + HEAD_PAD, :] = blk.astype(_BF16)
    kvn = _rms(kv_lat, kvn_ref[...]).astype(_BF16)
    kn = _dot(kvn, wuk_ref[...])
    lane = lax.broadcasted_iota(jnp.int32, kr.shape, 1)
    swapped = jnp.where(lane < 16, pltpu.roll(kr, 112, 1), pltpu.roll(kr, 16, 1))
    kpe = kr * ccs_ref[...] + swapped * sn_ref[...]
    kpe = pltpu.roll(kpe, 64, 1)
    k_ref[0] = (kn + jnp.concatenate([kpe] * MLA_HEADS, axis=1)).astype(_BF16)
    vt = _dot_nt(wuvt_ref[...], kvn).astype(_BF16)
    extra = (lax.broadcasted_iota(jnp.int32, (VT_ROWS - MLA_V, tm), 0) == 0).astype(_BF16)
    for h in range(MLA_HEADS):
        vt_ref[0, h * VT_ROWS:h * VT_ROWS + MLA_V, :] = vt[h * MLA_V:(h + 1) * MLA_V, :]
        vt_ref[0, h * VT_ROWS + MLA_V:(h + 1) * VT_ROWS, :] = extra
    qb_ref[...] = (proj[:, 512:1024] * (SWA_D ** -0.5)).astype(_BF16)
    kb_ref[0] = proj[:, 1024:1280].astype(_BF16)
    vb_ref[0] = proj[:, 1280:1536].astype(_BF16)


def _attn_inproj(x2d, w, qn, wuqt, kvn, wuk, wuvt, cos_t, sin_t, ccs, sn, bsz, seq, tm):
    nst = seq // tm
    const = lambda shape: pl.BlockSpec(shape, lambda b, i: (0,) * len(shape))
    out_shape = (
        jax.ShapeDtypeStruct((bsz, MLA_HEADS * HEAD_PAD, seq), _BF16),
        jax.ShapeDtypeStruct((bsz, seq, MLA_HEADS * HEAD_PAD), _BF16),
        jax.ShapeDtypeStruct((bsz, MLA_HEADS * VT_ROWS, seq), _BF16),
        jax.ShapeDtypeStruct((bsz * seq, SWA_HEADS * SWA_D), _BF16),
        jax.ShapeDtypeStruct((bsz, seq, 256), _BF16),
        jax.ShapeDtypeStruct((bsz, seq, 256), _BF16),
    )
    return pl.pallas_call(
        _attn_inproj_kernel,
        out_shape=out_shape,
        grid=(bsz, nst),
        in_specs=[
            pl.BlockSpec((tm, x2d.shape[1]), lambda b, i: (b * nst + i, 0)),
            const(w.shape), const(qn.shape), const(wuqt.shape), const(kvn.shape), const(wuk.shape),
            const(wuvt.shape),
            pl.BlockSpec((16, tm), lambda b, i: (0, i)),
            pl.BlockSpec((16, tm), lambda b, i: (0, i)),
            pl.BlockSpec((tm, LANE), lambda b, i: (i, 0)),
            pl.BlockSpec((tm, LANE), lambda b, i: (i, 0)),
        ],
        out_specs=(
            pl.BlockSpec((1, MLA_HEADS * HEAD_PAD, tm), lambda b, i: (b, 0, i)),
            pl.BlockSpec((1, tm, MLA_HEADS * HEAD_PAD), lambda b, i: (b, i, 0)),
            pl.BlockSpec((1, MLA_HEADS * VT_ROWS, tm), lambda b, i: (b, 0, i)),
            pl.BlockSpec((tm, SWA_HEADS * SWA_D), lambda b, i: (b * nst + i, 0)),
            pl.BlockSpec((1, tm, 256), lambda b, i: (b, i, 0)),
            pl.BlockSpec((1, tm, 256), lambda b, i: (b, i, 0)),
        ),
        name="attn_inproj",
    )(x2d, w, qn, wuqt, kvn, wuk, wuvt, cos_t, sin_t, ccs, sn)


def _mla_flash_kernel(qt_ref, k_ref, vt_ref, ot_ref, *scratch, tk, nstream):
    st_sc = scratch[:2 * nstream]
    p_sc = scratch[2 * nstream:4 * nstream]
    tq = qt_ref.shape[2] // nstream
    nkv = k_ref.shape[1] // tk
    assert nkv % 2 == 0
    qts = [qt_ref[0, :, s * tq:(s + 1) * tq] for s in range(nstream)]

    def keys(j):
        return k_ref[0, pl.ds(pl.multiple_of(j * tk, tk), tk), :]

    def values(j):
        return vt_ref[0, :, pl.ds(pl.multiple_of(j * tk, tk), tk)]

    def phase(j, cur, state):
        nxt = 1 - cur
        kt = keys(jnp.minimum(j + 1, nkv - 1))
        vt = values(jnp.maximum(j - 1, 0))
        out = []
        for s in range(nstream):
            m, acc = state[s]
            st_sc[2 * s + nxt][...] = _dot(kt, qts[s])
            pv = _dot(vt, p_sc[2 * s + nxt][...])
            st = st_sc[2 * s + cur][...]
            m_new = jnp.maximum(m, jnp.max(st, axis=0, keepdims=True))
            alpha = jnp.exp2(m - m_new)
            p_sc[2 * s + cur][...] = jnp.exp2(st - m_new).astype(_BF16)
            out.append((m_new, (acc + pv) * alpha))
        return out

    def body(jj, state):
        state = phase(2 * jj, 0, state)
        return phase(2 * jj + 1, 1, state)

    k0 = keys(0)
    state = []
    for s in range(nstream):
        st_sc[2 * s][...] = _dot(k0, qts[s])
        p_sc[2 * s + 1][...] = jnp.zeros((tk, tq), _BF16)
        state.append((jnp.full((1, tq), -jnp.inf, _F32), jnp.zeros((VT_ROWS, tq), _F32)))
    state = lax.fori_loop(0, nkv // 2, body, state)
    v_last = values(nkv - 1)
    for s in range(nstream):
        acc = state[s][1] + _dot(v_last, p_sc[2 * s + 1][...])
        ot_ref[0, :, s * tq:(s + 1) * tq] = (acc[0:MLA_V, :] / acc[MLA_V:MLA_V + 1, :]).astype(ot_ref.dtype)


def _mla_flash(qt, k, vt, tq, tk, nstream):
    bsz, _, seq = qt.shape
    tqs = tq * nstream
    scratch = [pltpu.VMEM((tk, tq), _F32)] * (2 * nstream) + [pltpu.VMEM((tk, tq), _BF16)] * (2 * nstream)
    return pl.pallas_call(
        functools.partial(_mla_flash_kernel, tk=tk, nstream=nstream),
        out_shape=jax.ShapeDtypeStruct((bsz, MLA_HEADS * MLA_V, seq), _BF16),
        grid=(bsz, MLA_HEADS, seq // tqs),
        in_specs=[
            pl.BlockSpec((1, HEAD_PAD, tqs), lambda b, h, i: (b, h, i)),
            pl.BlockSpec((1, seq, HEAD_PAD), lambda b, h, i: (b, 0, h)),
            pl.BlockSpec((1, VT_ROWS, seq), lambda b, h, i: (b, h, 0)),
        ],
        out_specs=pl.BlockSpec((1, MLA_V, tqs), lambda b, h, i: (b, h, i)),
        scratch_shapes=scratch,
        name="mla_flash",
    )(qt, k, vt)


def _swa_kernel(relb_ref, sinks_ref, q_ref, k_ref, v_ref, bucket_ref, o_ref, bias_sc, *, nblk, seq):
    first = jnp.logical_and(pl.program_id(0) == 0, pl.program_id(1) == 0)

    @pl.when(first)
    def _build_bias():
        bucket = bucket_ref[...]
        for h in range(SWA_HEADS):
            acc = jnp.full(bucket.shape, NEG_BIG, _F32)
            for bk in range(REL_BUCKETS):
                acc = jnp.where(bucket == bk, relb_ref[bk, h], acc)
            bias_sc[h] = acc

    j = pl.program_id(1)
    lo = lax.broadcasted_iota(jnp.int32, (BLOCK, LANE), 1) < SWA_D
    koff = lax.broadcasted_iota(jnp.int32, (1, 3 * BLOCK), 1)
    zero = jnp.zeros((BLOCK, LANE), _BF16)

    def block(u, c):
        n = j * nblk + u
        r0 = pl.multiple_of(u * BLOCK, BLOCK)
        w0 = pl.multiple_of(n * BLOCK, BLOCK)
        key_pos = n * BLOCK - BLOCK + koff
        emask = jnp.where(jnp.logical_and(key_pos >= 0, key_pos < seq), 0.0, NEG_BIG).astype(_F32)
        kw = k_ref[0, pl.ds(w0, 3 * BLOCK), :]
        vw = v_ref[0, pl.ds(w0, 3 * BLOCK), :]
        qblk = q_ref[pl.ds(r0, BLOCK), :]
        outs = []
        for g in range(SWA_KV_HEADS):
            k2 = kw[:, g * LANE:(g + 1) * LANE]
            v2 = vw[:, g * LANE:(g + 1) * LANE]
            p0 = qblk[:, g * 256:g * 256 + LANE]
            p1 = qblk[:, g * 256 + LANE:(g + 1) * 256]
            qs = jnp.concatenate([jnp.where(lo, p0, zero), jnp.where(lo, zero, p0),
                                  jnp.where(lo, p1, zero), jnp.where(lo, zero, p1)], axis=0)
            s = _dot_nt(qs, k2)
            bias = bias_sc[g * SWA_REP:(g + 1) * SWA_REP].reshape(SWA_REP * BLOCK, 3 * BLOCK)
            s = s + bias + emask
            sink = jnp.concatenate(
                [jnp.full((BLOCK, 1), sinks_ref[g * SWA_REP + r], _F32) for r in range(SWA_REP)], axis=0)
            m = jnp.maximum(jnp.max(s, axis=1, keepdims=True), sink)
            p = jnp.exp(s - m)
            den = jnp.sum(p, axis=1, keepdims=True) + jnp.exp(sink - m)
            o = _dot(p.astype(_BF16), v2) / den
            outs.append(jnp.where(lo, o[0:BLOCK], o[BLOCK:2 * BLOCK]))
            outs.append(jnp.where(lo, o[2 * BLOCK:3 * BLOCK], o[3 * BLOCK:4 * BLOCK]))
        o_ref[pl.ds(r0, BLOCK), :] = jnp.concatenate(outs, axis=1).astype(o_ref.dtype)
        return c

    lax.fori_loop(0, nblk, block, 0)


def _swa(qb, kpad, vpad, bucket, rel_bias, sinks, bsz, seq, nblk):
    nsteps = seq // (nblk * BLOCK)
    rows = nblk * BLOCK
    return pl.pallas_call(
        functools.partial(_swa_kernel, nblk=nblk, seq=seq),
        out_shape=jax.ShapeDtypeStruct((bsz * seq, SWA_HEADS * SWA_D), _BF16),
        grid=(bsz, nsteps),
        in_specs=[
            pl.BlockSpec(memory_space=pltpu.SMEM),
            pl.BlockSpec(memory_space=pltpu.SMEM),
            pl.BlockSpec((rows, SWA_HEADS * SWA_D), lambda b, j: (b * nsteps + j, 0)),
            pl.BlockSpec((1, seq + 2 * BLOCK, 256), lambda b, j: (b, 0, 0)),
            pl.BlockSpec((1, seq + 2 * BLOCK, 256), lambda b, j: (b, 0, 0)),
            pl.BlockSpec((BLOCK, 3 * BLOCK), lambda b, j: (0, 0)),
        ],
        out_specs=pl.BlockSpec((rows, SWA_HEADS * SWA_D), lambda b, j: (b * nsteps + j, 0)),
        scratch_shapes=[pltpu.VMEM((SWA_HEADS, BLOCK, 3 * BLOCK), _F32)],
        compiler_params=pltpu.CompilerParams(dimension_semantics=("arbitrary", "arbitrary")),
        name="swa",
    )(rel_bias, sinks, qb, kpad, vpad, bucket)


def _route(logits_t, rbias):
    sc = jax.nn.sigmoid(logits_t)
    bz = sc + rbias
    s_rows = [sc[e:e + 1, :] for e in range(N_EXPERTS)]
    b_rows = [bz[e:e + 1, :] for e in range(N_EXPERTS)]
    gsel = None
    best = None
    for g in range(N_GROUPS):
        r = b_rows[4 * g:4 * g + 4]
        gs = r[0] + r[1]
        for (i, k) in ((0, 2), (0, 3), (1, 2), (1, 3), (2, 3)):
            gs = jnp.maximum(gs, r[i] + r[k])
        if g == 0:
            gsel = jnp.zeros(gs.shape, jnp.int32)
            best = gs
        else:
            better = gs > best
            gsel = jnp.where(better, g, gsel)
            best = jnp.where(better, gs, best)

    def pick(rows, k):
        out = rows[12 + k]
        for g in (2, 1, 0):
            out = jnp.where(gsel == g, rows[4 * g + k], out)
        return out

    v = [pick(b_rows, k) for k in range(4)]
    s = [pick(s_rows, k) for k in range(4)]
    i1 = jnp.zeros(gsel.shape, jnp.int32)
    m1 = v[0]
    w1 = s[0]
    for k in range(1, 4):
        gt = v[k] > m1
        i1 = jnp.where(gt, k, i1)
        m1 = jnp.where(gt, v[k], m1)
        w1 = jnp.where(gt, s[k], w1)
    i2 = jnp.full(gsel.shape, -1, jnp.int32)
    m2 = jnp.full(m1.shape, -jnp.inf, _F32)
    w2 = jnp.zeros(m1.shape, _F32)
    for k in range(4):
        ok = jnp.logical_and(i1 != k, jnp.logical_or(i2 < 0, v[k] > m2))
        i2 = jnp.where(ok, k, i2)
        m2 = jnp.where(ok, v[k], m2)
        w2 = jnp.where(ok, s[k], w2)
    tot = w1 + w2
    g1 = w1 / tot
    g2 = w2 / tot
    first_lo = i1 < i2
    lo = jnp.where(first_lo, i1, i2)
    hi = jnp.where(first_lo, i2, i1)
    pair = jnp.where(lo == 0, hi - 1, jnp.where(lo == 1, hi + 1, 5))
    cls = gsel * 6 + pair
    return cls, jnp.where(first_lo, g1, g2), jnp.where(first_lo, g2, g1)


def _ln_router_store(z, g_ref, b_ref, rwt_ref, rb_ref, h_ref, cls_ref):
    h = _layer_norm(z, g_ref[...], b_ref[...])
    tm = h.shape[0]
    logits_t = lax.dot_general(rwt_ref[...], h, _NT_DIMS, precision=lax.Precision.HIGHEST,
                               preferred_element_type=_F32)
    cls, g_lo, g_hi = _route(logits_t, rb_ref[...])
    rows = jnp.concatenate([g_lo, g_hi, jnp.zeros((GATE_LANES - 2, tm), _F32)], axis=0)
    d = h.shape[1]
    h_ref[:, 0:d] = h
    h_ref[:, d:d + GATE_LANES] = rows.T
    cls_ref[0] = cls


def _outproj_ln_router_kernel(ot_ref, ob_ref, x_ref, wa_ref, wb_ref, g_ref, b_ref, rwt_ref, rb_ref,
                              h_ref, cls_ref):
    mixed = lax.dot_general(ot_ref[0], wa_ref[...], _TN_DIMS, preferred_element_type=_F32)
    mixed = mixed + _dot(ob_ref[...], wb_ref[...])
    z = ALPHA * x_ref[...] + mixed
    _ln_router_store(z, g_ref, b_ref, rwt_ref, rb_ref, h_ref, cls_ref)


def _outproj_ln_router(ot, ob, x2d, wa, wb, g, b, rwt, rb, bsz, seq, tm):
    nst = seq // tm
    d = x2d.shape[1]
    const = lambda shape: pl.BlockSpec(shape, lambda bb, i: (0,) * len(shape))
    return pl.pallas_call(
        _outproj_ln_router_kernel,
        out_shape=(jax.ShapeDtypeStruct((bsz * seq, d + GATE_LANES), _F32),
                   jax.ShapeDtypeStruct((bsz * nst, 1, tm), jnp.int32)),
        grid=(bsz, nst),
        in_specs=[
            pl.BlockSpec((1, ot.shape[1], tm), lambda bb, i: (bb, 0, i)),
            pl.BlockSpec((tm, ob.shape[1]), lambda bb, i: (bb * nst + i, 0)),
            pl.BlockSpec((tm, d), lambda bb, i: (bb * nst + i, 0)),
            const(wa.shape), const(wb.shape), const(g.shape), const(b.shape), const(rwt.shape), const(rb.shape),
        ],
        out_specs=(pl.BlockSpec((tm, d + GATE_LANES), lambda bb, i: (bb * nst + i, 0)),
                   pl.BlockSpec((1, 1, tm), lambda bb, i: (bb * nst + i, 0, 0))),
        name="outproj_ln_router",
    )(ot, ob, x2d, wa, wb, g, b, rwt, rb)


_SUBLANES = 8
_DMA_CHUNK = 32


def _moe_kernel(ea_ref, eb_ref, nv_ref, src_ref, nsrc_ref, h_hbm, w1a, w3a, w2a, w1b, w3b, w2b, g_ref, b_ref,
                out_hbm, xbuf, obuf, sem_in, sem_out):
    i = pl.program_id(0)
    ntiles = pl.num_programs(0)
    tm = xbuf.shape[1] * _SUBLANES
    d = obuf.shape[2]
    nv = nv_ref[i]
    nv_prev = jnp.where(i > 0, nv_ref[jnp.maximum(i - 1, 0)], 0)
    nv_next = jnp.where(i + 1 < ntiles, nv_ref[jnp.minimum(i + 1, ntiles - 1)], 0)
    slot = lax.rem(i, 2)

    def hbm_row(ref, idx):
        return ref.at[lax.shift_right_logical(idx, 3), jnp.bitwise_and(idx, _SUBLANES - 1)]

    def gather_start(idx_ref, s):
        for r in range(tm):
            idx = idx_ref[0, 0, r]
            pltpu.make_async_copy(hbm_row(h_hbm, idx), xbuf.at[s, r // _SUBLANES, r % _SUBLANES], sem_in.at[s]).start()

    def scatter_wait(n):
        for bit in range(tm.bit_length()):
            rows = 1 << bit

            @pl.when(jnp.bitwise_and(n, rows) != 0)
            def _():
                if rows >= _SUBLANES:
                    grp = pl.ds(0, rows // _SUBLANES)
                    pltpu.make_async_copy(obuf.at[grp], out_hbm.at[grp], sem_out).wait()
                else:
                    pltpu.make_async_copy(obuf.at[0, pl.ds(0, rows)], out_hbm.at[0, pl.ds(0, rows)], sem_out).wait()

    @pl.when(jnp.logical_and(i == 0, nv > 0))
    def _prologue():
        gather_start(src_ref, 0)

    @pl.when(nv_next > 0)
    def _prefetch():
        gather_start(nsrc_ref, 1 - slot)

    @pl.when(nv > 0)
    def _tile():
        pltpu.make_async_copy(h_hbm.at[pl.ds(0, tm // _SUBLANES)], xbuf.at[slot], sem_in.at[slot]).wait()
        xg = xbuf[slot].reshape(tm, d + GATE_LANES)
        x = xg[:, 0:d]
        gates = xg[:, d:d + GATE_LANES]
        ga = gates[:, 0:1]
        gb = gates[:, 1:2]
        xb = x.astype(_BF16)

        def expert(w1, w3, w2):
            u = _dot(xb, w1[...])
            hh = u * _sigmoid(u) * _dot(xb, w3[...])
            return _dot(hh.astype(_BF16), w2[...])

        y = ga * expert(w1a, w3a, w2a) + gb * expert(w1b, w3b, w2b)
        z = _layer_norm(ALPHA * x + y, g_ref[...], b_ref[...])

        @pl.when(nv_prev > 0)
        def _drain_prev():
            scatter_wait(nv_prev)

        obuf[...] = z.reshape(tm // _SUBLANES, _SUBLANES, d)
        for c in range(tm // _DMA_CHUNK):
            @pl.when(nv >= (c + 1) * _DMA_CHUNK)
            def _chunk():
                for r in range(c * _DMA_CHUNK, (c + 1) * _DMA_CHUNK):
                    idx = src_ref[0, 0, r]
                    pltpu.make_async_copy(obuf.at[r // _SUBLANES, r % _SUBLANES], hbm_row(out_hbm, idx),
                                          sem_out).start()

        def group(gi, c):
            for u in range(_SUBLANES):
                idx = src_ref[0, 0, gi * _SUBLANES + u]
                pltpu.make_async_copy(obuf.at[gi, u], hbm_row(out_hbm, idx), sem_out).start()
            return c

        def single(r, c):
            idx = src_ref[0, 0, r]
            pltpu.make_async_copy(hbm_row(obuf, r), hbm_row(out_hbm, idx), sem_out).start()
            return c

        ngroups = lax.shift_right_logical(nv, 3)
        lax.fori_loop((nv // _DMA_CHUNK) * (_DMA_CHUNK // _SUBLANES), ngroups, group, 0)
        lax.fori_loop(ngroups * _SUBLANES, nv, single, 0)

        @pl.when(nv_next == 0)
        def _drain_last():
            scatter_wait(nv)


def _moe(he, ea, eb, nvalid, src, w1, w3, w2, g, b, n_tokens, d):
    ntiles = ea.shape[0]
    tm = MOE_TM
    dff = w1.shape[2]
    wspec_a = lambda shape: pl.BlockSpec((None,) + shape, lambda i, ea_r, eb_r, nv_r: (ea_r[i], 0, 0))
    wspec_b = lambda shape: pl.BlockSpec((None,) + shape, lambda i, ea_r, eb_r, nv_r: (eb_r[i], 0, 0))
    grid_spec = pltpu.PrefetchScalarGridSpec(
        num_scalar_prefetch=3,
        grid=(ntiles,),
        in_specs=[
            pl.BlockSpec((1, 1, tm), lambda i, *_: (i, 0, 0), memory_space=pltpu.SMEM),
            pl.BlockSpec((1, 1, tm), lambda i, *_: (jnp.minimum(i + 1, ntiles - 1), 0, 0), memory_space=pltpu.SMEM),
            pl.BlockSpec(memory_space=pl.ANY),
            wspec_a((d, dff)), wspec_a((d, dff)), wspec_a((dff, d)),
            wspec_b((d, dff)), wspec_b((d, dff)), wspec_b((dff, d)),
            pl.BlockSpec((1, d), lambda i, *_: (0, 0)),
            pl.BlockSpec((1, d), lambda i, *_: (0, 0)),
        ],
        out_specs=pl.BlockSpec(memory_space=pl.ANY),
        scratch_shapes=[
            pltpu.VMEM((2, tm // _SUBLANES, _SUBLANES, d + GATE_LANES), _F32),
            pltpu.VMEM((tm // _SUBLANES, _SUBLANES, d), _F32),
            pltpu.SemaphoreType.DMA((2,)),
            pltpu.SemaphoreType.DMA,
        ],
    )
    he3 = he.reshape(n_tokens // _SUBLANES, _SUBLANES, d + GATE_LANES)
    out = pl.pallas_call(
        _moe_kernel,
        out_shape=jax.ShapeDtypeStruct((n_tokens // _SUBLANES, _SUBLANES, d), _F32),
        grid_spec=grid_spec,
        compiler_params=pltpu.CompilerParams(dimension_semantics=("arbitrary",)),
        name="moe",
    )(ea, eb, nvalid, src, src, he3, w1, w3, w2, w1, w3, w2, g, b)
    return out.reshape(n_tokens, d)


def _moe_plan(cls, n_tokens):
    tm = MOE_TM
    ntiles = n_tokens // tm + N_CLASSES
    cls = cls.reshape(n_tokens)
    order = jnp.argsort(cls, stable=True).astype(jnp.int32)
    counts = jnp.sum((cls[:, None] == jnp.arange(N_CLASSES, dtype=jnp.int32)[None, :]).astype(jnp.int32), axis=0)
    cstart = jnp.cumsum(counts) - counts
    ptiles = (counts + tm - 1) // tm
    tend = jnp.cumsum(ptiles)
    tstart = tend - ptiles
    tile = jnp.arange(ntiles, dtype=jnp.int32)
    tcls = jnp.minimum(jnp.sum((tile[:, None] >= tend[None, :]).astype(jnp.int32), axis=1), N_CLASSES - 1)
    used = tile < tend[-1]
    nvalid = jnp.where(used, jnp.clip(counts[tcls] - (tile - tstart[tcls]) * tm, 0, tm), 0).astype(jnp.int32)
    r = jnp.arange(tm, dtype=jnp.int32)[None, :]
    valid = r < nvalid[:, None]
    pos = jnp.clip((cstart[tcls] + (tile - tstart[tcls]) * tm)[:, None] + r, 0, n_tokens - 1)
    src = jnp.where(valid, order[pos], 0).astype(jnp.int32).reshape(ntiles, 1, tm)
    pair_lo = jnp.array([0, 0, 0, 1, 1, 2], jnp.int32)
    pair_hi = jnp.array([1, 2, 3, 2, 3, 3], jnp.int32)
    ea = (4 * (tcls // 6) + pair_lo[tcls % 6]).astype(jnp.int32)
    eb = (4 * (tcls // 6) + pair_hi[tcls % 6]).astype(jnp.int32)
    return ea, eb, nvalid, src


def _lru_inproj_kernel(x_ref, w_ref, gate_ref, xr_ref):
    u = _dot(x_ref[...].astype(_BF16), w_ref[...])
    half = gate_ref.shape[1]
    gate_ref[...] = u[:, 0:half]
    xr_ref[...] = u[:, half:2 * half]


def _lru_inproj(h2d, w, tm):
    n, d = h2d.shape
    half = w.shape[1] // 2
    return pl.pallas_call(
        _lru_inproj_kernel,
        out_shape=(jax.ShapeDtypeStruct((n, half), _F32), jax.ShapeDtypeStruct((n, half), _F32)),
        grid=(n // tm,),
        in_specs=[pl.BlockSpec((tm, d), lambda i: (i, 0)), pl.BlockSpec(w.shape, lambda i: (0, 0))],
        out_specs=(pl.BlockSpec((tm, half), lambda i: (i, 0)), pl.BlockSpec((tm, half), lambda i: (i, 0))),
        name="lru_inproj",
    )(h2d, w)


def _scan_block(a, b, h_in, reverse):
    tt = a.shape[0]
    sub = lax.broadcasted_iota(jnp.int32, a.shape, 0) % 8
    for s in (1, 2, 4):
        if reverse:
            keep = sub < 8 - s
            a_sh = pltpu.roll(a, tt - s, 0)
            b_sh = pltpu.roll(b, tt - s, 0)
        else:
            keep = sub >= s
            a_sh = pltpu.roll(a, s, 0)
            b_sh = pltpu.roll(b, s, 0)
        b = jnp.where(keep, a * b_sh + b, b)
        a = jnp.where(keep, a * a_sh, a)
    ngrp = tt // 8
    hb = jnp.broadcast_to(h_in, (8, a.shape[1]))
    outs = [None] * ngrp
    order = range(ngrp - 1, -1, -1) if reverse else range(ngrp)
    edge = 0 if reverse else 7
    for j in order:
        hj = a[8 * j:8 * j + 8, :] * hb + b[8 * j:8 * j + 8, :]
        outs[j] = hj
        hb = jnp.broadcast_to(hj[edge:edge + 1, :], hj.shape)
    return jnp.concatenate(outs, axis=0), hb[0:1, :]


def _rglru_kernel(*refs, reverse, final, ntile):
    if final:
        (xr_ref, prev_ref, next_ref, cw_ref, cb_ref, wa_ref, ba_ref, wx_ref, bx_ref, lam_ref,
         hf_ref, gate_ref, res_ref, wout_ref, g_ref, b_ref, rwt_ref, rb_ref,
         h_ref, cls_ref, xe_sc, carry_sc) = refs
    else:
        (xr_ref, prev_ref, next_ref, cw_ref, cb_ref, wa_ref, ba_ref, wx_ref, bx_ref, lam_ref,
         hout_ref, xe_sc, carry_sc) = refs
    i = pl.program_id(1)
    ti = (ntile - 1 - i) if reverse else i
    tt = xr_ref.shape[0]

    @pl.when(i == 0)
    def _reset():
        carry_sc[...] = jnp.zeros_like(carry_sc)

    xe_sc[0:8, :] = jnp.where(ti > 0, prev_ref[...], 0.0)
    xe_sc[8:8 + tt, :] = xr_ref[...]
    xe_sc[8 + tt:16 + tt, :] = jnp.where(ti < ntile - 1, next_ref[...], 0.0)
    cw = cw_ref[...]
    xc = (cb_ref[...] + cw[0:1, :] * xe_sc[6:6 + tt, :] + cw[1:2, :] * xe_sc[7:7 + tt, :]
          + cw[2:3, :] * xe_sc[8:8 + tt, :] + cw[3:4, :] * xe_sc[9:9 + tt, :])
    lam = lam_ref[...]
    log_sig = -jnp.log(1.0 + jnp.exp(-lam))
    ys = []
    for n in range(LRU_BLOCKS):
        sl = slice(n * LRU_BW, (n + 1) * LRU_BW)
        xcn = xc[:, sl]
        xb = xcn.astype(_BF16)
        r = _sigmoid(_dot(xb, wa_ref[n]) + ba_ref[:, sl])
        ig = _sigmoid(_dot(xb, wx_ref[n]) + bx_ref[:, sl])
        log_a = LRU_C * r * log_sig[:, sl]
        a = jnp.exp(log_a)
        th = jnp.tanh(log_a)
        bb = jnp.sqrt(-2.0 * th) * lax.rsqrt(1.0 - th) * (ig * xcn)
        h, last = _scan_block(a, bb, carry_sc[:, sl], reverse)
        carry_sc[:, sl] = last
        if final:
            ys.append((hf_ref[:, sl] + h) * jax.nn.gelu(gate_ref[:, sl], approximate=True))
        else:
            hout_ref[:, sl] = h
    if final:
        y = jnp.concatenate(ys, axis=1).astype(_BF16)
        z = ALPHA * res_ref[...] + _dot(y, wout_ref[...])
        _ln_router_store(z, g_ref, b_ref, rwt_ref, rb_ref, h_ref, cls_ref)


def _rglru(xr, cw, cb, wa, ba, wx, bx, lam, bsz, seq, tt, reverse, tail=None):
    ntile = seq // tt
    width = xr.shape[1]
    final = tail is not None
    tidx = (lambda i: ntile - 1 - i) if reverse else (lambda i: i)
    row = lambda b, i: (b * ntile + tidx(i), 0)
    g8 = tt // 8
    nb8 = seq // 8

    def prev_map(b, i):
        return (b * nb8 + jnp.maximum(tidx(i) * g8 - 1, 0), 0)

    def next_map(b, i):
        return (b * nb8 + jnp.minimum((tidx(i) + 1) * g8, nb8 - 1), 0)

    const = lambda shape: pl.BlockSpec(shape, lambda b, i: (0,) * len(shape))
    in_specs = [
        pl.BlockSpec((tt, width), row),
        pl.BlockSpec((8, width), prev_map),
        pl.BlockSpec((8, width), next_map),
        const(cw.shape), const(cb.shape), const(wa.shape), const(ba.shape), const(wx.shape), const(bx.shape),
        const(lam.shape),
    ]
    args = [xr, xr, xr, cw, cb, wa, ba, wx, bx, lam]
    scratch = [pltpu.VMEM((tt + 16, width), _F32), pltpu.VMEM((1, width), _F32)]
    if final:
        hf, gate, res, wout, g, b, rwt, rb = tail
        d = res.shape[1]
        in_specs += [pl.BlockSpec((tt, width), row), pl.BlockSpec((tt, width), row), pl.BlockSpec((tt, d), row),
                     const(wout.shape), const(g.shape), const(b.shape), const(rwt.shape), const(rb.shape)]
        args += [hf, gate, res, wout, g, b, rwt, rb]
        out_shape = (jax.ShapeDtypeStruct((bsz * seq, d + GATE_LANES), _F32),
                     jax.ShapeDtypeStruct((bsz * ntile, 1, tt), jnp.int32))
        out_specs = (pl.BlockSpec((tt, d + GATE_LANES), row),
                     pl.BlockSpec((1, 1, tt), lambda b, i: (b * ntile + tidx(i), 0, 0)))
    else:
        out_shape = jax.ShapeDtypeStruct((bsz * seq, width), _F32)
        out_specs = pl.BlockSpec((tt, width), row)
    return pl.pallas_call(
        functools.partial(_rglru_kernel, reverse=reverse, final=final, ntile=ntile),
        out_shape=out_shape,
        grid=(bsz, ntile),
        in_specs=in_specs,
        out_specs=out_specs,
        scratch_shapes=scratch,
        compiler_params=pltpu.CompilerParams(dimension_semantics=("arbitrary", "arbitrary")),
        name="rglru_bwd_tail" if final else "rglru_fwd",
    )(*args)


def _rotary_tables(seq):
    half = MLA_ROPE // 2
    inv_freq = ROPE_THETA ** (-jnp.arange(half, dtype=_F32) / half)
    ang = jnp.arange(seq, dtype=_F32)[:, None] * inv_freq[None, :]
    return jnp.cos(ang), jnp.sin(ang)


def _t5_bucket(rel):
    n_side = REL_BUCKETS // 2
    max_exact = n_side // 2
    dist = jnp.abs(rel)
    far = max_exact + (jnp.log(jnp.maximum(dist, 1).astype(_F32) / max_exact)
                       / math.log(REL_MAX_DIST / max_exact) * (n_side - max_exact)).astype(jnp.int32)
    far = jnp.minimum(far, n_side - 1)
    return jnp.where(rel > 0, n_side, 0) + jnp.where(dist < max_exact, dist, far)


def _band_buckets():
    rel = jnp.arange(3 * BLOCK)[None, :] - BLOCK - jnp.arange(BLOCK)[:, None]
    return jnp.where(jnp.abs(rel) <= WINDOW, _t5_bucket(rel), -1).astype(jnp.int32)


def _pick_tile(seq, want):
    t = min(want, seq)
    while seq % t:
        t //= 2
    return t


def kernel(x, rel_bias, router_w, router_bias, l0_w_in, l0_q_norm, l0_w_uq, l0_kv_norm, l0_w_ukv, l0_sinks, l0_w_out, l0_ln1_g, l0_ln1_b, l0_w1, l0_w3, l0_w2, l0_ln2_g, l0_ln2_b, l1_w_in, l1_conv_w, l1_conv_b, l1_wa_f, l1_ba_f, l1_wx_f, l1_bx_f, l1_lam_f, l1_wa_b, l1_ba_b, l1_wx_b, l1_bx_b, l1_lam_b, l1_w_out, l1_ln1_g, l1_ln1_b, l1_w1, l1_w3, l1_w2, l1_ln2_g, l1_ln2_b):
    bsz, seq, d = x.shape
    n_tok = bsz * seq
    assert seq % BLOCK == 0 and d == LRU_BLOCKS * LRU_BW and n_tok % MOE_TM == 0
    x2d = x.reshape(n_tok, d)
    row = lambda v: v.reshape(1, -1).astype(_F32)
    rwt = router_w.astype(_F32).T
    rb = router_bias.astype(_F32).reshape(N_EXPERTS, 1)

    o = np.cumsum([0, MLA_Q_RANK, MLA_KV_RANK, MLA_ROPE, SWA_HEADS * SWA_D, SWA_KV_HEADS * SWA_D,
                   SWA_KV_HEADS * SWA_D])
    w_q, w_kv, w_kr, w_qb, w_kb, w_vb = [l0_w_in[:, o[i]:o[i + 1]] for i in range(6)]
    dup = lambda w: jnp.concatenate([w[:, 0:64], w[:, 0:64], w[:, 64:128], w[:, 64:128]], axis=1)
    w_in0 = jnp.concatenate([w_q, w_kv, jnp.pad(w_kr, ((0, 0), (0, LANE - MLA_ROPE))), w_qb, dup(w_kb), dup(w_vb)],
                            axis=1).astype(_BF16)
    wuq = l0_w_uq.reshape(MLA_Q_RANK, MLA_HEADS, MLA_QK)
    wuqt = jnp.pad(wuq, ((0, 0), (0, 0), (0, HEAD_PAD - MLA_QK))).reshape(MLA_Q_RANK, MLA_HEADS * HEAD_PAD).T
    wukv = l0_w_ukv.reshape(MLA_KV_RANK, MLA_HEADS, MLA_NOPE + MLA_V)
    wuk = jnp.pad(wukv[:, :, :MLA_NOPE], ((0, 0), (0, 0), (0, HEAD_PAD - MLA_NOPE))).reshape(MLA_KV_RANK, -1)
    wuvt = wukv[:, :, MLA_NOPE:].reshape(MLA_KV_RANK, MLA_HEADS * MLA_V).T
    cos, sin = _rotary_tables(seq)
    ccs = jnp.pad(jnp.concatenate([cos, cos], axis=1), ((0, 0), (0, LANE - MLA_ROPE)))
    sn = jnp.pad(jnp.concatenate([-sin, sin], axis=1), ((0, 0), (0, LANE - MLA_ROPE)))

    tm = _pick_tile(seq, 512)
    qt, k, vt, qb, kb, vb = _attn_inproj(
        x2d, w_in0, row(l0_q_norm), wuqt.astype(_BF16), row(l0_kv_norm), wuk.astype(_BF16), wuvt.astype(_BF16),
        cos.T, sin.T, ccs, sn, bsz, seq, tm)
    nstream = 2 if seq % 1024 == 0 else 1
    ot = _mla_flash(qt, k, vt, _pick_tile(seq // nstream, 512), _pick_tile(seq // 2, 512), nstream)
    pad = ((0, 0), (BLOCK, BLOCK), (0, 0))
    ob = _swa(qb, jnp.pad(kb, pad), jnp.pad(vb, pad), _band_buckets(), rel_bias.astype(_F32),
              l0_sinks.astype(_F32), bsz, seq, _pick_tile(seq // BLOCK, 8))
    n_a = MLA_HEADS * MLA_V
    he, cls = _outproj_ln_router(ot, ob, x2d, l0_w_out[:n_a].astype(_BF16), l0_w_out[n_a:].astype(_BF16),
                                 row(l0_ln1_g), row(l0_ln1_b), rwt, rb, bsz, seq, tm)
    ea, eb, nvalid, src = _moe_plan(cls, n_tok)
    h = _moe(he, ea, eb, nvalid, src, l0_w1.astype(_BF16), l0_w3.astype(_BF16), l0_w2.astype(_BF16),
             row(l0_ln2_g), row(l0_ln2_b), n_tok, d)

    gate, xr = _lru_inproj(h, l1_w_in.astype(_BF16), tm)
    tt = _pick_tile(seq, 512)
    cw = l1_conv_w.astype(_F32)
    cb = row(l1_conv_b)
    hf = _rglru(xr, cw, cb, l1_wa_f.astype(_BF16), row(l1_ba_f), l1_wx_f.astype(_BF16), row(l1_bx_f),
                row(l1_lam_f), bsz, seq, tt, reverse=False)
    he, cls = _rglru(xr, cw, cb, l1_wa_b.astype(_BF16), row(l1_ba_b), l1_wx_b.astype(_BF16), row(l1_bx_b),
                     row(l1_lam_b), bsz, seq, tt, reverse=True,
                     tail=(hf, gate, h, l1_w_out.astype(_BF16), row(l1_ln1_g), row(l1_ln1_b), rwt, rb))
    ea, eb, nvalid, src = _moe_plan(cls, n_tok)
    h = _moe(he, ea, eb, nvalid, src, l1_w1.astype(_BF16), l1_w3.astype(_BF16), l1_w2.astype(_BF16),
             row(l1_ln2_g), row(l1_ln2_b), n_tok, d)
    return h.reshape(bsz, seq, d)
```

```python
import functools
import math

import jax
import jax.numpy as jnp
import numpy as np
from jax import lax
from jax.experimental import pallas as pl
from jax.experimental.pallas import tpu as pltpu

MLA_HEADS = 8
MLA_Q_RANK = 256
MLA_KV_RANK = 128
MLA_NOPE = 64
MLA_ROPE = 32
MLA_V = 64
MLA_QK = MLA_NOPE + MLA_ROPE
ROPE_THETA = 10000.0
SWA_HEADS = 8
SWA_KV_HEADS = 2
SWA_REP = SWA_HEADS // SWA_KV_HEADS
SWA_D = 64
WINDOW = 128
BLOCK = 128
REL_BUCKETS = 32
REL_MAX_DIST = 128
LRU_BLOCKS = 8
LRU_BW = 128
LRU_C = 8.0
CONV_LEFT = 2
N_EXPERTS = 16
N_GROUPS = 4
EXPERTS_PER_GROUP = 4
N_CLASSES = N_GROUPS * 6
DEPTH = 2
ALPHA = (2.0 * DEPTH) ** 0.25
LN_EPS = 1e-5
RMS_EPS = 1e-6
NEG_BIG = -1e30

LANE = 128
HEAD_PAD = 128
GATE_LANES = 128
VT_ROWS = 80
_LOG2E = math.log2(math.e)
MOE_TM = 256

_F32 = jnp.float32
_BF16 = jnp.bfloat16
_NT_DIMS = (((1,), (1,)), ((), ()))
_TN_DIMS = (((0,), (0,)), ((), ()))


def _dot(a, b):
    return jnp.dot(a, b, preferred_element_type=_F32)


def _dot_nt(a, b):
    return lax.dot_general(a, b, _NT_DIMS, preferred_element_type=_F32)


def _sigmoid(x):
    return 0.5 * jnp.tanh(0.5 * x) + 0.5


def _rms(x, g):
    return x * lax.rsqrt(jnp.mean(jnp.square(x), -1, keepdims=True) + RMS_EPS) * g


def _layer_norm(x, g, b):
    mu = jnp.mean(x, -1, keepdims=True)
    xc = x - mu
    var = jnp.mean(jnp.square(xc), -1, keepdims=True)
    return xc * lax.rsqrt(var + LN_EPS) * g + b


def _attn_inproj_kernel(x_ref, w_ref, wqbt_ref, wvbt_ref, qn_ref, wuqt_ref, kvn_ref, wuk_ref, wuvt_ref,
                        cost_ref, sint_ref, ccs_ref, sn_ref,
                        qt_ref, k_ref, vt_ref, qbt_ref, kb_ref, vbt_ref):
    xb = x_ref[...].astype(_BF16)
    proj = _dot(xb, w_ref[...])
    q_lat = proj[:, 0:256]
    kv_lat = proj[:, 256:384]
    kr = proj[:, 384:512]
    qn = _rms(q_lat, qn_ref[...]).astype(_BF16)
    qt = _dot_nt(wuqt_ref[...], qn) * (MLA_QK ** -0.5 * _LOG2E)
    cos_t = cost_ref[...]
    sin_t = sint_ref[...]
    tm = qt.shape[1]
    zpad = jnp.zeros((HEAD_PAD - MLA_QK, tm), _F32)
    for h in range(MLA_HEADS):
        r0 = h * HEAD_PAD
        x1 = qt[r0 + 64:r0 + 80, :]
        x2 = qt[r0 + 80:r0 + 96, :]
        blk = jnp.concatenate([qt[r0:r0 + 64, :], x1 * cos_t - x2 * sin_t, x1 * sin_t + x2 * cos_t, zpad], axis=0)
        qt_ref[0, r0:r0 + HEAD_PAD, :] = blk.astype(_BF16)
    kvn = _rms(kv_lat, kvn_ref[...]).astype(_BF16)
    kn = _dot(kvn, wuk_ref[...])
    lane = lax.broadcasted_iota(jnp.int32, kr.shape, 1)
    swapped = jnp.where(lane < 16, pltpu.roll(kr, 112, 1), pltpu.roll(kr, 16, 1))
    kpe = kr * ccs_ref[...] + swapped * sn_ref[...]
    kpe = pltpu.roll(kpe, 64, 1)
    k_ref[0] = (kn + jnp.concatenate([kpe] * MLA_HEADS, axis=1)).astype(_BF16)
    vt = _dot_nt(wuvt_ref[...], kvn).astype(_BF16)
    extra = (lax.broadcasted_iota(jnp.int32, (VT_ROWS - MLA_V, tm), 0) == 0).astype(_BF16)
    for h in range(MLA_HEADS):
        vt_ref[0, h * VT_ROWS:h * VT_ROWS + MLA_V, :] = vt[h * MLA_V:(h + 1) * MLA_V, :]
        vt_ref[0, h * VT_ROWS + MLA_V:(h + 1) * VT_ROWS, :] = extra
    kb_ref[0] = proj[:, 512:640].astype(_BF16)
    vbt_ref[0] = _dot_nt(wvbt_ref[...], xb).astype(_BF16)
    qbt = (_dot_nt(wqbt_ref[...], xb) * (SWA_D ** -0.5)).astype(_BF16)
    zhalf = jnp.zeros((SWA_D, tm), _BF16)
    for h in range(SWA_HEADS):
        g = h // SWA_REP
        real = qbt[h * SWA_D:(h + 1) * SWA_D, :]
        qbt_ref[0, h * LANE + g * SWA_D:h * LANE + (g + 1) * SWA_D, :] = real
        qbt_ref[0, h * LANE + (1 - g) * SWA_D:h * LANE + (2 - g) * SWA_D, :] = zhalf


def _attn_inproj(x2d, w, wqbt, wvbt, qn, wuqt, kvn, wuk, wuvt, cos_t, sin_t, ccs, sn, bsz, seq, tm):
    nst = seq // tm
    const = lambda shape: pl.BlockSpec(shape, lambda b, i: (0,) * len(shape))
    out_shape = (
        jax.ShapeDtypeStruct((bsz, MLA_HEADS * HEAD_PAD, seq), _BF16),
        jax.ShapeDtypeStruct((bsz, seq, MLA_HEADS * HEAD_PAD), _BF16),
        jax.ShapeDtypeStruct((bsz, MLA_HEADS * VT_ROWS, seq), _BF16),
        jax.ShapeDtypeStruct((bsz, SWA_HEADS * LANE, seq), _BF16),
        jax.ShapeDtypeStruct((bsz, seq, SWA_KV_HEADS * SWA_D), _BF16),
        jax.ShapeDtypeStruct((bsz, SWA_KV_HEADS * SWA_D, seq), _BF16),
    )
    return pl.pallas_call(
        _attn_inproj_kernel,
        out_shape=out_shape,
        grid=(bsz, nst),
        in_specs=[
            pl.BlockSpec((tm, x2d.shape[1]), lambda b, i: (b * nst + i, 0)),
            const(w.shape), const(wqbt.shape), const(wvbt.shape), const(qn.shape), const(wuqt.shape),
            const(kvn.shape), const(wuk.shape), const(wuvt.shape),
            pl.BlockSpec((16, tm), lambda b, i: (0, i)),
            pl.BlockSpec((16, tm), lambda b, i: (0, i)),
            pl.BlockSpec((tm, LANE), lambda b, i: (i, 0)),
            pl.BlockSpec((tm, LANE), lambda b, i: (i, 0)),
        ],
        out_specs=(
            pl.BlockSpec((1, MLA_HEADS * HEAD_PAD, tm), lambda b, i: (b, 0, i)),
            pl.BlockSpec((1, tm, MLA_HEADS * HEAD_PAD), lambda b, i: (b, i, 0)),
            pl.BlockSpec((1, MLA_HEADS * VT_ROWS, tm), lambda b, i: (b, 0, i)),
            pl.BlockSpec((1, SWA_HEADS * LANE, tm), lambda b, i: (b, 0, i)),
            pl.BlockSpec((1, tm, SWA_KV_HEADS * SWA_D), lambda b, i: (b, i, 0)),
            pl.BlockSpec((1, SWA_KV_HEADS * SWA_D, tm), lambda b, i: (b, 0, i)),
        ),
        name="attn_inproj",
    )(x2d, w, wqbt, wvbt, qn, wuqt, kvn, wuk, wuvt, cos_t, sin_t, ccs, sn)


def _mla_flash_kernel(qt_ref, k_ref, vt_ref, ot_ref, *scratch, tk, nstream):
    st_sc = scratch[:2 * nstream]
    p_sc = scratch[2 * nstream:4 * nstream]
    tq = qt_ref.shape[2] // nstream
    nkv = k_ref.shape[1] // tk
    assert nkv % 2 == 0
    qts = [qt_ref[0, :, s * tq:(s + 1) * tq] for s in range(nstream)]

    def keys(j):
        return k_ref[0, pl.ds(pl.multiple_of(j * tk, tk), tk), :]

    def values(j):
        return vt_ref[0, :, pl.ds(pl.multiple_of(j * tk, tk), tk)]

    def phase(j, cur, state):
        nxt = 1 - cur
        kt = keys(jnp.minimum(j + 1, nkv - 1))
        vt = values(jnp.maximum(j - 1, 0))
        out = []
        for s in range(nstream):
            m, acc = state[s]
            st_sc[2 * s + nxt][...] = _dot(kt, qts[s])
            pv = _dot(vt, p_sc[2 * s + nxt][...])
            st = st_sc[2 * s + cur][...]
            m_new = jnp.maximum(m, jnp.max(st, axis=0, keepdims=True))
            alpha = jnp.exp2(m - m_new)
            p_sc[2 * s + cur][...] = jnp.exp2(st - m_new).astype(_BF16)
            out.append((m_new, (acc + pv) * alpha))
        return out

    def body(jj, state):
        state = phase(2 * jj, 0, state)
        return phase(2 * jj + 1, 1, state)

    k0 = keys(0)
    state = []
    for s in range(nstream):
        st_sc[2 * s][...] = _dot(k0, qts[s])
        p_sc[2 * s + 1][...] = jnp.zeros((tk, tq), _BF16)
        state.append((jnp.full((1, tq), -jnp.inf, _F32), jnp.zeros((VT_ROWS, tq), _F32)))
    state = lax.fori_loop(0, nkv // 2, body, state)
    v_last = values(nkv - 1)
    for s in range(nstream):
        acc = state[s][1] + _dot(v_last, p_sc[2 * s + 1][...])
        ot_ref[0, :, s * tq:(s + 1) * tq] = (acc[0:MLA_V, :] / acc[MLA_V:MLA_V + 1, :]).astype(ot_ref.dtype)


def _mla_flash(qt, k, vt, tq, tk, nstream):
    bsz, _, seq = qt.shape
    tqs = tq * nstream
    scratch = [pltpu.VMEM((tk, tq), _F32)] * (2 * nstream) + [pltpu.VMEM((tk, tq), _BF16)] * (2 * nstream)
    return pl.pallas_call(
        functools.partial(_mla_flash_kernel, tk=tk, nstream=nstream),
        out_shape=jax.ShapeDtypeStruct((bsz, MLA_HEADS * MLA_V, seq), _BF16),
        grid=(bsz, MLA_HEADS, seq // tqs),
        in_specs=[
            pl.BlockSpec((1, HEAD_PAD, tqs), lambda b, h, i: (b, h, i)),
            pl.BlockSpec((1, seq, HEAD_PAD), lambda b, h, i: (b, 0, h)),
            pl.BlockSpec((1, VT_ROWS, seq), lambda b, h, i: (b, h, 0)),
        ],
        out_specs=pl.BlockSpec((1, MLA_V, tqs), lambda b, h, i: (b, h, i)),
        scratch_shapes=scratch,
        name="mla_flash",
    )(qt, k, vt)


def _swa_kernel(relb_ref, sinks_ref, qt_ref, k_ref, vt_ref, bucket_ref, o_ref, bias_sc, *, nblk, seq):
    first = jnp.logical_and(pl.program_id(0) == 0, pl.program_id(1) == 0)

    @pl.when(first)
    def _build_bias():
        bucket = bucket_ref[...]
        for h in range(SWA_HEADS):
            acc = jnp.full(bucket.shape, NEG_BIG, _F32)
            for bk in range(REL_BUCKETS):
                acc = jnp.where(bucket == bk, relb_ref[bk, h], acc)
            bias_sc[h] = acc

    j = pl.program_id(1)
    krow = lax.broadcasted_iota(jnp.int32, (3 * BLOCK, 1), 0)

    def block(u, c):
        n = j * nblk + u
        c0 = pl.multiple_of(u * BLOCK, BLOCK)
        w0 = pl.multiple_of(n * BLOCK, BLOCK)
        key_pos = n * BLOCK - BLOCK + krow
        emask = jnp.where(jnp.logical_and(key_pos >= 0, key_pos < seq), 0.0, NEG_BIG).astype(_F32)
        kw = k_ref[0, pl.ds(w0, 3 * BLOCK), :]
        vw = vt_ref[0, :, pl.ds(w0, 3 * BLOCK)]
        for g in range(SWA_KV_HEADS):
            heads = range(g * SWA_REP, (g + 1) * SWA_REP)
            qs = jnp.concatenate([qt_ref[0, h * LANE:(h + 1) * LANE, pl.ds(c0, BLOCK)] for h in heads], axis=1)
            bias = jnp.concatenate([bias_sc[h] for h in heads], axis=1)
            sink = jnp.concatenate([jnp.full((1, BLOCK), sinks_ref[h], _F32) for h in heads], axis=1)
            st = _dot(kw, qs) + bias + emask
            m = jnp.maximum(jnp.max(st, axis=0, keepdims=True), sink)
            p = jnp.exp(st - m)
            den = jnp.sum(p, axis=0, keepdims=True) + jnp.exp(sink - m)
            ot = _dot(vw[g * SWA_D:(g + 1) * SWA_D, :], p.astype(_BF16)) / den
            for r, h in enumerate(heads):
                o_ref[0, h * SWA_D:(h + 1) * SWA_D, pl.ds(c0, BLOCK)] = ot[:, r * BLOCK:(r + 1) * BLOCK].astype(o_ref.dtype)
        return c

    lax.fori_loop(0, nblk, block, 0)


def _swa(qbt, kpad, vtpad, bucket_t, rel_bias, sinks, bsz, seq, nblk):
    nsteps = seq // (nblk * BLOCK)
    cols = nblk * BLOCK
    return pl.pallas_call(
        functools.partial(_swa_kernel, nblk=nblk, seq=seq),
        out_shape=jax.ShapeDtypeStruct((bsz, SWA_HEADS * SWA_D, seq), _BF16),
        grid=(bsz, nsteps),
        in_specs=[
            pl.BlockSpec(memory_space=pltpu.SMEM),
            pl.BlockSpec(memory_space=pltpu.SMEM),
            pl.BlockSpec((1, SWA_HEADS * LANE, cols), lambda b, j: (b, 0, j)),
            pl.BlockSpec((1, seq + 2 * BLOCK, SWA_KV_HEADS * SWA_D), lambda b, j: (b, 0, 0)),
            pl.BlockSpec((1, SWA_KV_HEADS * SWA_D, seq + 2 * BLOCK), lambda b, j: (b, 0, 0)),
            pl.BlockSpec((3 * BLOCK, BLOCK), lambda b, j: (0, 0)),
        ],
        out_specs=pl.BlockSpec((1, SWA_HEADS * SWA_D, cols), lambda b, j: (b, 0, j)),
        scratch_shapes=[pltpu.VMEM((SWA_HEADS, 3 * BLOCK, BLOCK), _F32)],
        compiler_params=pltpu.CompilerParams(dimension_semantics=("arbitrary", "arbitrary")),
        name="swa",
    )(rel_bias, sinks, qbt, kpad, vtpad, bucket_t)


def _route(logits_t, rbias):
    sc = jax.nn.sigmoid(logits_t)
    bz = sc + rbias
    s_rows = [sc[e:e + 1, :] for e in range(N_EXPERTS)]
    b_rows = [bz[e:e + 1, :] for e in range(N_EXPERTS)]
    gsel = None
    best = None
    for g in range(N_GROUPS):
        r = b_rows[4 * g:4 * g + 4]
        gs = r[0] + r[1]
        for (i, k) in ((0, 2), (0, 3), (1, 2), (1, 3), (2, 3)):
            gs = jnp.maximum(gs, r[i] + r[k])
        if g == 0:
            gsel = jnp.zeros(gs.shape, jnp.int32)
            best = gs
        else:
            better = gs > best
            gsel = jnp.where(better, g, gsel)
            best = jnp.where(better, gs, best)

    def pick(rows, k):
        out = rows[12 + k]
        for g in (2, 1, 0):
            out = jnp.where(gsel == g, rows[4 * g + k], out)
        return out

    v = [pick(b_rows, k) for k in range(4)]
    s = [pick(s_rows, k) for k in range(4)]
    i1 = jnp.zeros(gsel.shape, jnp.int32)
    m1 = v[0]
    w1 = s[0]
    for k in range(1, 4):
        gt = v[k] > m1
        i1 = jnp.where(gt, k, i1)
        m1 = jnp.where(gt, v[k], m1)
        w1 = jnp.where(gt, s[k], w1)
    i2 = jnp.full(gsel.shape, -1, jnp.int32)
    m2 = jnp.full(m1.shape, -jnp.inf, _F32)
    w2 = jnp.zeros(m1.shape, _F32)
    for k in range(4):
        ok = jnp.logical_and(i1 != k, jnp.logical_or(i2 < 0, v[k] > m2))
        i2 = jnp.where(ok, k, i2)
        m2 = jnp.where(ok, v[k], m2)
        w2 = jnp.where(ok, s[k], w2)
    tot = w1 + w2
    g1 = w1 / tot
    g2 = w2 / tot
    first_lo = i1 < i2
    lo = jnp.where(first_lo, i1, i2)
    hi = jnp.where(first_lo, i2, i1)
    pair = jnp.where(lo == 0, hi - 1, jnp.where(lo == 1, hi + 1, 5))
    cls = gsel * 6 + pair
    return cls, jnp.where(first_lo, g1, g2), jnp.where(first_lo, g2, g1)


def _ln_router_store(z, g_ref, b_ref, rwt_ref, rb_ref, h_ref, cls_ref):
    h = _layer_norm(z, g_ref[...], b_ref[...])
    tm = h.shape[0]
    logits_t = lax.dot_general(rwt_ref[...], h, _NT_DIMS, precision=lax.Precision.HIGHEST,
                               preferred_element_type=_F32)
    cls, g_lo, g_hi = _route(logits_t, rb_ref[...])
    rows = jnp.concatenate([g_lo, g_hi, jnp.zeros((GATE_LANES - 2, tm), _F32)], axis=0)
    d = h.shape[1]
    h_ref[:, 0:d] = h
    h_ref[:, d:d + GATE_LANES] = rows.T
    cls_ref[0] = cls


def _outproj_ln_router_kernel(ot_ref, ob_ref, x_ref, w_ref, g_ref, b_ref, rwt_ref, rb_ref, h_ref, cls_ref):
    heads_t = jnp.concatenate([ot_ref[0], ob_ref[0]], axis=0)
    mixed = lax.dot_general(heads_t, w_ref[...], _TN_DIMS, preferred_element_type=_F32)
    z = ALPHA * x_ref[...] + mixed
    _ln_router_store(z, g_ref, b_ref, rwt_ref, rb_ref, h_ref, cls_ref)


def _outproj_ln_router(ot, ob, x2d, w, g, b, rwt, rb, bsz, seq, tm):
    nst = seq // tm
    d = x2d.shape[1]
    const = lambda shape: pl.BlockSpec(shape, lambda bb, i: (0,) * len(shape))
    return pl.pallas_call(
        _outproj_ln_router_kernel,
        out_shape=(jax.ShapeDtypeStruct((bsz * seq, d + GATE_LANES), _F32),
                   jax.ShapeDtypeStruct((bsz * nst, 1, tm), jnp.int32)),
        grid=(bsz, nst),
        in_specs=[
            pl.BlockSpec((1, ot.shape[1], tm), lambda bb, i: (bb, 0, i)),
            pl.BlockSpec((1, ob.shape[1], tm), lambda bb, i: (bb, 0, i)),
            pl.BlockSpec((tm, d), lambda bb, i: (bb * nst + i, 0)),
            const(w.shape), const(g.shape), const(b.shape), const(rwt.shape), const(rb.shape),
        ],
        out_specs=(pl.BlockSpec((tm, d + GATE_LANES), lambda bb, i: (bb * nst + i, 0)),
                   pl.BlockSpec((1, 1, tm), lambda bb, i: (bb * nst + i, 0, 0))),
        name="outproj_ln_router",
    )(ot, ob, x2d, w, g, b, rwt, rb)


_SUBLANES = 8
_SCAN_RADIX = 4
_DMA_CHUNK = 32


def _moe_kernel(ea_ref, eb_ref, nv_ref, src_ref, nsrc_ref, h_hbm, w1a, w3a, w2a, w1b, w3b, w2b, g_ref, b_ref,
                out_hbm, xbuf, obuf, sem_in, sem_out):
    i = pl.program_id(0)
    ntiles = pl.num_programs(0)
    tm = xbuf.shape[1] * _SUBLANES
    d = obuf.shape[2]
    nv = nv_ref[i]
    nv_prev = jnp.where(i > 0, nv_ref[jnp.maximum(i - 1, 0)], 0)
    nv_next = jnp.where(i + 1 < ntiles, nv_ref[jnp.minimum(i + 1, ntiles - 1)], 0)
    slot = lax.rem(i, 2)

    def hbm_row(ref, idx):
        return ref.at[lax.shift_right_logical(idx, 3), jnp.bitwise_and(idx, _SUBLANES - 1)]

    def gather_start(idx_ref, s):
        for r in range(tm):
            idx = idx_ref[0, 0, r]
            pltpu.make_async_copy(hbm_row(h_hbm, idx), xbuf.at[s, r // _SUBLANES, r % _SUBLANES], sem_in.at[s]).start()

    def scatter_wait(n):
        for bit in range(tm.bit_length()):
            rows = 1 << bit

            @pl.when(jnp.bitwise_and(n, rows) != 0)
            def _():
                if rows >= _SUBLANES:
                    grp = pl.ds(0, rows // _SUBLANES)
                    pltpu.make_async_copy(obuf.at[grp], out_hbm.at[grp], sem_out).wait()
                else:
                    pltpu.make_async_copy(obuf.at[0, pl.ds(0, rows)], out_hbm.at[0, pl.ds(0, rows)], sem_out).wait()

    @pl.when(jnp.logical_and(i == 0, nv > 0))
    def _prologue():
        gather_start(src_ref, 0)

    @pl.when(nv_next > 0)
    def _prefetch():
        gather_start(nsrc_ref, 1 - slot)

    @pl.when(nv > 0)
    def _tile():
        pltpu.make_async_copy(h_hbm.at[pl.ds(0, tm // _SUBLANES)], xbuf.at[slot], sem_in.at[slot]).wait()
        xg = xbuf[slot].reshape(tm, d + GATE_LANES)
        x = xg[:, 0:d]
        gates = xg[:, d:d + GATE_LANES]
        ga = gates[:, 0:1]
        gb = gates[:, 1:2]
        xb = x.astype(_BF16)

        def expert(w1, w3, w2):
            u = _dot(xb, w1[...])
            hh = u * _sigmoid(u) * _dot(xb, w3[...])
            return _dot(hh.astype(_BF16), w2[...])

        y = ga * expert(w1a, w3a, w2a) + gb * expert(w1b, w3b, w2b)
        z = _layer_norm(ALPHA * x + y, g_ref[...], b_ref[...])

        @pl.when(nv_prev > 0)
        def _drain_prev():
            scatter_wait(nv_prev)

        obuf[...] = z.reshape(tm // _SUBLANES, _SUBLANES, d)
        for c in range(tm // _DMA_CHUNK):
            @pl.when(nv >= (c + 1) * _DMA_CHUNK)
            def _chunk():
                for r in range(c * _DMA_CHUNK, (c + 1) * _DMA_CHUNK):
                    idx = src_ref[0, 0, r]
                    pltpu.make_async_copy(obuf.at[r // _SUBLANES, r % _SUBLANES], hbm_row(out_hbm, idx),
                                          sem_out).start()

        def group(gi, c):
            for u in range(_SUBLANES):
                idx = src_ref[0, 0, gi * _SUBLANES + u]
                pltpu.make_async_copy(obuf.at[gi, u], hbm_row(out_hbm, idx), sem_out).start()
            return c

        def single(r, c):
            idx = src_ref[0, 0, r]
            pltpu.make_async_copy(hbm_row(obuf, r), hbm_row(out_hbm, idx), sem_out).start()
            return c

        ngroups = lax.shift_right_logical(nv, 3)
        lax.fori_loop((nv // _DMA_CHUNK) * (_DMA_CHUNK // _SUBLANES), ngroups, group, 0)
        lax.fori_loop(ngroups * _SUBLANES, nv, single, 0)

        @pl.when(nv_next == 0)
        def _drain_last():
            scatter_wait(nv)


def _moe(he, ea, eb, nvalid, src, w1, w3, w2, g, b, n_tokens, d):
    ntiles = ea.shape[0]
    tm = MOE_TM
    dff = w1.shape[2]
    wspec_a = lambda shape: pl.BlockSpec((None,) + shape, lambda i, ea_r, eb_r, nv_r: (ea_r[i], 0, 0))
    wspec_b = lambda shape: pl.BlockSpec((None,) + shape, lambda i, ea_r, eb_r, nv_r: (eb_r[i], 0, 0))
    grid_spec = pltpu.PrefetchScalarGridSpec(
        num_scalar_prefetch=3,
        grid=(ntiles,),
        in_specs=[
            pl.BlockSpec((1, 1, tm), lambda i, *_: (i, 0, 0), memory_space=pltpu.SMEM),
            pl.BlockSpec((1, 1, tm), lambda i, *_: (jnp.minimum(i + 1, ntiles - 1), 0, 0), memory_space=pltpu.SMEM),
            pl.BlockSpec(memory_space=pl.ANY),
            wspec_a((d, dff)), wspec_a((d, dff)), wspec_a((dff, d)),
            wspec_b((d, dff)), wspec_b((d, dff)), wspec_b((dff, d)),
            pl.BlockSpec((1, d), lambda i, *_: (0, 0)),
            pl.BlockSpec((1, d), lambda i, *_: (0, 0)),
        ],
        out_specs=pl.BlockSpec(memory_space=pl.ANY),
        scratch_shapes=[
            pltpu.VMEM((2, tm // _SUBLANES, _SUBLANES, d + GATE_LANES), _F32),
            pltpu.VMEM((tm // _SUBLANES, _SUBLANES, d), _F32),
            pltpu.SemaphoreType.DMA((2,)),
            pltpu.SemaphoreType.DMA,
        ],
    )
    he3 = he.reshape(n_tokens // _SUBLANES, _SUBLANES, d + GATE_LANES)
    out = pl.pallas_call(
        _moe_kernel,
        out_shape=jax.ShapeDtypeStruct((n_tokens // _SUBLANES, _SUBLANES, d), _F32),
        grid_spec=grid_spec,
        compiler_params=pltpu.CompilerParams(dimension_semantics=("arbitrary",)),
        name="moe",
    )(ea, eb, nvalid, src, src, he3, w1, w3, w2, w1, w3, w2, g, b)
    return out.reshape(n_tokens, d)


def _moe_plan(cls, n_tokens):
    tm = MOE_TM
    ntiles = n_tokens // tm + N_CLASSES
    cls = cls.reshape(n_tokens)
    order = jnp.argsort(cls, stable=True).astype(jnp.int32)
    counts = jnp.sum((cls[:, None] == jnp.arange(N_CLASSES, dtype=jnp.int32)[None, :]).astype(jnp.int32), axis=0)
    cstart = jnp.cumsum(counts) - counts
    ptiles = (counts + tm - 1) // tm
    tend = jnp.cumsum(ptiles)
    tstart = tend - ptiles
    tile = jnp.arange(ntiles, dtype=jnp.int32)
    tcls = jnp.minimum(jnp.sum((tile[:, None] >= tend[None, :]).astype(jnp.int32), axis=1), N_CLASSES - 1)
    used = tile < tend[-1]
    nvalid = jnp.where(used, jnp.clip(counts[tcls] - (tile - tstart[tcls]) * tm, 0, tm), 0).astype(jnp.int32)
    r = jnp.arange(tm, dtype=jnp.int32)[None, :]
    valid = r < nvalid[:, None]
    pos = jnp.clip((cstart[tcls] + (tile - tstart[tcls]) * tm)[:, None] + r, 0, n_tokens - 1)
    src = jnp.where(valid, order[pos], 0).astype(jnp.int32).reshape(ntiles, 1, tm)
    pair_lo = jnp.array([0, 0, 0, 1, 1, 2], jnp.int32)
    pair_hi = jnp.array([1, 2, 3, 2, 3, 3], jnp.int32)
    ea = (4 * (tcls // 6) + pair_lo[tcls % 6]).astype(jnp.int32)
    eb = (4 * (tcls // 6) + pair_hi[tcls % 6]).astype(jnp.int32)
    return ea, eb, nvalid, src


def _lru_inproj_kernel(x_ref, w_ref, gate_ref, xr_ref):
    u = _dot(x_ref[...].astype(_BF16), w_ref[...])
    half = gate_ref.shape[1]
    gate_ref[...] = u[:, 0:half]
    xr_ref[...] = u[:, half:2 * half]


def _lru_inproj(h2d, w, tm):
    n, d = h2d.shape
    half = w.shape[1] // 2
    return pl.pallas_call(
        _lru_inproj_kernel,
        out_shape=(jax.ShapeDtypeStruct((n, half), _F32), jax.ShapeDtypeStruct((n, half), _F32)),
        grid=(n // tm,),
        in_specs=[pl.BlockSpec((tm, d), lambda i: (i, 0)), pl.BlockSpec(w.shape, lambda i: (0, 0))],
        out_specs=(pl.BlockSpec((tm, half), lambda i: (i, 0)), pl.BlockSpec((tm, half), lambda i: (i, 0))),
        name="lru_inproj",
    )(h2d, w)


def _scan_slabs(a_refs, b_refs, c_refs, h_in, reverse):
    nlev = len(a_refs)
    ks = list(range(_SCAN_RADIX))
    if reverse:
        ks.reverse()
    for l in range(nlev - 1):
        grp = a_refs[l].shape[1] // _SCAN_RADIX
        p = q = None
        for k in ks:
            sl = pl.ds(k, grp, stride=_SCAN_RADIX)
            a = a_refs[l][:, sl, :]
            b = b_refs[l][:, sl, :]
            if p is None:
                p, q = a, b
            else:
                p, q = a * p, a * q + b
                a_refs[l][:, sl, :] = p
                b_refs[l][:, sl, :] = q
        a_refs[l + 1][...] = p
        b_refs[l + 1][...] = q
    rows = a_refs[-1].shape[1]
    a = a_refs[-1][...]
    b = b_refs[-1][...]
    out = [None] * rows
    h = h_in
    for r in (range(rows - 1, -1, -1) if reverse else range(rows)):
        h = a[:, r:r + 1, :] * h + b[:, r:r + 1, :]
        out[r] = h
    h_out = h
    b_refs[-1][...] = jnp.concatenate(out, axis=1)
    edge = jnp.broadcast_to(h_in, (h_in.shape[0], _SUBLANES, h_in.shape[2]))
    for l in range(nlev - 2, -1, -1):
        grp = a_refs[l].shape[1] // _SCAN_RADIX
        c = c_refs[l]
        c[:, _SUBLANES:_SUBLANES + grp, :] = b_refs[l + 1][...]
        if reverse:
            c[:, _SUBLANES + grp:2 * _SUBLANES + grp, :] = edge
            cin = c[:, _SUBLANES + 1:_SUBLANES + 1 + grp, :]
        else:
            c[:, 0:_SUBLANES, :] = edge
            cin = c[:, _SUBLANES - 1:_SUBLANES - 1 + grp, :]
        for k in ks:
            sl = pl.ds(k, grp, stride=_SCAN_RADIX)
            b_refs[l][:, sl, :] = a_refs[l][:, sl, :] * cin + b_refs[l][:, sl, :]
    return h_out


def _rglru_kernel(*refs, reverse, final, ntile):
    if final:
        (xr_ref, prev_ref, next_ref, cw_ref, cb_ref, wa_ref, ba_ref, wx_ref, bx_ref, lam_ref,
         hf_ref, gate_ref, res_ref, wout_ref, g_ref, b_ref, rwt_ref, rb_ref,
         h_ref, cls_ref, xe_sc, carry_sc, *scan_sc) = refs
    else:
        (xr_ref, prev_ref, next_ref, cw_ref, cb_ref, wa_ref, ba_ref, wx_ref, bx_ref, lam_ref,
         hout_ref, xe_sc, carry_sc, *scan_sc) = refs
    nlev = (len(scan_sc) + 1) // 3
    a_refs, b_refs, c_refs = scan_sc[:nlev], scan_sc[nlev:2 * nlev], scan_sc[2 * nlev:]
    i = pl.program_id(1)
    ti = (ntile - 1 - i) if reverse else i
    tt = xr_ref.shape[0]

    @pl.when(i == 0)
    def _reset():
        carry_sc[...] = jnp.zeros_like(carry_sc)

    prev = jnp.where(ti > 0, prev_ref[...], 0.0)
    nxt = jnp.where(ti < ntile - 1, next_ref[...], 0.0)
    for n in range(LRU_BLOCKS):
        sl = slice(n * LRU_BW, (n + 1) * LRU_BW)
        xe_sc[n, 0:_SUBLANES, :] = prev[:, sl]
        xe_sc[n, _SUBLANES:_SUBLANES + tt, :] = xr_ref[:, sl]
        xe_sc[n, _SUBLANES + tt:2 * _SUBLANES + tt, :] = nxt[:, sl]
    cw = cw_ref[...]
    cb = cb_ref[...]
    decay = LRU_C * -jnp.log(1.0 + jnp.exp(-lam_ref[...]))
    for n in range(LRU_BLOCKS):
        sl = slice(n * LRU_BW, (n + 1) * LRU_BW)
        xcn = cb[:, sl]
        for k in range(cw.shape[0]):
            xcn = xcn + cw[k:k + 1, sl] * xe_sc[n, _SUBLANES - CONV_LEFT + k:_SUBLANES - CONV_LEFT + k + tt, :]
        xb = xcn.astype(_BF16)
        r = _sigmoid(_dot(xb, wa_ref[n]) + ba_ref[:, sl])
        ig = _sigmoid(_dot(xb, wx_ref[n]) + bx_ref[:, sl])
        log_a = r * decay[:, sl]
        th = jnp.tanh(log_a)
        num = -2.0 * th
        scale = jnp.where(num > 0.0, num * lax.rsqrt(num * (1.0 - th)), 0.0)
        a_refs[0][n] = jnp.exp(log_a)
        b_refs[0][n] = scale * (ig * xcn)
    carry_sc[...] = _scan_slabs(a_refs, b_refs, c_refs, carry_sc[...], reverse)
    ys = []
    for n in range(LRU_BLOCKS):
        sl = slice(n * LRU_BW, (n + 1) * LRU_BW)
        h = b_refs[0][n]
        if final:
            ys.append((hf_ref[:, sl] + h) * jax.nn.gelu(gate_ref[:, sl], approximate=True))
        else:
            hout_ref[:, sl] = h
    if final:
        y = jnp.concatenate(ys, axis=1).astype(_BF16)
        z = ALPHA * res_ref[...] + _dot(y, wout_ref[...])
        _ln_router_store(z, g_ref, b_ref, rwt_ref, rb_ref, h_ref, cls_ref)


def _rglru(xr, cw, cb, wa, ba, wx, bx, lam, bsz, seq, tt, reverse, tail=None):
    ntile = seq // tt
    width = xr.shape[1]
    final = tail is not None
    tidx = (lambda i: ntile - 1 - i) if reverse else (lambda i: i)
    row = lambda b, i: (b * ntile + tidx(i), 0)
    g8 = tt // 8
    nb8 = seq // 8

    def prev_map(b, i):
        return (b * nb8 + jnp.maximum(tidx(i) * g8 - 1, 0), 0)

    def next_map(b, i):
        return (b * nb8 + jnp.minimum((tidx(i) + 1) * g8, nb8 - 1), 0)

    const = lambda shape: pl.BlockSpec(shape, lambda b, i: (0,) * len(shape))
    in_specs = [
        pl.BlockSpec((tt, width), row),
        pl.BlockSpec((8, width), prev_map),
        pl.BlockSpec((8, width), next_map),
        const(cw.shape), const(cb.shape), const(wa.shape), const(ba.shape), const(wx.shape), const(bx.shape),
        const(lam.shape),
    ]
    args = [xr, xr, xr, cw, cb, wa, ba, wx, bx, lam]
    rows = [tt]
    while rows[-1] > _SUBLANES:
        assert rows[-1] % _SCAN_RADIX == 0
        rows.append(rows[-1] // _SCAN_RADIX)
    assert rows[-1] == _SUBLANES
    slab = lambda r: pltpu.VMEM((LRU_BLOCKS, r, LRU_BW), _F32)
    scratch = [slab(tt + 2 * _SUBLANES), pltpu.VMEM((LRU_BLOCKS, 1, LRU_BW), _F32)]
    scratch += [slab(r) for r in rows] * 2 + [slab(r + 2 * _SUBLANES) for r in rows[1:]]
    if final:
        hf, gate, res, wout, g, b, rwt, rb = tail
        d = res.shape[1]
        in_specs += [pl.BlockSpec((tt, width), row), pl.BlockSpec((tt, width), row), pl.BlockSpec((tt, d), row),
                     const(wout.shape), const(g.shape), const(b.shape), const(rwt.shape), const(rb.shape)]
        args += [hf, gate, res, wout, g, b, rwt, rb]
        out_shape = (jax.ShapeDtypeStruct((bsz * seq, d + GATE_LANES), _F32),
                     jax.ShapeDtypeStruct((bsz * ntile, 1, tt), jnp.int32))
        out_specs = (pl.BlockSpec((tt, d + GATE_LANES), row),
                     pl.BlockSpec((1, 1, tt), lambda b, i: (b * ntile + tidx(i), 0, 0)))
    else:
        out_shape = jax.ShapeDtypeStruct((bsz * seq, width), _F32)
        out_specs = pl.BlockSpec((tt, width), row)
    return pl.pallas_call(
        functools.partial(_rglru_kernel, reverse=reverse, final=final, ntile=ntile),
        out_shape=out_shape,
        grid=(bsz, ntile),
        in_specs=in_specs,
        out_specs=out_specs,
        scratch_shapes=scratch,
        compiler_params=pltpu.CompilerParams(dimension_semantics=("arbitrary", "arbitrary")),
        name="rglru_bwd_tail" if final else "rglru_fwd",
    )(*args)


def _rotary_tables(seq):
    half = MLA_ROPE // 2
    inv_freq = ROPE_THETA ** (-jnp.arange(half, dtype=_F32) / half)
    ang = jnp.arange(seq, dtype=_F32)[:, None] * inv_freq[None, :]
    return jnp.cos(ang), jnp.sin(ang)


def _t5_bucket(rel):
    n_side = REL_BUCKETS // 2
    max_exact = n_side // 2
    dist = jnp.abs(rel)
    far = max_exact + (jnp.log(jnp.maximum(dist, 1).astype(_F32) / max_exact)
                       / math.log(REL_MAX_DIST / max_exact) * (n_side - max_exact)).astype(jnp.int32)
    far = jnp.minimum(far, n_side - 1)
    return jnp.where(rel > 0, n_side, 0) + jnp.where(dist < max_exact, dist, far)


def _band_buckets():
    rel = jnp.arange(3 * BLOCK)[None, :] - BLOCK - jnp.arange(BLOCK)[:, None]
    ids = jnp.where(jnp.abs(rel) <= WINDOW, _t5_bucket(rel), REL_BUCKETS).astype(jnp.int32)
    return jnp.bitwise_and(ids, 2 * REL_BUCKETS - 1)


def _pick_tile(seq, want):
    t = min(want, seq)
    while seq % t:
        t //= 2
    return t


def kernel(x, rel_bias, router_w, router_bias, l0_w_in, l0_q_norm, l0_w_uq, l0_kv_norm, l0_w_ukv, l0_sinks, l0_w_out, l0_ln1_g, l0_ln1_b, l0_w1, l0_w3, l0_w2, l0_ln2_g, l0_ln2_b, l1_w_in, l1_conv_w, l1_conv_b, l1_wa_f, l1_ba_f, l1_wx_f, l1_bx_f, l1_lam_f, l1_wa_b, l1_ba_b, l1_wx_b, l1_bx_b, l1_lam_b, l1_w_out, l1_ln1_g, l1_ln1_b, l1_w1, l1_w3, l1_w2, l1_ln2_g, l1_ln2_b):
    bsz, seq, d = x.shape
    n_tok = bsz * seq
    assert seq % BLOCK == 0 and d == LRU_BLOCKS * LRU_BW and n_tok % MOE_TM == 0
    x2d = x.reshape(n_tok, d)
    row = lambda v: v.reshape(1, -1).astype(_F32)
    rwt = router_w.astype(_F32).T
    rb = router_bias.astype(_F32).reshape(N_EXPERTS, 1)

    o = np.cumsum([0, MLA_Q_RANK, MLA_KV_RANK, MLA_ROPE, SWA_HEADS * SWA_D, SWA_KV_HEADS * SWA_D,
                   SWA_KV_HEADS * SWA_D])
    w_q, w_kv, w_kr, w_qb, w_kb, w_vb = [l0_w_in[:, o[i]:o[i + 1]] for i in range(6)]
    w_in0 = jnp.concatenate([w_q, w_kv, jnp.pad(w_kr, ((0, 0), (0, LANE - MLA_ROPE))), w_kb], axis=1).astype(_BF16)
    wuq = l0_w_uq.reshape(MLA_Q_RANK, MLA_HEADS, MLA_QK)
    wuqt = jnp.pad(wuq, ((0, 0), (0, 0), (0, HEAD_PAD - MLA_QK))).reshape(MLA_Q_RANK, MLA_HEADS * HEAD_PAD).T
    wukv = l0_w_ukv.reshape(MLA_KV_RANK, MLA_HEADS, MLA_NOPE + MLA_V)
    wuk = jnp.pad(wukv[:, :, :MLA_NOPE], ((0, 0), (0, 0), (0, HEAD_PAD - MLA_NOPE))).reshape(MLA_KV_RANK, -1)
    wuvt = wukv[:, :, MLA_NOPE:].reshape(MLA_KV_RANK, MLA_HEADS * MLA_V).T
    cos, sin = _rotary_tables(seq)
    ccs = jnp.pad(jnp.concatenate([cos, cos], axis=1), ((0, 0), (0, LANE - MLA_ROPE)))
    sn = jnp.pad(jnp.concatenate([-sin, sin], axis=1), ((0, 0), (0, LANE - MLA_ROPE)))

    tm = _pick_tile(seq, 512)
    qt, k, vt, qbt, kb, vbt = _attn_inproj(
        x2d, w_in0, w_qb.T.astype(_BF16), w_vb.T.astype(_BF16), row(l0_q_norm), wuqt.astype(_BF16),
        row(l0_kv_norm), wuk.astype(_BF16), wuvt.astype(_BF16), cos.T, sin.T, ccs, sn, bsz, seq, tm)
    nstream = 2 if seq % 1024 == 0 else 1
    ot = _mla_flash(qt, k, vt, _pick_tile(seq // nstream, 512), _pick_tile(seq // 2, 512), nstream)
    kpad = jnp.pad(kb, ((0, 0), (BLOCK, BLOCK), (0, 0)))
    vtpad = jnp.pad(vbt, ((0, 0), (0, 0), (BLOCK, BLOCK)))
    obt = _swa(qbt, kpad, vtpad, _band_buckets().T, rel_bias.astype(_F32), l0_sinks.astype(_F32), bsz, seq,
               _pick_tile(seq // BLOCK, 8))
    he, cls = _outproj_ln_router(ot, obt, x2d, l0_w_out.astype(_BF16), row(l0_ln1_g), row(l0_ln1_b), rwt, rb,
                                 bsz, seq, tm)
    ea, eb, nvalid, src = _moe_plan(cls, n_tok)
    h = _moe(he, ea, eb, nvalid, src, l0_w1.astype(_BF16), l0_w3.astype(_BF16), l0_w2.astype(_BF16),
             row(l0_ln2_g), row(l0_ln2_b), n_tok, d)

    gate, xr = _lru_inproj(h, l1_w_in.astype(_BF16), tm)
    tt = _pick_tile(seq, 512)
    cw = l1_conv_w.astype(_F32)
    cb = row(l1_conv_b)
    hf = _rglru(xr, cw, cb, l1_wa_f.astype(_BF16), row(l1_ba_f), l1_wx_f.astype(_BF16), row(l1_bx_f),
                row(l1_lam_f), bsz, seq, tt, reverse=False)
    he, cls = _rglru(xr, cw, cb, l1_wa_b.astype(_BF16), row(l1_ba_b), l1_wx_b.astype(_BF16), row(l1_bx_b),
                     row(l1_lam_b), bsz, seq, tt, reverse=True,
                     tail=(hf, gate, h, l1_w_out.astype(_BF16), row(l1_ln1_g), row(l1_ln1_b), rwt, rb))
    ea, eb, nvalid, src = _moe_plan(cls, n_tok)
    h = _moe(he, ea, eb, nvalid, src, l1_w1.astype(_BF16), l1_w3.astype(_BF16), l1_w2.astype(_BF16),
             row(l1_ln2_g), row(l1_ln2_b), n_tok, d)
    return h.reshape(bsz, seq, d)
```

```python
import functools
import math

import jax
import jax.numpy as jnp
import numpy as np
from jax import lax
from jax.experimental import pallas as pl
from jax.experimental.pallas import tpu as pltpu

MLA_HEADS = 8
MLA_Q_RANK = 256
MLA_KV_RANK = 128
MLA_NOPE = 64
MLA_ROPE = 32
MLA_V = 64
MLA_QK = MLA_NOPE + MLA_ROPE
ROPE_THETA = 10000.0
SWA_HEADS = 8
SWA_KV_HEADS = 2
SWA_REP = SWA_HEADS // SWA_KV_HEADS
SWA_D = 64
WINDOW = 128
BLOCK = 128
REL_BUCKETS = 32
REL_MAX_DIST = 128
LRU_BLOCKS = 8
LRU_BW = 128
LRU_C = 8.0
CONV_LEFT = 2
N_EXPERTS = 16
N_GROUPS = 4
EXPERTS_PER_GROUP = 4
N_CLASSES = N_GROUPS * 6
DEPTH = 2
ALPHA = (2.0 * DEPTH) ** 0.25
LN_EPS = 1e-5
RMS_EPS = 1e-6
NEG_BIG = -1e30

LANE = 128
HEAD_PAD = 128
GATE_LANES = 128
VT_ROWS = 80
_LOG2E = math.log2(math.e)
MOE_TM = 256

_F32 = jnp.float32
_BF16 = jnp.bfloat16
_NT_DIMS = (((1,), (1,)), ((), ()))
_TN_DIMS = (((0,), (0,)), ((), ()))


def _dot(a, b):
    return jnp.dot(a, b, preferred_element_type=_F32)


def _dot_nt(a, b):
    return lax.dot_general(a, b, _NT_DIMS, preferred_element_type=_F32)


def _sigmoid(x):
    return 0.5 * jnp.tanh(0.5 * x) + 0.5


def _rms(x, g):
    return x * lax.rsqrt(jnp.mean(jnp.square(x), -1, keepdims=True) + RMS_EPS) * g


def _layer_norm(x, g, b):
    mu = jnp.mean(x, -1, keepdims=True)
    xc = x - mu
    var = jnp.mean(jnp.square(xc), -1, keepdims=True)
    return xc * lax.rsqrt(var + LN_EPS) * g + b


def _attn_inproj_kernel(x_ref, w_ref, wqbt_ref, wvbt_ref, qn_ref, wuqt_ref, kvn_ref, wuk_ref, wuvt_ref,
                        cost_ref, sint_ref, ccs_ref, sn_ref,
                        qt_ref, k_ref, vt_ref, qbt_ref, kb_ref, vbt_ref):
    xb = x_ref[...].astype(_BF16)
    proj = _dot(xb, w_ref[...])
    q_lat = proj[:, 0:256]
    kv_lat = proj[:, 256:384]
    kr = proj[:, 384:512]
    qn = _rms(q_lat, qn_ref[...]).astype(_BF16)
    qt = _dot_nt(wuqt_ref[...], qn) * (MLA_QK ** -0.5 * _LOG2E)
    cos_t = cost_ref[...]
    sin_t = sint_ref[...]
    tm = qt.shape[1]
    zpad = jnp.zeros((HEAD_PAD - MLA_QK, tm), _F32)
    for h in range(MLA_HEADS):
        r0 = h * HEAD_PAD
        x1 = qt[r0 + 64:r0 + 80, :]
        x2 = qt[r0 + 80:r0 + 96, :]
        blk = jnp.concatenate([qt[r0:r0 + 64, :], x1 * cos_t - x2 * sin_t, x1 * sin_t + x2 * cos_t, zpad], axis=0)
        qt_ref[0, r0:r0 + HEAD_PAD, :] = blk.astype(_BF16)
    kvn = _rms(kv_lat, kvn_ref[...]).astype(_BF16)
    kn = _dot(kvn, wuk_ref[...])
    lane = lax.broadcasted_iota(jnp.int32, kr.shape, 1)
    swapped = jnp.where(lane < 16, pltpu.roll(kr, 112, 1), pltpu.roll(kr, 16, 1))
    kpe = kr * ccs_ref[...] + swapped * sn_ref[...]
    kpe = pltpu.roll(kpe, 64, 1)
    k_ref[0] = (kn + jnp.concatenate([kpe] * MLA_HEADS, axis=1)).astype(_BF16)
    vt = _dot_nt(wuvt_ref[...], kvn).astype(_BF16)
    extra = (lax.broadcasted_iota(jnp.int32, (VT_ROWS - MLA_V, tm), 0) == 0).astype(_BF16)
    for h in range(MLA_HEADS):
        vt_ref[0, h * VT_ROWS:h * VT_ROWS + MLA_V, :] = vt[h * MLA_V:(h + 1) * MLA_V, :]
        vt_ref[0, h * VT_ROWS + MLA_V:(h + 1) * VT_ROWS, :] = extra
    kb_ref[0] = proj[:, 512:640].astype(_BF16)
    vbt_ref[0] = _dot_nt(wvbt_ref[...], xb).astype(_BF16)
    qbt = (_dot_nt(wqbt_ref[...], xb) * (SWA_D ** -0.5)).astype(_BF16)
    zhalf = jnp.zeros((SWA_D, tm), _BF16)
    for h in range(SWA_HEADS):
        g = h // SWA_REP
        real = qbt[h * SWA_D:(h + 1) * SWA_D, :]
        qbt_ref[0, h * LANE + g * SWA_D:h * LANE + (g + 1) * SWA_D, :] = real
        qbt_ref[0, h * LANE + (1 - g) * SWA_D:h * LANE + (2 - g) * SWA_D, :] = zhalf


def _attn_inproj(x2d, w, wqbt, wvbt, qn, wuqt, kvn, wuk, wuvt, cos_t, sin_t, ccs, sn, bsz, seq, tm):
    nst = seq // tm
    const = lambda shape: pl.BlockSpec(shape, lambda b, i: (0,) * len(shape))
    out_shape = (
        jax.ShapeDtypeStruct((bsz, MLA_HEADS * HEAD_PAD, seq), _BF16),
        jax.ShapeDtypeStruct((bsz, seq, MLA_HEADS * HEAD_PAD), _BF16),
        jax.ShapeDtypeStruct((bsz, MLA_HEADS * VT_ROWS, seq), _BF16),
        jax.ShapeDtypeStruct((bsz, SWA_HEADS * LANE, seq), _BF16),
        jax.ShapeDtypeStruct((bsz, seq, SWA_KV_HEADS * SWA_D), _BF16),
        jax.ShapeDtypeStruct((bsz, SWA_KV_HEADS * SWA_D, seq), _BF16),
    )
    return pl.pallas_call(
        _attn_inproj_kernel,
        out_shape=out_shape,
        grid=(bsz, nst),
        in_specs=[
            pl.BlockSpec((tm, x2d.shape[1]), lambda b, i: (b * nst + i, 0)),
            const(w.shape), const(wqbt.shape), const(wvbt.shape), const(qn.shape), const(wuqt.shape),
            const(kvn.shape), const(wuk.shape), const(wuvt.shape),
            pl.BlockSpec((16, tm), lambda b, i: (0, i)),
            pl.BlockSpec((16, tm), lambda b, i: (0, i)),
            pl.BlockSpec((tm, LANE), lambda b, i: (i, 0)),
            pl.BlockSpec((tm, LANE), lambda b, i: (i, 0)),
        ],
        out_specs=(
            pl.BlockSpec((1, MLA_HEADS * HEAD_PAD, tm), lambda b, i: (b, 0, i)),
            pl.BlockSpec((1, tm, MLA_HEADS * HEAD_PAD), lambda b, i: (b, i, 0)),
            pl.BlockSpec((1, MLA_HEADS * VT_ROWS, tm), lambda b, i: (b, 0, i)),
            pl.BlockSpec((1, SWA_HEADS * LANE, tm), lambda b, i: (b, 0, i)),
            pl.BlockSpec((1, tm, SWA_KV_HEADS * SWA_D), lambda b, i: (b, i, 0)),
            pl.BlockSpec((1, SWA_KV_HEADS * SWA_D, tm), lambda b, i: (b, 0, i)),
        ),
        name="attn_inproj",
    )(x2d, w, wqbt, wvbt, qn, wuqt, kvn, wuk, wuvt, cos_t, sin_t, ccs, sn)


def _mla_flash_kernel(qt_ref, k_ref, vt_ref, ot_ref, *scratch, tk, nstream):
    st_sc = scratch[:2 * nstream]
    p_sc = scratch[2 * nstream:4 * nstream]
    tq = qt_ref.shape[2] // nstream
    nkv = k_ref.shape[1] // tk
    assert nkv % 2 == 0
    qts = [qt_ref[0, :, s * tq:(s + 1) * tq] for s in range(nstream)]

    def keys(j):
        return k_ref[0, pl.ds(pl.multiple_of(j * tk, tk), tk), :]

    def values(j):
        return vt_ref[0, :, pl.ds(pl.multiple_of(j * tk, tk), tk)]

    def phase(j, cur, state):
        nxt = 1 - cur
        kt = keys(jnp.minimum(j + 1, nkv - 1))
        vt = values(jnp.maximum(j - 1, 0))
        out = []
        for s in range(nstream):
            m, acc = state[s]
            st_sc[2 * s + nxt][...] = _dot(kt, qts[s])
            pv = _dot(vt, p_sc[2 * s + nxt][...])
            st = st_sc[2 * s + cur][...]
            m_new = jnp.maximum(m, jnp.max(st, axis=0, keepdims=True))
            alpha = jnp.exp2(m - m_new)
            p_sc[2 * s + cur][...] = jnp.exp2(st - m_new).astype(_BF16)
            out.append((m_new, (acc + pv) * alpha))
        return out

    def body(jj, state):
        state = phase(2 * jj, 0, state)
        return phase(2 * jj + 1, 1, state)

    k0 = keys(0)
    state = []
    for s in range(nstream):
        st_sc[2 * s][...] = _dot(k0, qts[s])
        p_sc[2 * s + 1][...] = jnp.zeros((tk, tq), _BF16)
        state.append((jnp.full((1, tq), -jnp.inf, _F32), jnp.zeros((VT_ROWS, tq), _F32)))
    state = lax.fori_loop(0, nkv // 2, body, state)
    v_last = values(nkv - 1)
    for s in range(nstream):
        acc = state[s][1] + _dot(v_last, p_sc[2 * s + 1][...])
        ot_ref[0, :, s * tq:(s + 1) * tq] = (acc[0:MLA_V, :] / acc[MLA_V:MLA_V + 1, :]).astype(ot_ref.dtype)


def _mla_flash(qt, k, vt, tq, tk, nstream):
    bsz, _, seq = qt.shape
    tqs = tq * nstream
    scratch = [pltpu.VMEM((tk, tq), _F32)] * (2 * nstream) + [pltpu.VMEM((tk, tq), _BF16)] * (2 * nstream)
    return pl.pallas_call(
        functools.partial(_mla_flash_kernel, tk=tk, nstream=nstream),
        out_shape=jax.ShapeDtypeStruct((bsz, MLA_HEADS * MLA_V, seq), _BF16),
        grid=(bsz, MLA_HEADS, seq // tqs),
        in_specs=[
            pl.BlockSpec((1, HEAD_PAD, tqs), lambda b, h, i: (b, h, i)),
            pl.BlockSpec((1, seq, HEAD_PAD), lambda b, h, i: (b, 0, h)),
            pl.BlockSpec((1, VT_ROWS, seq), lambda b, h, i: (b, h, 0)),
        ],
        out_specs=pl.BlockSpec((1, MLA_V, tqs), lambda b, h, i: (b, h, i)),
        scratch_shapes=scratch,
        name="mla_flash",
    )(qt, k, vt)


def _swa_kernel(relb_ref, sinks_ref, qt_ref, k_ref, vt_ref, bucket_ref, o_ref, bias_sc, *, nblk, seq):
    first = jnp.logical_and(pl.program_id(0) == 0, pl.program_id(1) == 0)

    @pl.when(first)
    def _build_bias():
        bucket = bucket_ref[...]
        for h in range(SWA_HEADS):
            acc = jnp.full(bucket.shape, NEG_BIG, _F32)
            for bk in range(REL_BUCKETS):
                acc = jnp.where(bucket == bk, relb_ref[bk, h], acc)
            bias_sc[h] = acc

    j = pl.program_id(1)
    krow = lax.broadcasted_iota(jnp.int32, (3 * BLOCK, 1), 0)

    def block(u, c):
        n = j * nblk + u
        c0 = pl.multiple_of(u * BLOCK, BLOCK)
        w0 = pl.multiple_of(n * BLOCK, BLOCK)
        key_pos = n * BLOCK - BLOCK + krow
        emask = jnp.where(jnp.logical_and(key_pos >= 0, key_pos < seq), 0.0, NEG_BIG).astype(_F32)
        kw = k_ref[0, pl.ds(w0, 3 * BLOCK), :]
        vw = vt_ref[0, :, pl.ds(w0, 3 * BLOCK)]
        for g in range(SWA_KV_HEADS):
            heads = range(g * SWA_REP, (g + 1) * SWA_REP)
            qs = jnp.concatenate([qt_ref[0, h * LANE:(h + 1) * LANE, pl.ds(c0, BLOCK)] for h in heads], axis=1)
            bias = jnp.concatenate([bias_sc[h] for h in heads], axis=1)
            sink = jnp.concatenate([jnp.full((1, BLOCK), sinks_ref[h], _F32) for h in heads], axis=1)
            st = _dot(kw, qs) + bias + emask
            m = jnp.maximum(jnp.max(st, axis=0, keepdims=True), sink)
            p = jnp.exp(st - m)
            den = jnp.sum(p, axis=0, keepdims=True) + jnp.exp(sink - m)
            ot = _dot(vw[g * SWA_D:(g + 1) * SWA_D, :], p.astype(_BF16)) / den
            for r, h in enumerate(heads):
                o_ref[0, h * SWA_D:(h + 1) * SWA_D, pl.ds(c0, BLOCK)] = ot[:, r * BLOCK:(r + 1) * BLOCK].astype(o_ref.dtype)
        return c

    def pair(t, c):
        block(2 * t, c)
        return block(2 * t + 1, c)

    if nblk % 2 == 0:
        lax.fori_loop(0, nblk // 2, pair, 0)
    else:
        lax.fori_loop(0, nblk, block, 0)


def _swa(qbt, kpad, vtpad, bucket_t, rel_bias, sinks, bsz, seq, nblk):
    nsteps = seq // (nblk * BLOCK)
    cols = nblk * BLOCK
    return pl.pallas_call(
        functools.partial(_swa_kernel, nblk=nblk, seq=seq),
        out_shape=jax.ShapeDtypeStruct((bsz, SWA_HEADS * SWA_D, seq), _BF16),
        grid=(bsz, nsteps),
        in_specs=[
            pl.BlockSpec(memory_space=pltpu.SMEM),
            pl.BlockSpec(memory_space=pltpu.SMEM),
            pl.BlockSpec((1, SWA_HEADS * LANE, cols), lambda b, j: (b, 0, j)),
            pl.BlockSpec((1, seq + 2 * BLOCK, SWA_KV_HEADS * SWA_D), lambda b, j: (b, 0, 0)),
            pl.BlockSpec((1, SWA_KV_HEADS * SWA_D, seq + 2 * BLOCK), lambda b, j: (b, 0, 0)),
            pl.BlockSpec((3 * BLOCK, BLOCK), lambda b, j: (0, 0)),
        ],
        out_specs=pl.BlockSpec((1, SWA_HEADS * SWA_D, cols), lambda b, j: (b, 0, j)),
        scratch_shapes=[pltpu.VMEM((SWA_HEADS, 3 * BLOCK, BLOCK), _F32)],
        compiler_params=pltpu.CompilerParams(dimension_semantics=("arbitrary", "arbitrary")),
        name="swa",
    )(rel_bias, sinks, qbt, kpad, vtpad, bucket_t)


def _route(logits_t, rbias):
    sc = jax.nn.sigmoid(logits_t)
    bz = sc + rbias
    s_rows = [sc[e:e + 1, :] for e in range(N_EXPERTS)]
    b_rows = [bz[e:e + 1, :] for e in range(N_EXPERTS)]
    gsel = None
    best = None
    for g in range(N_GROUPS):
        r = b_rows[4 * g:4 * g + 4]
        gs = r[0] + r[1]
        for (i, k) in ((0, 2), (0, 3), (1, 2), (1, 3), (2, 3)):
            gs = jnp.maximum(gs, r[i] + r[k])
        if g == 0:
            gsel = jnp.zeros(gs.shape, jnp.int32)
            best = gs
        else:
            better = gs > best
            gsel = jnp.where(better, g, gsel)
            best = jnp.where(better, gs, best)

    def pick(rows, k):
        out = rows[12 + k]
        for g in (2, 1, 0):
            out = jnp.where(gsel == g, rows[4 * g + k], out)
        return out

    v = [pick(b_rows, k) for k in range(4)]
    s = [pick(s_rows, k) for k in range(4)]
    i1 = jnp.zeros(gsel.shape, jnp.int32)
    m1 = v[0]
    w1 = s[0]
    for k in range(1, 4):
        gt = v[k] > m1
        i1 = jnp.where(gt, k, i1)
        m1 = jnp.where(gt, v[k], m1)
        w1 = jnp.where(gt, s[k], w1)
    i2 = jnp.full(gsel.shape, -1, jnp.int32)
    m2 = jnp.full(m1.shape, -jnp.inf, _F32)
    w2 = jnp.zeros(m1.shape, _F32)
    for k in range(4):
        ok = jnp.logical_and(i1 != k, jnp.logical_or(i2 < 0, v[k] > m2))
        i2 = jnp.where(ok, k, i2)
        m2 = jnp.where(ok, v[k], m2)
        w2 = jnp.where(ok, s[k], w2)
    tot = w1 + w2
    g1 = w1 / tot
    g2 = w2 / tot
    first_lo = i1 < i2
    lo = jnp.where(first_lo, i1, i2)
    hi = jnp.where(first_lo, i2, i1)
    pair = jnp.where(lo == 0, hi - 1, jnp.where(lo == 1, hi + 1, 5))
    cls = gsel * 6 + pair
    return cls, jnp.where(first_lo, g1, g2), jnp.where(first_lo, g2, g1)


def _ln_router_store(z, g_ref, b_ref, rwt_ref, rb_ref, h_ref, cls_ref):
    h = _layer_norm(z, g_ref[...], b_ref[...])
    tm = h.shape[0]
    logits_t = lax.dot_general(rwt_ref[...], h, _NT_DIMS, precision=lax.Precision.HIGHEST,
                               preferred_element_type=_F32)
    cls, g_lo, g_hi = _route(logits_t, rb_ref[...])
    rows = jnp.concatenate([g_lo, g_hi, jnp.zeros((GATE_LANES - 2, tm), _F32)], axis=0)
    d = h.shape[1]
    h_ref[:, 0:d] = h
    h_ref[:, d:d + GATE_LANES] = rows.T
    cls_ref[0] = cls


def _outproj_ln_router_kernel(ot_ref, ob_ref, x_ref, w_ref, g_ref, b_ref, rwt_ref, rb_ref, h_ref, cls_ref):
    heads_t = jnp.concatenate([ot_ref[0], ob_ref[0]], axis=0)
    mixed = lax.dot_general(heads_t, w_ref[...], _TN_DIMS, preferred_element_type=_F32)
    z = ALPHA * x_ref[...] + mixed
    _ln_router_store(z, g_ref, b_ref, rwt_ref, rb_ref, h_ref, cls_ref)


def _outproj_ln_router(ot, ob, x2d, w, g, b, rwt, rb, bsz, seq, tm):
    nst = seq // tm
    d = x2d.shape[1]
    const = lambda shape: pl.BlockSpec(shape, lambda bb, i: (0,) * len(shape))
    return pl.pallas_call(
        _outproj_ln_router_kernel,
        out_shape=(jax.ShapeDtypeStruct((bsz * seq, d + GATE_LANES), _F32),
                   jax.ShapeDtypeStruct((bsz * nst, 1, tm), jnp.int32)),
        grid=(bsz, nst),
        in_specs=[
            pl.BlockSpec((1, ot.shape[1], tm), lambda bb, i: (bb, 0, i)),
            pl.BlockSpec((1, ob.shape[1], tm), lambda bb, i: (bb, 0, i)),
            pl.BlockSpec((tm, d), lambda bb, i: (bb * nst + i, 0)),
            const(w.shape), const(g.shape), const(b.shape), const(rwt.shape), const(rb.shape),
        ],
        out_specs=(pl.BlockSpec((tm, d + GATE_LANES), lambda bb, i: (bb * nst + i, 0)),
                   pl.BlockSpec((1, 1, tm), lambda bb, i: (bb * nst + i, 0, 0))),
        name="outproj_ln_router",
    )(ot, ob, x2d, w, g, b, rwt, rb)


_SUBLANES = 8
_SCAN_RADIX = 4
_DMA_CHUNK = 32


def _moe_kernel(ea_ref, eb_ref, nv_ref, src_ref, nsrc_ref, h_hbm, w1a, w3a, w2a, w1b, w3b, w2b, g_ref, b_ref,
                out_hbm, xbuf, obuf, sem_in, sem_out):
    i = pl.program_id(0)
    ntiles = pl.num_programs(0)
    tm = xbuf.shape[1] * _SUBLANES
    d = obuf.shape[2]
    nv = nv_ref[i]
    nv_prev = jnp.where(i > 0, nv_ref[jnp.maximum(i - 1, 0)], 0)
    nv_next = jnp.where(i + 1 < ntiles, nv_ref[jnp.minimum(i + 1, ntiles - 1)], 0)
    slot = lax.rem(i, 2)

    def hbm_row(ref, idx):
        return ref.at[lax.shift_right_logical(idx, 3), jnp.bitwise_and(idx, _SUBLANES - 1)]

    def gather_start(idx_ref, s):
        for r in range(tm):
            idx = idx_ref[0, 0, r]
            pltpu.make_async_copy(hbm_row(h_hbm, idx), xbuf.at[s, r // _SUBLANES, r % _SUBLANES], sem_in.at[s]).start()

    def scatter_wait(n):
        for bit in range(tm.bit_length()):
            rows = 1 << bit

            @pl.when(jnp.bitwise_and(n, rows) != 0)
            def _():
                if rows >= _SUBLANES:
                    grp = pl.ds(0, rows // _SUBLANES)
                    pltpu.make_async_copy(obuf.at[grp], out_hbm.at[grp], sem_out).wait()
                else:
                    pltpu.make_async_copy(obuf.at[0, pl.ds(0, rows)], out_hbm.at[0, pl.ds(0, rows)], sem_out).wait()

    @pl.when(jnp.logical_and(i == 0, nv > 0))
    def _prologue():
        gather_start(src_ref, 0)

    @pl.when(nv_next > 0)
    def _prefetch():
        gather_start(nsrc_ref, 1 - slot)

    @pl.when(nv > 0)
    def _tile():
        pltpu.make_async_copy(h_hbm.at[pl.ds(0, tm // _SUBLANES)], xbuf.at[slot], sem_in.at[slot]).wait()
        xg = xbuf[slot].reshape(tm, d + GATE_LANES)
        x = xg[:, 0:d]
        gates = xg[:, d:d + GATE_LANES]
        ga = gates[:, 0:1]
        gb = gates[:, 1:2]
        xb = x.astype(_BF16)

        def expert(w1, w3, w2):
            u = _dot(xb, w1[...])
            hh = u * _sigmoid(u) * _dot(xb, w3[...])
            return _dot(hh.astype(_BF16), w2[...])

        y = ga * expert(w1a, w3a, w2a) + gb * expert(w1b, w3b, w2b)
        z = _layer_norm(ALPHA * x + y, g_ref[...], b_ref[...])

        @pl.when(nv_prev > 0)
        def _drain_prev():
            scatter_wait(nv_prev)

        obuf[...] = z.reshape(tm // _SUBLANES, _SUBLANES, d)
        for c in range(tm // _DMA_CHUNK):
            @pl.when(nv >= (c + 1) * _DMA_CHUNK)
            def _chunk():
                for r in range(c * _DMA_CHUNK, (c + 1) * _DMA_CHUNK):
                    idx = src_ref[0, 0, r]
                    pltpu.make_async_copy(obuf.at[r // _SUBLANES, r % _SUBLANES], hbm_row(out_hbm, idx),
                                          sem_out).start()

        def group(gi, c):
            for u in range(_SUBLANES):
                idx = src_ref[0, 0, gi * _SUBLANES + u]
                pltpu.make_async_copy(obuf.at[gi, u], hbm_row(out_hbm, idx), sem_out).start()
            return c

        def single(r, c):
            idx = src_ref[0, 0, r]
            pltpu.make_async_copy(hbm_row(obuf, r), hbm_row(out_hbm, idx), sem_out).start()
            return c

        ngroups = lax.shift_right_logical(nv, 3)
        lax.fori_loop((nv // _DMA_CHUNK) * (_DMA_CHUNK // _SUBLANES), ngroups, group, 0)
        lax.fori_loop(ngroups * _SUBLANES, nv, single, 0)

        @pl.when(nv_next == 0)
        def _drain_last():
            scatter_wait(nv)


def _moe(he, ea, eb, nvalid, src, w1, w3, w2, g, b, n_tokens, d):
    ntiles = ea.shape[0]
    tm = MOE_TM
    dff = w1.shape[2]
    wspec_a = lambda shape: pl.BlockSpec((None,) + shape, lambda i, ea_r, eb_r, nv_r: (ea_r[i], 0, 0))
    wspec_b = lambda shape: pl.BlockSpec((None,) + shape, lambda i, ea_r, eb_r, nv_r: (eb_r[i], 0, 0))
    grid_spec = pltpu.PrefetchScalarGridSpec(
        num_scalar_prefetch=3,
        grid=(ntiles,),
        in_specs=[
            pl.BlockSpec((1, 1, tm), lambda i, *_: (i, 0, 0), memory_space=pltpu.SMEM),
            pl.BlockSpec((1, 1, tm), lambda i, *_: (jnp.minimum(i + 1, ntiles - 1), 0, 0), memory_space=pltpu.SMEM),
            pl.BlockSpec(memory_space=pl.ANY),
            wspec_a((d, dff)), wspec_a((d, dff)), wspec_a((dff, d)),
            wspec_b((d, dff)), wspec_b((d, dff)), wspec_b((dff, d)),
            pl.BlockSpec((1, d), lambda i, *_: (0, 0)),
            pl.BlockSpec((1, d), lambda i, *_: (0, 0)),
        ],
        out_specs=pl.BlockSpec(memory_space=pl.ANY),
        scratch_shapes=[
            pltpu.VMEM((2, tm // _SUBLANES, _SUBLANES, d + GATE_LANES), _F32),
            pltpu.VMEM((tm // _SUBLANES, _SUBLANES, d), _F32),
            pltpu.SemaphoreType.DMA((2,)),
            pltpu.SemaphoreType.DMA,
        ],
    )
    he3 = he.reshape(n_tokens // _SUBLANES, _SUBLANES, d + GATE_LANES)
    out = pl.pallas_call(
        _moe_kernel,
        out_shape=jax.ShapeDtypeStruct((n_tokens // _SUBLANES, _SUBLANES, d), _F32),
        grid_spec=grid_spec,
        compiler_params=pltpu.CompilerParams(dimension_semantics=("arbitrary",)),
        name="moe",
    )(ea, eb, nvalid, src, src, he3, w1, w3, w2, w1, w3, w2, g, b)
    return out.reshape(n_tokens, d)


def _moe_plan(cls, n_tokens):
    tm = MOE_TM
    ntiles = n_tokens // tm + N_CLASSES
    cls = cls.reshape(n_tokens)
    order = jnp.argsort(cls, stable=True).astype(jnp.int32)
    counts = jnp.sum((cls[:, None] == jnp.arange(N_CLASSES, dtype=jnp.int32)[None, :]).astype(jnp.int32), axis=0)
    cstart = jnp.cumsum(counts) - counts
    ptiles = (counts + tm - 1) // tm
    tend = jnp.cumsum(ptiles)
    tstart = tend - ptiles
    tile = jnp.arange(ntiles, dtype=jnp.int32)
    tcls = jnp.minimum(jnp.sum((tile[:, None] >= tend[None, :]).astype(jnp.int32), axis=1), N_CLASSES - 1)
    used = tile < tend[-1]
    nvalid = jnp.where(used, jnp.clip(counts[tcls] - (tile - tstart[tcls]) * tm, 0, tm), 0).astype(jnp.int32)
    r = jnp.arange(tm, dtype=jnp.int32)[None, :]
    valid = r < nvalid[:, None]
    pos = jnp.clip((cstart[tcls] + (tile - tstart[tcls]) * tm)[:, None] + r, 0, n_tokens - 1)
    src = jnp.where(valid, order[pos], 0).astype(jnp.int32).reshape(ntiles, 1, tm)
    pair_lo = jnp.array([0, 0, 0, 1, 1, 2], jnp.int32)
    pair_hi = jnp.array([1, 2, 3, 2, 3, 3], jnp.int32)
    ea = (4 * (tcls // 6) + pair_lo[tcls % 6]).astype(jnp.int32)
    eb = (4 * (tcls // 6) + pair_hi[tcls % 6]).astype(jnp.int32)
    return ea, eb, nvalid, src


def _lru_inproj_kernel(x_ref, w_ref, gate_ref, xr_ref):
    u = _dot(x_ref[...].astype(_BF16), w_ref[...])
    half = gate_ref.shape[1]
    gate_ref[...] = u[:, 0:half]
    xr_ref[...] = u[:, half:2 * half]


def _lru_inproj(h2d, w, tm):
    n, d = h2d.shape
    half = w.shape[1] // 2
    return pl.pallas_call(
        _lru_inproj_kernel,
        out_shape=(jax.ShapeDtypeStruct((n, half), _F32), jax.ShapeDtypeStruct((n, half), _F32)),
        grid=(n // tm,),
        in_specs=[pl.BlockSpec((tm, d), lambda i: (i, 0)), pl.BlockSpec(w.shape, lambda i: (0, 0))],
        out_specs=(pl.BlockSpec((tm, half), lambda i: (i, 0)), pl.BlockSpec((tm, half), lambda i: (i, 0))),
        name="lru_inproj",
    )(h2d, w)


def _scan_slabs(a_refs, b_refs, c_refs, h_in, reverse):
    nlev = len(a_refs)
    ks = list(range(_SCAN_RADIX))
    if reverse:
        ks.reverse()
    for l in range(nlev - 1):
        grp = a_refs[l].shape[1] // _SCAN_RADIX
        p = q = None
        for k in ks:
            sl = pl.ds(k, grp, stride=_SCAN_RADIX)
            a = a_refs[l][:, sl, :]
            b = b_refs[l][:, sl, :]
            if p is None:
                p, q = a, b
            else:
                p, q = a * p, a * q + b
                a_refs[l][:, sl, :] = p
                b_refs[l][:, sl, :] = q
        a_refs[l + 1][...] = p
        b_refs[l + 1][...] = q
    rows = a_refs[-1].shape[1]
    a = a_refs[-1][...]
    b = b_refs[-1][...]
    out = [None] * rows
    h = h_in
    for r in (range(rows - 1, -1, -1) if reverse else range(rows)):
        h = a[:, r:r + 1, :] * h + b[:, r:r + 1, :]
        out[r] = h
    h_out = h
    b_refs[-1][...] = jnp.concatenate(out, axis=1)
    edge = jnp.broadcast_to(h_in, (h_in.shape[0], _SUBLANES, h_in.shape[2]))
    for l in range(nlev - 2, -1, -1):
        grp = a_refs[l].shape[1] // _SCAN_RADIX
        c = c_refs[l]
        c[:, _SUBLANES:_SUBLANES + grp, :] = b_refs[l + 1][...]
        if reverse:
            c[:, _SUBLANES + grp:2 * _SUBLANES + grp, :] = edge
            cin = c[:, _SUBLANES + 1:_SUBLANES + 1 + grp, :]
        else:
            c[:, 0:_SUBLANES, :] = edge
            cin = c[:, _SUBLANES - 1:_SUBLANES - 1 + grp, :]
        for k in ks:
            sl = pl.ds(k, grp, stride=_SCAN_RADIX)
            b_refs[l][:, sl, :] = a_refs[l][:, sl, :] * cin + b_refs[l][:, sl, :]
    return h_out


def _rglru_kernel(*refs, reverse, final, ntile):
    if final:
        (xr_ref, prev_ref, next_ref, cw_ref, cb_ref, wa_ref, ba_ref, wx_ref, bx_ref, lam_ref,
         hf_ref, gate_ref, res_ref, wout_ref, g_ref, b_ref, rwt_ref, rb_ref,
         h_ref, cls_ref, xe_sc, carry_sc, *scan_sc) = refs
    else:
        (xr_ref, prev_ref, next_ref, cw_ref, cb_ref, wa_ref, ba_ref, wx_ref, bx_ref, lam_ref,
         hout_ref, xe_sc, carry_sc, *scan_sc) = refs
    nlev = (len(scan_sc) + 1) // 3
    a_refs, b_refs, c_refs = scan_sc[:nlev], scan_sc[nlev:2 * nlev], scan_sc[2 * nlev:]
    i = pl.program_id(1)
    ti = (ntile - 1 - i) if reverse else i
    tt = xr_ref.shape[0]

    @pl.when(i == 0)
    def _reset():
        carry_sc[...] = jnp.zeros_like(carry_sc)

    prev = jnp.where(ti > 0, prev_ref[...], 0.0)
    nxt = jnp.where(ti < ntile - 1, next_ref[...], 0.0)
    for n in range(LRU_BLOCKS):
        sl = slice(n * LRU_BW, (n + 1) * LRU_BW)
        xe_sc[n, 0:_SUBLANES, :] = prev[:, sl]
        xe_sc[n, _SUBLANES:_SUBLANES + tt, :] = xr_ref[:, sl]
        xe_sc[n, _SUBLANES + tt:2 * _SUBLANES + tt, :] = nxt[:, sl]
    cw = cw_ref[...]
    cb = cb_ref[...]
    half_decay = (0.5 * LRU_C) * -jnp.log(1.0 + jnp.exp(-lam_ref[...]))
    for n in range(LRU_BLOCKS):
        sl = slice(n * LRU_BW, (n + 1) * LRU_BW)
        xcn = cb[:, sl]
        for k in range(cw.shape[0]):
            xcn = xcn + cw[k:k + 1, sl] * xe_sc[n, _SUBLANES - CONV_LEFT + k:_SUBLANES - CONV_LEFT + k + tt, :]
        xb = xcn.astype(_BF16)
        tr = jnp.tanh(_dot(xb, wa_ref[n]) + ba_ref[:, sl])
        ig = 0.5 * jnp.tanh(_dot(xb, wx_ref[n]) + bx_ref[:, sl]) + 0.5
        log_a = tr * half_decay[:, sl] + half_decay[:, sl]
        th = jnp.tanh(log_a)
        num = -2.0 * th
        scale = jnp.where(num > 0.0, num * lax.rsqrt(num * (1.0 - th)), 0.0)
        a_refs[0][n] = jnp.exp(log_a)
        b_refs[0][n] = scale * (ig * xcn)
    carry_sc[...] = _scan_slabs(a_refs, b_refs, c_refs, carry_sc[...], reverse)
    ys = []
    for n in range(LRU_BLOCKS):
        sl = slice(n * LRU_BW, (n + 1) * LRU_BW)
        h = b_refs[0][n]
        if final:
            ys.append((hf_ref[:, sl] + h) * jax.nn.gelu(gate_ref[:, sl], approximate=True))
        else:
            hout_ref[:, sl] = h
    if final:
        y = jnp.concatenate(ys, axis=1).astype(_BF16)
        z = ALPHA * res_ref[...] + _dot(y, wout_ref[...])
        _ln_router_store(z, g_ref, b_ref, rwt_ref, rb_ref, h_ref, cls_ref)


def _rglru(xr, cw, cb, wa, ba, wx, bx, lam, bsz, seq, tt, reverse, tail=None):
    ntile = seq // tt
    width = xr.shape[1]
    final = tail is not None
    tidx = (lambda i: ntile - 1 - i) if reverse else (lambda i: i)
    row = lambda b, i: (b * ntile + tidx(i), 0)
    g8 = tt // 8
    nb8 = seq // 8

    def prev_map(b, i):
        return (b * nb8 + jnp.maximum(tidx(i) * g8 - 1, 0), 0)

    def next_map(b, i):
        return (b * nb8 + jnp.minimum((tidx(i) + 1) * g8, nb8 - 1), 0)

    const = lambda shape: pl.BlockSpec(shape, lambda b, i: (0,) * len(shape))
    in_specs = [
        pl.BlockSpec((tt, width), row),
        pl.BlockSpec((8, width), prev_map),
        pl.BlockSpec((8, width), next_map),
        const(cw.shape), const(cb.shape), const(wa.shape), const(ba.shape), const(wx.shape), const(bx.shape),
        const(lam.shape),
    ]
    args = [xr, xr, xr, cw, cb, wa, ba, wx, bx, lam]
    rows = [tt]
    while rows[-1] > _SUBLANES:
        assert rows[-1] % _SCAN_RADIX == 0
        rows.append(rows[-1] // _SCAN_RADIX)
    assert rows[-1] == _SUBLANES
    slab = lambda r: pltpu.VMEM((LRU_BLOCKS, r, LRU_BW), _F32)
    scratch = [slab(tt + 2 * _SUBLANES), pltpu.VMEM((LRU_BLOCKS, 1, LRU_BW), _F32)]
    scratch += [slab(r) for r in rows] * 2 + [slab(r + 2 * _SUBLANES) for r in rows[1:]]
    if final:
        hf, gate, res, wout, g, b, rwt, rb = tail
        d = res.shape[1]
        in_specs += [pl.BlockSpec((tt, width), row), pl.BlockSpec((tt, width), row), pl.BlockSpec((tt, d), row),
                     const(wout.shape), const(g.shape), const(b.shape), const(rwt.shape), const(rb.shape)]
        args += [hf, gate, res, wout, g, b, rwt, rb]
        out_shape = (jax.ShapeDtypeStruct((bsz * seq, d + GATE_LANES), _F32),
                     jax.ShapeDtypeStruct((bsz * ntile, 1, tt), jnp.int32))
        out_specs = (pl.BlockSpec((tt, d + GATE_LANES), row),
                     pl.BlockSpec((1, 1, tt), lambda b, i: (b * ntile + tidx(i), 0, 0)))
    else:
        out_shape = jax.ShapeDtypeStruct((bsz * seq, width), _F32)
        out_specs = pl.BlockSpec((tt, width), row)
    return pl.pallas_call(
        functools.partial(_rglru_kernel, reverse=reverse, final=final, ntile=ntile),
        out_shape=out_shape,
        grid=(bsz, ntile),
        in_specs=in_specs,
        out_specs=out_specs,
        scratch_shapes=scratch,
        compiler_params=pltpu.CompilerParams(dimension_semantics=("arbitrary", "arbitrary")),
        name="rglru_bwd_tail" if final else "rglru_fwd",
    )(*args)


def _rotary_tables(seq):
    half = MLA_ROPE // 2
    inv_freq = ROPE_THETA ** (-jnp.arange(half, dtype=_F32) / half)
    ang = jnp.arange(seq, dtype=_F32)[:, None] * inv_freq[None, :]
    return jnp.cos(ang), jnp.sin(ang)


def _t5_bucket(rel):
    n_side = REL_BUCKETS // 2
    max_exact = n_side // 2
    dist = jnp.abs(rel)
    far = max_exact + (jnp.log(jnp.maximum(dist, 1).astype(_F32) / max_exact)
                       / math.log(REL_MAX_DIST / max_exact) * (n_side - max_exact)).astype(jnp.int32)
    far = jnp.minimum(far, n_side - 1)
    return jnp.where(rel > 0, n_side, 0) + jnp.where(dist < max_exact, dist, far)


def _band_buckets():
    rel = jnp.arange(3 * BLOCK)[None, :] - BLOCK - jnp.arange(BLOCK)[:, None]
    ids = jnp.where(jnp.abs(rel) <= WINDOW, _t5_bucket(rel), REL_BUCKETS).astype(jnp.int32)
    return jnp.bitwise_and(ids, 2 * REL_BUCKETS - 1)


def _pick_tile(seq, want):
    t = min(want, seq)
    while seq % t:
        t //= 2
    return t


def kernel(x, rel_bias, router_w, router_bias, l0_w_in, l0_q_norm, l0_w_uq, l0_kv_norm, l0_w_ukv, l0_sinks, l0_w_out, l0_ln1_g, l0_ln1_b, l0_w1, l0_w3, l0_w2, l0_ln2_g, l0_ln2_b, l1_w_in, l1_conv_w, l1_conv_b, l1_wa_f, l1_ba_f, l1_wx_f, l1_bx_f, l1_lam_f, l1_wa_b, l1_ba_b, l1_wx_b, l1_bx_b, l1_lam_b, l1_w_out, l1_ln1_g, l1_ln1_b, l1_w1, l1_w3, l1_w2, l1_ln2_g, l1_ln2_b):
    bsz, seq, d = x.shape
    n_tok = bsz * seq
    assert seq % BLOCK == 0 and d == LRU_BLOCKS * LRU_BW and n_tok % MOE_TM == 0
    x2d = x.reshape(n_tok, d)
    row = lambda v: v.reshape(1, -1).astype(_F32)
    rwt = router_w.astype(_F32).T
    rb = router_bias.astype(_F32).reshape(N_EXPERTS, 1)

    o = np.cumsum([0, MLA_Q_RANK, MLA_KV_RANK, MLA_ROPE, SWA_HEADS * SWA_D, SWA_KV_HEADS * SWA_D,
                   SWA_KV_HEADS * SWA_D])
    w_q, w_kv, w_kr, w_qb, w_kb, w_vb = [l0_w_in[:, o[i]:o[i + 1]] for i in range(6)]
    w_in0 = jnp.concatenate([w_q, w_kv, jnp.pad(w_kr, ((0, 0), (0, LANE - MLA_ROPE))), w_kb], axis=1).astype(_BF16)
    wuq = l0_w_uq.reshape(MLA_Q_RANK, MLA_HEADS, MLA_QK)
    wuqt = jnp.pad(wuq, ((0, 0), (0, 0), (0, HEAD_PAD - MLA_QK))).reshape(MLA_Q_RANK, MLA_HEADS * HEAD_PAD).T
    wukv = l0_w_ukv.reshape(MLA_KV_RANK, MLA_HEADS, MLA_NOPE + MLA_V)
    wuk = jnp.pad(wukv[:, :, :MLA_NOPE], ((0, 0), (0, 0), (0, HEAD_PAD - MLA_NOPE))).reshape(MLA_KV_RANK, -1)
    wuvt = wukv[:, :, MLA_NOPE:].reshape(MLA_KV_RANK, MLA_HEADS * MLA_V).T
    cos, sin = _rotary_tables(seq)
    ccs = jnp.pad(jnp.concatenate([cos, cos], axis=1), ((0, 0), (0, LANE - MLA_ROPE)))
    sn = jnp.pad(jnp.concatenate([-sin, sin], axis=1), ((0, 0), (0, LANE - MLA_ROPE)))

    tm = _pick_tile(seq, 512)
    qt, k, vt, qbt, kb, vbt = _attn_inproj(
        x2d, w_in0, w_qb.T.astype(_BF16), w_vb.T.astype(_BF16), row(l0_q_norm), wuqt.astype(_BF16),
        row(l0_kv_norm), wuk.astype(_BF16), wuvt.astype(_BF16), cos.T, sin.T, ccs, sn, bsz, seq, tm)
    nstream = 4 if seq % 2048 == 0 else (2 if seq % 1024 == 0 else 1)
    ot = _mla_flash(qt, k, vt, _pick_tile(seq // nstream, 512), _pick_tile(seq // 2, 512), nstream)
    kpad = jnp.pad(kb, ((0, 0), (BLOCK, BLOCK), (0, 0)))
    vtpad = jnp.pad(vbt, ((0, 0), (0, 0), (BLOCK, BLOCK)))
    obt = _swa(qbt, kpad, vtpad, _band_buckets().T, rel_bias.astype(_F32), l0_sinks.astype(_F32), bsz, seq,
               _pick_tile(seq // BLOCK, 8))
    he, cls = _outproj_ln_router(ot, obt, x2d, l0_w_out.astype(_BF16), row(l0_ln1_g), row(l0_ln1_b), rwt, rb,
                                 bsz, seq, tm)
    ea, eb, nvalid, src = _moe_plan(cls, n_tok)
    h = _moe(he, ea, eb, nvalid, src, l0_w1.astype(_BF16), l0_w3.astype(_BF16), l0_w2.astype(_BF16),
             row(l0_ln2_g), row(l0_ln2_b), n_tok, d)

    gate, xr = _lru_inproj(h, l1_w_in.astype(_BF16), tm)
    tt = _pick_tile(seq, 512)
    cw = l1_conv_w.astype(_F32)
    cb = row(l1_conv_b)
    halfw = lambda w: (0.5 * w).astype(_BF16)
    halfb = lambda v: 0.5 * row(v)
    hf = _rglru(xr, cw, cb, halfw(l1_wa_f), halfb(l1_ba_f), halfw(l1_wx_f), halfb(l1_bx_f),
                row(l1_lam_f), bsz, seq, tt, reverse=False)
    he, cls = _rglru(xr, cw, cb, halfw(l1_wa_b), halfb(l1_ba_b), halfw(l1_wx_b), halfb(l1_bx_b),
                     row(l1_lam_b), bsz, seq, tt, reverse=True,
                     tail=(hf, gate, h, l1_w_out.astype(_BF16), row(l1_ln1_g), row(l1_ln1_b), rwt, rb))
    ea, eb, nvalid, src = _moe_plan(cls, n_tok)
    h = _moe(he, ea, eb, nvalid, src, l1_w1.astype(_BF16), l1_w3.astype(_BF16), l1_w2.astype(_BF16),
             row(l1_ln2_g), row(l1_ln2_b), n_tok, d)
    return h.reshape(bsz, seq, d)
```

```python
import functools
import math

import jax
import jax.numpy as jnp
import numpy as np
from jax import lax
from jax.experimental import pallas as pl
from jax.experimental.pallas import tpu as pltpu

MLA_HEADS = 8
MLA_Q_RANK = 256
MLA_KV_RANK = 128
MLA_NOPE = 64
MLA_ROPE = 32
MLA_V = 64
MLA_QK = MLA_NOPE + MLA_ROPE
ROPE_THETA = 10000.0
SWA_HEADS = 8
SWA_KV_HEADS = 2
SWA_REP = SWA_HEADS // SWA_KV_HEADS
SWA_D = 64
WINDOW = 128
BLOCK = 128
REL_BUCKETS = 32
REL_MAX_DIST = 128
LRU_BLOCKS = 8
LRU_BW = 128
LRU_C = 8.0
CONV_LEFT = 2
N_EXPERTS = 16
N_GROUPS = 4
EXPERTS_PER_GROUP = 4
N_CLASSES = N_GROUPS * 6
DEPTH = 2
ALPHA = (2.0 * DEPTH) ** 0.25
LN_EPS = 1e-5
RMS_EPS = 1e-6
NEG_BIG = -1e30

LANE = 128
HEAD_PAD = 128
GATE_LANES = 128
VT_ROWS = 80
_LOG2E = math.log2(math.e)
MOE_TM = 256

_F32 = jnp.float32
_BF16 = jnp.bfloat16
_NT_DIMS = (((1,), (1,)), ((), ()))
_TN_DIMS = (((0,), (0,)), ((), ()))


def _dot(a, b):
    return jnp.dot(a, b, preferred_element_type=_F32)


def _dot_nt(a, b):
    return lax.dot_general(a, b, _NT_DIMS, preferred_element_type=_F32)


def _sigmoid(x):
    return 0.5 * jnp.tanh(0.5 * x) + 0.5


def _rms(x, g):
    return x * lax.rsqrt(jnp.mean(jnp.square(x), -1, keepdims=True) + RMS_EPS) * g


def _layer_norm(x, g, b):
    mu = jnp.mean(x, -1, keepdims=True)
    xc = x - mu
    var = jnp.mean(jnp.square(xc), -1, keepdims=True)
    return xc * lax.rsqrt(var + LN_EPS) * g + b


def _attn_inproj_kernel(x_ref, w_ref, wqbt_ref, wvbt_ref, qn_ref, wuqt_ref, kvn_ref, wuk_ref, wuvt_ref,
                        cost_ref, sint_ref, ccs_ref, sn_ref,
                        qt_ref, k_ref, vt_ref, qbt_ref, kb_ref, vbt_ref):
    xb = x_ref[...].astype(_BF16)
    proj = _dot(xb, w_ref[...])
    q_lat = proj[:, 0:256]
    kv_lat = proj[:, 256:384]
    kr = proj[:, 384:512]
    qn = _rms(q_lat, qn_ref[...]).astype(_BF16)
    qt = _dot_nt(wuqt_ref[...], qn) * (MLA_QK ** -0.5 * _LOG2E)
    cos_t = cost_ref[...]
    sin_t = sint_ref[...]
    tm = qt.shape[1]
    zpad = jnp.zeros((HEAD_PAD - MLA_QK, tm), _F32)
    for h in range(MLA_HEADS):
        r0 = h * HEAD_PAD
        x1 = qt[r0 + 64:r0 + 80, :]
        x2 = qt[r0 + 80:r0 + 96, :]
        blk = jnp.concatenate([qt[r0:r0 + 64, :], x1 * cos_t - x2 * sin_t, x1 * sin_t + x2 * cos_t, zpad], axis=0)
        qt_ref[0, r0:r0 + HEAD_PAD, :] = blk.astype(_BF16)
    kvn = _rms(kv_lat, kvn_ref[...]).astype(_BF16)
    kn = _dot(kvn, wuk_ref[...])
    lane = lax.broadcasted_iota(jnp.int32, kr.shape, 1)
    swapped = jnp.where(lane < 16, pltpu.roll(kr, 112, 1), pltpu.roll(kr, 16, 1))
    kpe = kr * ccs_ref[...] + swapped * sn_ref[...]
    kpe = pltpu.roll(kpe, 64, 1)
    k_ref[0] = (kn + jnp.concatenate([kpe] * MLA_HEADS, axis=1)).astype(_BF16)
    vt = _dot_nt(wuvt_ref[...], kvn).astype(_BF16)
    extra = (lax.broadcasted_iota(jnp.int32, (VT_ROWS - MLA_V, tm), 0) == 0).astype(_BF16)
    for h in range(MLA_HEADS):
        vt_ref[0, h * VT_ROWS:h * VT_ROWS + MLA_V, :] = vt[h * MLA_V:(h + 1) * MLA_V, :]
        vt_ref[0, h * VT_ROWS + MLA_V:(h + 1) * VT_ROWS, :] = extra
    kb_ref[0] = proj[:, 512:640].astype(_BF16)
    vbt_ref[0] = _dot_nt(wvbt_ref[...], xb).astype(_BF16)
    qbt = (_dot_nt(wqbt_ref[...], xb) * (SWA_D ** -0.5)).astype(_BF16)
    zhalf = jnp.zeros((SWA_D, tm), _BF16)
    for h in range(SWA_HEADS):
        g = h // SWA_REP
        real = qbt[h * SWA_D:(h + 1) * SWA_D, :]
        qbt_ref[0, h * LANE + g * SWA_D:h * LANE + (g + 1) * SWA_D, :] = real
        qbt_ref[0, h * LANE + (1 - g) * SWA_D:h * LANE + (2 - g) * SWA_D, :] = zhalf


def _attn_inproj(x2d, w, wqbt, wvbt, qn, wuqt, kvn, wuk, wuvt, cos_t, sin_t, ccs, sn, bsz, seq, tm):
    nst = seq // tm
    const = lambda shape: pl.BlockSpec(shape, lambda b, i: (0,) * len(shape))
    out_shape = (
        jax.ShapeDtypeStruct((bsz, MLA_HEADS * HEAD_PAD, seq), _BF16),
        jax.ShapeDtypeStruct((bsz, seq, MLA_HEADS * HEAD_PAD), _BF16),
        jax.ShapeDtypeStruct((bsz, MLA_HEADS * VT_ROWS, seq), _BF16),
        jax.ShapeDtypeStruct((bsz, SWA_HEADS * LANE, seq), _BF16),
        jax.ShapeDtypeStruct((bsz, seq, SWA_KV_HEADS * SWA_D), _BF16),
        jax.ShapeDtypeStruct((bsz, SWA_KV_HEADS * SWA_D, seq), _BF16),
    )
    return pl.pallas_call(
        _attn_inproj_kernel,
        out_shape=out_shape,
        grid=(bsz, nst),
        in_specs=[
            pl.BlockSpec((tm, x2d.shape[1]), lambda b, i: (b * nst + i, 0)),
            const(w.shape), const(wqbt.shape), const(wvbt.shape), const(qn.shape), const(wuqt.shape),
            const(kvn.shape), const(wuk.shape), const(wuvt.shape),
            pl.BlockSpec((16, tm), lambda b, i: (0, i)),
            pl.BlockSpec((16, tm), lambda b, i: (0, i)),
            pl.BlockSpec((tm, LANE), lambda b, i: (i, 0)),
            pl.BlockSpec((tm, LANE), lambda b, i: (i, 0)),
        ],
        out_specs=(
            pl.BlockSpec((1, MLA_HEADS * HEAD_PAD, tm), lambda b, i: (b, 0, i)),
            pl.BlockSpec((1, tm, MLA_HEADS * HEAD_PAD), lambda b, i: (b, i, 0)),
            pl.BlockSpec((1, MLA_HEADS * VT_ROWS, tm), lambda b, i: (b, 0, i)),
            pl.BlockSpec((1, SWA_HEADS * LANE, tm), lambda b, i: (b, 0, i)),
            pl.BlockSpec((1, tm, SWA_KV_HEADS * SWA_D), lambda b, i: (b, i, 0)),
            pl.BlockSpec((1, SWA_KV_HEADS * SWA_D, tm), lambda b, i: (b, 0, i)),
        ),
        name="attn_inproj",
    )(x2d, w, wqbt, wvbt, qn, wuqt, kvn, wuk, wuvt, cos_t, sin_t, ccs, sn)


def _mla_flash_kernel(qt_ref, k_ref, vt_ref, ot_ref, *scratch, tk, nstream):
    st_sc = scratch[:2 * nstream]
    p_sc = scratch[2 * nstream:4 * nstream]
    tq = qt_ref.shape[2] // nstream
    nkv = k_ref.shape[1] // tk
    assert nkv % 2 == 0
    qts = [qt_ref[0, :, s * tq:(s + 1) * tq] for s in range(nstream)]

    def keys(j):
        return k_ref[0, pl.ds(pl.multiple_of(j * tk, tk), tk), :]

    def values(j):
        return vt_ref[0, :, pl.ds(pl.multiple_of(j * tk, tk), tk)]

    def phase(j, cur, state):
        nxt = 1 - cur
        kt = keys(jnp.minimum(j + 1, nkv - 1))
        vt = values(jnp.maximum(j - 1, 0))
        out = []
        for s in range(nstream):
            m, acc = state[s]
            st_sc[2 * s + nxt][...] = _dot(kt, qts[s])
            pv = _dot(vt, p_sc[2 * s + nxt][...])
            st = st_sc[2 * s + cur][...]
            m_new = jnp.maximum(m, jnp.max(st, axis=0, keepdims=True))
            alpha = jnp.exp2(m - m_new)
            p_sc[2 * s + cur][...] = jnp.exp2(st - m_new).astype(_BF16)
            out.append((m_new, (acc + pv) * alpha))
        return out

    def body(jj, state):
        state = phase(2 * jj, 0, state)
        return phase(2 * jj + 1, 1, state)

    k0 = keys(0)
    state = []
    for s in range(nstream):
        st_sc[2 * s][...] = _dot(k0, qts[s])
        p_sc[2 * s + 1][...] = jnp.zeros((tk, tq), _BF16)
        state.append((jnp.full((1, tq), -jnp.inf, _F32), jnp.zeros((VT_ROWS, tq), _F32)))
    state = lax.fori_loop(0, nkv // 2, body, state)
    v_last = values(nkv - 1)
    for s in range(nstream):
        acc = state[s][1] + _dot(v_last, p_sc[2 * s + 1][...])
        ot_ref[0, :, s * tq:(s + 1) * tq] = (acc[0:MLA_V, :] / acc[MLA_V:MLA_V + 1, :]).astype(ot_ref.dtype)


def _mla_flash(qt, k, vt, tq, tk, nstream):
    bsz, _, seq = qt.shape
    tqs = tq * nstream
    scratch = [pltpu.VMEM((tk, tq), _F32)] * (2 * nstream) + [pltpu.VMEM((tk, tq), _BF16)] * (2 * nstream)
    return pl.pallas_call(
        functools.partial(_mla_flash_kernel, tk=tk, nstream=nstream),
        out_shape=jax.ShapeDtypeStruct((bsz, MLA_HEADS * MLA_V, seq), _BF16),
        grid=(bsz, MLA_HEADS, seq // tqs),
        in_specs=[
            pl.BlockSpec((1, HEAD_PAD, tqs), lambda b, h, i: (b, h, i)),
            pl.BlockSpec((1, seq, HEAD_PAD), lambda b, h, i: (b, 0, h)),
            pl.BlockSpec((1, VT_ROWS, seq), lambda b, h, i: (b, h, 0)),
        ],
        out_specs=pl.BlockSpec((1, MLA_V, tqs), lambda b, h, i: (b, h, i)),
        scratch_shapes=scratch,
        name="mla_flash",
    )(qt, k, vt)


def _swa_kernel(relb_ref, sinks_ref, qt_ref, k_ref, vt_ref, bucket_ref, o_ref, bias_sc, *, nblk, seq):
    first = jnp.logical_and(pl.program_id(0) == 0, pl.program_id(1) == 0)

    @pl.when(first)
    def _build_bias():
        bucket = bucket_ref[...]
        for h in range(SWA_HEADS):
            acc = jnp.full(bucket.shape, NEG_BIG, _F32)
            for bk in range(REL_BUCKETS):
                acc = jnp.where(bucket == bk, relb_ref[bk, h], acc)
            bias_sc[h] = acc

    j = pl.program_id(1)
    krow = lax.broadcasted_iota(jnp.int32, (3 * BLOCK, 1), 0)

    def block(u, c):
        n = j * nblk + u
        c0 = pl.multiple_of(u * BLOCK, BLOCK)
        w0 = pl.multiple_of(n * BLOCK, BLOCK)
        key_pos = n * BLOCK - BLOCK + krow
        emask = jnp.where(jnp.logical_and(key_pos >= 0, key_pos < seq), 0.0, NEG_BIG).astype(_F32)
        kw = k_ref[0, pl.ds(w0, 3 * BLOCK), :]
        vw = vt_ref[0, :, pl.ds(w0, 3 * BLOCK)]
        for g in range(SWA_KV_HEADS):
            heads = range(g * SWA_REP, (g + 1) * SWA_REP)
            qs = jnp.concatenate([qt_ref[0, h * LANE:(h + 1) * LANE, pl.ds(c0, BLOCK)] for h in heads], axis=1)
            bias = jnp.concatenate([bias_sc[h] for h in heads], axis=1)
            sink = jnp.concatenate([jnp.full((1, BLOCK), sinks_ref[h], _F32) for h in heads], axis=1)
            st = _dot(kw, qs) + bias + emask
            m = jnp.maximum(jnp.max(st, axis=0, keepdims=True), sink)
            p = jnp.exp(st - m)
            den = jnp.sum(p, axis=0, keepdims=True) + jnp.exp(sink - m)
            ot = _dot(vw[g * SWA_D:(g + 1) * SWA_D, :], p.astype(_BF16)) / den
            for r, h in enumerate(heads):
                o_ref[0, h * SWA_D:(h + 1) * SWA_D, pl.ds(c0, BLOCK)] = ot[:, r * BLOCK:(r + 1) * BLOCK].astype(o_ref.dtype)
        return c

    def pair(t, c):
        block(2 * t, c)
        return block(2 * t + 1, c)

    if nblk % 2 == 0:
        lax.fori_loop(0, nblk // 2, pair, 0)
    else:
        lax.fori_loop(0, nblk, block, 0)


def _swa(qbt, kpad, vtpad, bucket_t, rel_bias, sinks, bsz, seq, nblk):
    nsteps = seq // (nblk * BLOCK)
    cols = nblk * BLOCK
    return pl.pallas_call(
        functools.partial(_swa_kernel, nblk=nblk, seq=seq),
        out_shape=jax.ShapeDtypeStruct((bsz, SWA_HEADS * SWA_D, seq), _BF16),
        grid=(bsz, nsteps),
        in_specs=[
            pl.BlockSpec(memory_space=pltpu.SMEM),
            pl.BlockSpec(memory_space=pltpu.SMEM),
            pl.BlockSpec((1, SWA_HEADS * LANE, cols), lambda b, j: (b, 0, j)),
            pl.BlockSpec((1, seq + 2 * BLOCK, SWA_KV_HEADS * SWA_D), lambda b, j: (b, 0, 0)),
            pl.BlockSpec((1, SWA_KV_HEADS * SWA_D, seq + 2 * BLOCK), lambda b, j: (b, 0, 0)),
            pl.BlockSpec((3 * BLOCK, BLOCK), lambda b, j: (0, 0)),
        ],
        out_specs=pl.BlockSpec((1, SWA_HEADS * SWA_D, cols), lambda b, j: (b, 0, j)),
        scratch_shapes=[pltpu.VMEM((SWA_HEADS, 3 * BLOCK, BLOCK), _F32)],
        compiler_params=pltpu.CompilerParams(dimension_semantics=("arbitrary", "arbitrary")),
        name="swa",
    )(rel_bias, sinks, qbt, kpad, vtpad, bucket_t)


def _route(logits_t, rbias):
    sc = jax.nn.sigmoid(logits_t)
    bz = sc + rbias
    s_rows = [sc[e:e + 1, :] for e in range(N_EXPERTS)]
    b_rows = [bz[e:e + 1, :] for e in range(N_EXPERTS)]
    gsel = None
    best = None
    for g in range(N_GROUPS):
        r = b_rows[4 * g:4 * g + 4]
        gs = r[0] + r[1]
        for (i, k) in ((0, 2), (0, 3), (1, 2), (1, 3), (2, 3)):
            gs = jnp.maximum(gs, r[i] + r[k])
        if g == 0:
            gsel = jnp.zeros(gs.shape, jnp.int32)
            best = gs
        else:
            better = gs > best
            gsel = jnp.where(better, g, gsel)
            best = jnp.where(better, gs, best)

    def pick(rows, k):
        out = rows[12 + k]
        for g in (2, 1, 0):
            out = jnp.where(gsel == g, rows[4 * g + k], out)
        return out

    v = [pick(b_rows, k) for k in range(4)]
    s = [pick(s_rows, k) for k in range(4)]
    i1 = jnp.zeros(gsel.shape, jnp.int32)
    m1 = v[0]
    w1 = s[0]
    for k in range(1, 4):
        gt = v[k] > m1
        i1 = jnp.where(gt, k, i1)
        m1 = jnp.where(gt, v[k], m1)
        w1 = jnp.where(gt, s[k], w1)
    i2 = jnp.full(gsel.shape, -1, jnp.int32)
    m2 = jnp.full(m1.shape, -jnp.inf, _F32)
    w2 = jnp.zeros(m1.shape, _F32)
    for k in range(4):
        ok = jnp.logical_and(i1 != k, jnp.logical_or(i2 < 0, v[k] > m2))
        i2 = jnp.where(ok, k, i2)
        m2 = jnp.where(ok, v[k], m2)
        w2 = jnp.where(ok, s[k], w2)
    tot = w1 + w2
    g1 = w1 / tot
    g2 = w2 / tot
    first_lo = i1 < i2
    lo = jnp.where(first_lo, i1, i2)
    hi = jnp.where(first_lo, i2, i1)
    pair = jnp.where(lo == 0, hi - 1, jnp.where(lo == 1, hi + 1, 5))
    cls = gsel * 6 + pair
    return cls, jnp.where(first_lo, g1, g2), jnp.where(first_lo, g2, g1)


def _ln_router_store(z, g_ref, b_ref, rwt_ref, rb_ref, h_ref, cls_ref):
    h = _layer_norm(z, g_ref[...], b_ref[...])
    tm = h.shape[0]
    h_hi = h.astype(_BF16)
    h_lo = (h - h_hi.astype(_F32)).astype(_BF16)
    rw = rwt_ref[...]
    rw_hi = rw.astype(_BF16)
    rw_lo = (rw - rw_hi.astype(_F32)).astype(_BF16)
    part = _dot_nt(jnp.concatenate([rw_hi, rw_lo], axis=0), h_hi)
    logits_t = part[0:N_EXPERTS] + part[N_EXPERTS:2 * N_EXPERTS] + _dot_nt(rw_hi, h_lo)
    cls, g_lo, g_hi = _route(logits_t, rb_ref[...])
    rows = jnp.concatenate([g_lo, g_hi, jnp.zeros((GATE_LANES - 2, tm), _F32)], axis=0)
    d = h.shape[1]
    h_ref[:, 0:d] = h
    h_ref[:, d:d + GATE_LANES] = rows.T
    cls_ref[0] = cls


def _outproj_ln_router_kernel(ot_ref, ob_ref, x_ref, w_ref, g_ref, b_ref, rwt_ref, rb_ref, h_ref, cls_ref):
    heads_t = jnp.concatenate([ot_ref[0], ob_ref[0]], axis=0)
    mixed = lax.dot_general(heads_t, w_ref[...], _TN_DIMS, preferred_element_type=_F32)
    z = ALPHA * x_ref[...] + mixed
    _ln_router_store(z, g_ref, b_ref, rwt_ref, rb_ref, h_ref, cls_ref)


def _outproj_ln_router(ot, ob, x2d, w, g, b, rwt, rb, bsz, seq, tm):
    nst = seq // tm
    d = x2d.shape[1]
    const = lambda shape: pl.BlockSpec(shape, lambda bb, i: (0,) * len(shape))
    return pl.pallas_call(
        _outproj_ln_router_kernel,
        out_shape=(jax.ShapeDtypeStruct((bsz * seq, d + GATE_LANES), _F32),
                   jax.ShapeDtypeStruct((bsz * nst, 1, tm), jnp.int32)),
        grid=(bsz, nst),
        in_specs=[
            pl.BlockSpec((1, ot.shape[1], tm), lambda bb, i: (bb, 0, i)),
            pl.BlockSpec((1, ob.shape[1], tm), lambda bb, i: (bb, 0, i)),
            pl.BlockSpec((tm, d), lambda bb, i: (bb * nst + i, 0)),
            const(w.shape), const(g.shape), const(b.shape), const(rwt.shape), const(rb.shape),
        ],
        out_specs=(pl.BlockSpec((tm, d + GATE_LANES), lambda bb, i: (bb * nst + i, 0)),
                   pl.BlockSpec((1, 1, tm), lambda bb, i: (bb * nst + i, 0, 0))),
        name="outproj_ln_router",
    )(ot, ob, x2d, w, g, b, rwt, rb)


_SUBLANES = 8
_SCAN_RADIX = 4
_DMA_CHUNK = 32


def _moe_kernel(ea_ref, eb_ref, nv_ref, src_ref, nsrc_ref, h_hbm, w13a, w2a, w13b, w2b, g_ref, b_ref,
                out_hbm, xbuf, obuf, sem_in, sem_out):
    i = pl.program_id(0)
    ntiles = pl.num_programs(0)
    tm = xbuf.shape[1] * _SUBLANES
    d = obuf.shape[2]
    nv = nv_ref[i]
    nv_prev = jnp.where(i > 0, nv_ref[jnp.maximum(i - 1, 0)], 0)
    nv_next = jnp.where(i + 1 < ntiles, nv_ref[jnp.minimum(i + 1, ntiles - 1)], 0)
    slot = lax.rem(i, 2)

    def hbm_row(ref, idx):
        return ref.at[lax.shift_right_logical(idx, 3), jnp.bitwise_and(idx, _SUBLANES - 1)]

    def gather_start(idx_ref, s):
        for r in range(tm):
            idx = idx_ref[0, 0, r]
            pltpu.make_async_copy(hbm_row(h_hbm, idx), xbuf.at[s, r // _SUBLANES, r % _SUBLANES], sem_in.at[s]).start()

    def scatter_wait(n):
        for bit in range(tm.bit_length()):
            rows = 1 << bit

            @pl.when(jnp.bitwise_and(n, rows) != 0)
            def _():
                if rows >= _SUBLANES:
                    grp = pl.ds(0, rows // _SUBLANES)
                    pltpu.make_async_copy(obuf.at[grp], out_hbm.at[grp], sem_out).wait()
                else:
                    pltpu.make_async_copy(obuf.at[0, pl.ds(0, rows)], out_hbm.at[0, pl.ds(0, rows)], sem_out).wait()

    @pl.when(jnp.logical_and(i == 0, nv > 0))
    def _prologue():
        gather_start(src_ref, 0)

    @pl.when(nv_next > 0)
    def _prefetch():
        gather_start(nsrc_ref, 1 - slot)

    @pl.when(nv > 0)
    def _tile():
        pltpu.make_async_copy(h_hbm.at[pl.ds(0, tm // _SUBLANES)], xbuf.at[slot], sem_in.at[slot]).wait()
        xg = xbuf[slot].reshape(tm, d + GATE_LANES)
        x = xg[:, 0:d]
        gates = xg[:, d:d + GATE_LANES]
        ga = gates[:, 0:1]
        gb = gates[:, 1:2]
        xb = x.astype(_BF16)

        def expert(w13, w2):
            uv = _dot(xb, w13[...])
            dff = uv.shape[1] // 2
            u = uv[:, 0:dff]
            hh = u * _sigmoid(u) * uv[:, dff:2 * dff]
            return _dot(hh.astype(_BF16), w2[...])

        y = ga * expert(w13a, w2a) + gb * expert(w13b, w2b)
        z = _layer_norm(ALPHA * x + y, g_ref[...], b_ref[...])

        @pl.when(nv_prev > 0)
        def _drain_prev():
            scatter_wait(nv_prev)

        obuf[...] = z.reshape(tm // _SUBLANES, _SUBLANES, d)
        for c in range(tm // _DMA_CHUNK):
            @pl.when(nv >= (c + 1) * _DMA_CHUNK)
            def _chunk():
                for r in range(c * _DMA_CHUNK, (c + 1) * _DMA_CHUNK):
                    idx = src_ref[0, 0, r]
                    pltpu.make_async_copy(obuf.at[r // _SUBLANES, r % _SUBLANES], hbm_row(out_hbm, idx),
                                          sem_out).start()

        def group(gi, c):
            for u in range(_SUBLANES):
                idx = src_ref[0, 0, gi * _SUBLANES + u]
                pltpu.make_async_copy(obuf.at[gi, u], hbm_row(out_hbm, idx), sem_out).start()
            return c

        def single(r, c):
            idx = src_ref[0, 0, r]
            pltpu.make_async_copy(hbm_row(obuf, r), hbm_row(out_hbm, idx), sem_out).start()
            return c

        ngroups = lax.shift_right_logical(nv, 3)
        lax.fori_loop((nv // _DMA_CHUNK) * (_DMA_CHUNK // _SUBLANES), ngroups, group, 0)
        lax.fori_loop(ngroups * _SUBLANES, nv, single, 0)

        @pl.when(nv_next == 0)
        def _drain_last():
            scatter_wait(nv)


def _moe(he, ea, eb, nvalid, src, w13, w2, g, b, n_tokens, d):
    ntiles = ea.shape[0]
    tm = MOE_TM
    dff = w2.shape[1]
    wspec_a = lambda shape: pl.BlockSpec((None,) + shape, lambda i, ea_r, eb_r, nv_r: (ea_r[i], 0, 0))
    wspec_b = lambda shape: pl.BlockSpec((None,) + shape, lambda i, ea_r, eb_r, nv_r: (eb_r[i], 0, 0))
    grid_spec = pltpu.PrefetchScalarGridSpec(
        num_scalar_prefetch=3,
        grid=(ntiles,),
        in_specs=[
            pl.BlockSpec((1, 1, tm), lambda i, *_: (i, 0, 0), memory_space=pltpu.SMEM),
            pl.BlockSpec((1, 1, tm), lambda i, *_: (jnp.minimum(i + 1, ntiles - 1), 0, 0), memory_space=pltpu.SMEM),
            pl.BlockSpec(memory_space=pl.ANY),
            wspec_a((d, 2 * dff)), wspec_a((dff, d)),
            wspec_b((d, 2 * dff)), wspec_b((dff, d)),
            pl.BlockSpec((1, d), lambda i, *_: (0, 0)),
            pl.BlockSpec((1, d), lambda i, *_: (0, 0)),
        ],
        out_specs=pl.BlockSpec(memory_space=pl.ANY),
        scratch_shapes=[
            pltpu.VMEM((2, tm // _SUBLANES, _SUBLANES, d + GATE_LANES), _F32),
            pltpu.VMEM((tm // _SUBLANES, _SUBLANES, d), _F32),
            pltpu.SemaphoreType.DMA((2,)),
            pltpu.SemaphoreType.DMA,
        ],
    )
    he3 = he.reshape(n_tokens // _SUBLANES, _SUBLANES, d + GATE_LANES)
    out = pl.pallas_call(
        _moe_kernel,
        out_shape=jax.ShapeDtypeStruct((n_tokens // _SUBLANES, _SUBLANES, d), _F32),
        grid_spec=grid_spec,
        compiler_params=pltpu.CompilerParams(dimension_semantics=("arbitrary",)),
        name="moe",
    )(ea, eb, nvalid, src, src, he3, w13, w2, w13, w2, g, b)
    return out.reshape(n_tokens, d)


def _moe_plan(cls, n_tokens):
    tm = MOE_TM
    ntiles = n_tokens // tm + N_CLASSES
    cls = cls.reshape(n_tokens)
    order = jnp.argsort(cls, stable=True).astype(jnp.int32)
    counts = jnp.sum((cls[:, None] == jnp.arange(N_CLASSES, dtype=jnp.int32)[None, :]).astype(jnp.int32), axis=0)
    cstart = jnp.cumsum(counts) - counts
    ptiles = (counts + tm - 1) // tm
    tend = jnp.cumsum(ptiles)
    tstart = tend - ptiles
    tile = jnp.arange(ntiles, dtype=jnp.int32)
    tcls = jnp.minimum(jnp.sum((tile[:, None] >= tend[None, :]).astype(jnp.int32), axis=1), N_CLASSES - 1)
    used = tile < tend[-1]
    nvalid = jnp.where(used, jnp.clip(counts[tcls] - (tile - tstart[tcls]) * tm, 0, tm), 0).astype(jnp.int32)
    r = jnp.arange(tm, dtype=jnp.int32)[None, :]
    valid = r < nvalid[:, None]
    pos = jnp.clip((cstart[tcls] + (tile - tstart[tcls]) * tm)[:, None] + r, 0, n_tokens - 1)
    src = jnp.where(valid, order[pos], 0).astype(jnp.int32).reshape(ntiles, 1, tm)
    pair_lo = jnp.array([0, 0, 0, 1, 1, 2], jnp.int32)
    pair_hi = jnp.array([1, 2, 3, 2, 3, 3], jnp.int32)
    ea = (4 * (tcls // 6) + pair_lo[tcls % 6]).astype(jnp.int32)
    eb = (4 * (tcls // 6) + pair_hi[tcls % 6]).astype(jnp.int32)
    return ea, eb, nvalid, src


def _lru_inproj_kernel(x_ref, w_ref, gate_ref, xr_ref):
    u = _dot(x_ref[...].astype(_BF16), w_ref[...])
    half = gate_ref.shape[1]
    gate_ref[...] = u[:, 0:half]
    xr_ref[...] = u[:, half:2 * half]


def _lru_inproj(h2d, w, tm):
    n, d = h2d.shape
    half = w.shape[1] // 2
    return pl.pallas_call(
        _lru_inproj_kernel,
        out_shape=(jax.ShapeDtypeStruct((n, half), _F32), jax.ShapeDtypeStruct((n, half), _F32)),
        grid=(n // tm,),
        in_specs=[pl.BlockSpec((tm, d), lambda i: (i, 0)), pl.BlockSpec(w.shape, lambda i: (0, 0))],
        out_specs=(pl.BlockSpec((tm, half), lambda i: (i, 0)), pl.BlockSpec((tm, half), lambda i: (i, 0))),
        name="lru_inproj",
    )(h2d, w)


def _scan_slabs(a_refs, b_refs, c_refs, h_in, reverse):
    nlev = len(a_refs)
    ks = list(range(_SCAN_RADIX))
    if reverse:
        ks.reverse()
    for l in range(nlev - 1):
        grp = a_refs[l].shape[1] // _SCAN_RADIX
        p = q = None
        for k in ks:
            sl = pl.ds(k, grp, stride=_SCAN_RADIX)
            a = a_refs[l][:, sl, :]
            b = b_refs[l][:, sl, :]
            if p is None:
                p, q = a, b
            else:
                p, q = a * p, a * q + b
                a_refs[l][:, sl, :] = p
                b_refs[l][:, sl, :] = q
        a_refs[l + 1][...] = p
        b_refs[l + 1][...] = q
    rows = a_refs[-1].shape[1]
    a = a_refs[-1][...]
    b = b_refs[-1][...]
    out = [None] * rows
    h = h_in
    for r in (range(rows - 1, -1, -1) if reverse else range(rows)):
        h = a[:, r:r + 1, :] * h + b[:, r:r + 1, :]
        out[r] = h
    h_out = h
    b_refs[-1][...] = jnp.concatenate(out, axis=1)
    edge = jnp.broadcast_to(h_in, (h_in.shape[0], _SUBLANES, h_in.shape[2]))
    for l in range(nlev - 2, -1, -1):
        grp = a_refs[l].shape[1] // _SCAN_RADIX
        c = c_refs[l]
        c[:, _SUBLANES:_SUBLANES + grp, :] = b_refs[l + 1][...]
        if reverse:
            c[:, _SUBLANES + grp:2 * _SUBLANES + grp, :] = edge
            cin = c[:, _SUBLANES + 1:_SUBLANES + 1 + grp, :]
        else:
            c[:, 0:_SUBLANES, :] = edge
            cin = c[:, _SUBLANES - 1:_SUBLANES - 1 + grp, :]
        for k in ks:
            sl = pl.ds(k, grp, stride=_SCAN_RADIX)
            b_refs[l][:, sl, :] = a_refs[l][:, sl, :] * cin + b_refs[l][:, sl, :]
    return h_out


def _rglru_kernel(*refs, reverse, final, ntile):
    if final:
        (xr_ref, prev_ref, next_ref, cw_ref, cb_ref, wa_ref, ba_ref, wx_ref, bx_ref, lam_ref,
         hf_ref, gate_ref, res_ref, wout_ref, g_ref, b_ref, rwt_ref, rb_ref,
         h_ref, cls_ref, xe_sc, carry_sc, *scan_sc) = refs
    else:
        (xr_ref, prev_ref, next_ref, cw_ref, cb_ref, wa_ref, ba_ref, wx_ref, bx_ref, lam_ref,
         hout_ref, xe_sc, carry_sc, *scan_sc) = refs
    nlev = (len(scan_sc) + 1) // 3
    a_refs, b_refs, c_refs = scan_sc[:nlev], scan_sc[nlev:2 * nlev], scan_sc[2 * nlev:]
    i = pl.program_id(1)
    ti = (ntile - 1 - i) if reverse else i
    tt = xr_ref.shape[0]

    @pl.when(i == 0)
    def _reset():
        carry_sc[...] = jnp.zeros_like(carry_sc)

    prev = jnp.where(ti > 0, prev_ref[...], 0.0)
    nxt = jnp.where(ti < ntile - 1, next_ref[...], 0.0)
    for n in range(LRU_BLOCKS):
        sl = slice(n * LRU_BW, (n + 1) * LRU_BW)
        xe_sc[n, 0:_SUBLANES, :] = prev[:, sl]
        xe_sc[n, _SUBLANES:_SUBLANES + tt, :] = xr_ref[:, sl]
        xe_sc[n, _SUBLANES + tt:2 * _SUBLANES + tt, :] = nxt[:, sl]
    cw = cw_ref[...]
    cb = cb_ref[...]
    half_decay = (0.5 * LRU_C) * -jnp.log(1.0 + jnp.exp(-lam_ref[...]))
    for n in range(LRU_BLOCKS):
        sl = slice(n * LRU_BW, (n + 1) * LRU_BW)
        xcn = cb[:, sl]
        for k in range(cw.shape[0]):
            xcn = xcn + cw[k:k + 1, sl] * xe_sc[n, _SUBLANES - CONV_LEFT + k:_SUBLANES - CONV_LEFT + k + tt, :]
        xb = xcn.astype(_BF16)
        tr = jnp.tanh(_dot(xb, wa_ref[n]) + ba_ref[:, sl])
        ig = 0.5 * jnp.tanh(_dot(xb, wx_ref[n]) + bx_ref[:, sl]) + 0.5
        log_a = tr * half_decay[:, sl] + half_decay[:, sl]
        th = jnp.tanh(log_a)
        num = -2.0 * th
        scale = jnp.where(num > 0.0, num * lax.rsqrt(num * (1.0 - th)), 0.0)
        a_refs[0][n] = jnp.exp(log_a)
        b_refs[0][n] = scale * (ig * xcn)
    carry_sc[...] = _scan_slabs(a_refs, b_refs, c_refs, carry_sc[...], reverse)
    ys = []
    for n in range(LRU_BLOCKS):
        sl = slice(n * LRU_BW, (n + 1) * LRU_BW)
        h = b_refs[0][n]
        if final:
            ys.append((hf_ref[:, sl] + h) * jax.nn.gelu(gate_ref[:, sl], approximate=True))
        else:
            hout_ref[:, sl] = h
    if final:
        y = jnp.concatenate(ys, axis=1).astype(_BF16)
        z = ALPHA * res_ref[...] + _dot(y, wout_ref[...])
        _ln_router_store(z, g_ref, b_ref, rwt_ref, rb_ref, h_ref, cls_ref)


def _rglru(xr, cw, cb, wa, ba, wx, bx, lam, bsz, seq, tt, reverse, tail=None):
    ntile = seq // tt
    width = xr.shape[1]
    final = tail is not None
    tidx = (lambda i: ntile - 1 - i) if reverse else (lambda i: i)
    row = lambda b, i: (b * ntile + tidx(i), 0)
    g8 = tt // 8
    nb8 = seq // 8

    def prev_map(b, i):
        return (b * nb8 + jnp.maximum(tidx(i) * g8 - 1, 0), 0)

    def next_map(b, i):
        return (b * nb8 + jnp.minimum((tidx(i) + 1) * g8, nb8 - 1), 0)

    const = lambda shape: pl.BlockSpec(shape, lambda b, i: (0,) * len(shape))
    in_specs = [
        pl.BlockSpec((tt, width), row),
        pl.BlockSpec((8, width), prev_map),
        pl.BlockSpec((8, width), next_map),
        const(cw.shape), const(cb.shape), const(wa.shape), const(ba.shape), const(wx.shape), const(bx.shape),
        const(lam.shape),
    ]
    args = [xr, xr, xr, cw, cb, wa, ba, wx, bx, lam]
    rows = [tt]
    while rows[-1] > _SUBLANES:
        assert rows[-1] % _SCAN_RADIX == 0
        rows.append(rows[-1] // _SCAN_RADIX)
    assert rows[-1] == _SUBLANES
    slab = lambda r: pltpu.VMEM((LRU_BLOCKS, r, LRU_BW), _F32)
    scratch = [slab(tt + 2 * _SUBLANES), pltpu.VMEM((LRU_BLOCKS, 1, LRU_BW), _F32)]
    scratch += [slab(r) for r in rows] * 2 + [slab(r + 2 * _SUBLANES) for r in rows[1:]]
    if final:
        hf, gate, res, wout, g, b, rwt, rb = tail
        d = res.shape[1]
        in_specs += [pl.BlockSpec((tt, width), row), pl.BlockSpec((tt, width), row), pl.BlockSpec((tt, d), row),
                     const(wout.shape), const(g.shape), const(b.shape), const(rwt.shape), const(rb.shape)]
        args += [hf, gate, res, wout, g, b, rwt, rb]
        out_shape = (jax.ShapeDtypeStruct((bsz * seq, d + GATE_LANES), _F32),
                     jax.ShapeDtypeStruct((bsz * ntile, 1, tt), jnp.int32))
        out_specs = (pl.BlockSpec((tt, d + GATE_LANES), row),
                     pl.BlockSpec((1, 1, tt), lambda b, i: (b * ntile + tidx(i), 0, 0)))
    else:
        out_shape = jax.ShapeDtypeStruct((bsz * seq, width), _F32)
        out_specs = pl.BlockSpec((tt, width), row)
    return pl.pallas_call(
        functools.partial(_rglru_kernel, reverse=reverse, final=final, ntile=ntile),
        out_shape=out_shape,
        grid=(bsz, ntile),
        in_specs=in_specs,
        out_specs=out_specs,
        scratch_shapes=scratch,
        compiler_params=pltpu.CompilerParams(dimension_semantics=("arbitrary", "arbitrary")),
        name="rglru_bwd_tail" if final else "rglru_fwd",
    )(*args)


def _rotary_tables(seq):
    half = MLA_ROPE // 2
    inv_freq = ROPE_THETA ** (-jnp.arange(half, dtype=_F32) / half)
    ang = jnp.arange(seq, dtype=_F32)[:, None] * inv_freq[None, :]
    return jnp.cos(ang), jnp.sin(ang)


def _t5_bucket(rel):
    n_side = REL_BUCKETS // 2
    max_exact = n_side // 2
    dist = jnp.abs(rel)
    far = max_exact + (jnp.log(jnp.maximum(dist, 1).astype(_F32) / max_exact)
                       / math.log(REL_MAX_DIST / max_exact) * (n_side - max_exact)).astype(jnp.int32)
    far = jnp.minimum(far, n_side - 1)
    return jnp.where(rel > 0, n_side, 0) + jnp.where(dist < max_exact, dist, far)


def _band_buckets():
    rel = jnp.arange(3 * BLOCK)[None, :] - BLOCK - jnp.arange(BLOCK)[:, None]
    ids = jnp.where(jnp.abs(rel) <= WINDOW, _t5_bucket(rel), REL_BUCKETS).astype(jnp.int32)
    return jnp.bitwise_and(ids, 2 * REL_BUCKETS - 1)


def _pick_tile(seq, want):
    t = min(want, seq)
    while seq % t:
        t //= 2
    return t


def kernel(x, rel_bias, router_w, router_bias, l0_w_in, l0_q_norm, l0_w_uq, l0_kv_norm, l0_w_ukv, l0_sinks, l0_w_out, l0_ln1_g, l0_ln1_b, l0_w1, l0_w3, l0_w2, l0_ln2_g, l0_ln2_b, l1_w_in, l1_conv_w, l1_conv_b, l1_wa_f, l1_ba_f, l1_wx_f, l1_bx_f, l1_lam_f, l1_wa_b, l1_ba_b, l1_wx_b, l1_bx_b, l1_lam_b, l1_w_out, l1_ln1_g, l1_ln1_b, l1_w1, l1_w3, l1_w2, l1_ln2_g, l1_ln2_b):
    bsz, seq, d = x.shape
    n_tok = bsz * seq
    assert seq % BLOCK == 0 and d == LRU_BLOCKS * LRU_BW and n_tok % MOE_TM == 0
    x2d = x.reshape(n_tok, d)
    row = lambda v: v.reshape(1, -1).astype(_F32)
    rwt = router_w.astype(_F32).T
    rb = router_bias.astype(_F32).reshape(N_EXPERTS, 1)

    o = np.cumsum([0, MLA_Q_RANK, MLA_KV_RANK, MLA_ROPE, SWA_HEADS * SWA_D, SWA_KV_HEADS * SWA_D,
                   SWA_KV_HEADS * SWA_D])
    w_q, w_kv, w_kr, w_qb, w_kb, w_vb = [l0_w_in[:, o[i]:o[i + 1]] for i in range(6)]
    w_in0 = jnp.concatenate([w_q, w_kv, jnp.pad(w_kr, ((0, 0), (0, LANE - MLA_ROPE))), w_kb], axis=1).astype(_BF16)
    wuq = l0_w_uq.reshape(MLA_Q_RANK, MLA_HEADS, MLA_QK)
    wuqt = jnp.pad(wuq, ((0, 0), (0, 0), (0, HEAD_PAD - MLA_QK))).reshape(MLA_Q_RANK, MLA_HEADS * HEAD_PAD).T
    wukv = l0_w_ukv.reshape(MLA_KV_RANK, MLA_HEADS, MLA_NOPE + MLA_V)
    wuk = jnp.pad(wukv[:, :, :MLA_NOPE], ((0, 0), (0, 0), (0, HEAD_PAD - MLA_NOPE))).reshape(MLA_KV_RANK, -1)
    wuvt = wukv[:, :, MLA_NOPE:].reshape(MLA_KV_RANK, MLA_HEADS * MLA_V).T
    cos, sin = _rotary_tables(seq)
    ccs = jnp.pad(jnp.concatenate([cos, cos], axis=1), ((0, 0), (0, LANE - MLA_ROPE)))
    sn = jnp.pad(jnp.concatenate([-sin, sin], axis=1), ((0, 0), (0, LANE - MLA_ROPE)))

    tm = _pick_tile(seq, 512)
    qt, k, vt, qbt, kb, vbt = _attn_inproj(
        x2d, w_in0, w_qb.T.astype(_BF16), w_vb.T.astype(_BF16), row(l0_q_norm), wuqt.astype(_BF16),
        row(l0_kv_norm), wuk.astype(_BF16), wuvt.astype(_BF16), cos.T, sin.T, ccs, sn, bsz, seq, tm)
    nstream = 8 if seq % 4096 == 0 else (2 if seq % 1024 == 0 else 1)
    ot = _mla_flash(qt, k, vt, _pick_tile(seq // nstream, 512), _pick_tile(seq // 2, 512), nstream)
    kpad = jnp.pad(kb, ((0, 0), (BLOCK, BLOCK), (0, 0)))
    vtpad = jnp.pad(vbt, ((0, 0), (0, 0), (BLOCK, BLOCK)))
    obt = _swa(qbt, kpad, vtpad, _band_buckets().T, rel_bias.astype(_F32), l0_sinks.astype(_F32), bsz, seq,
               _pick_tile(seq // BLOCK, 8))
    he, cls = _outproj_ln_router(ot, obt, x2d, l0_w_out.astype(_BF16), row(l0_ln1_g), row(l0_ln1_b), rwt, rb,
                                 bsz, seq, tm)
    ea, eb, nvalid, src = _moe_plan(cls, n_tok)
    w13 = lambda w1, w3: jnp.concatenate([w1.astype(_BF16), w3.astype(_BF16)], axis=2)
    h = _moe(he, ea, eb, nvalid, src, w13(l0_w1, l0_w3), l0_w2.astype(_BF16),
             row(l0_ln2_g), row(l0_ln2_b), n_tok, d)

    gate, xr = _lru_inproj(h, l1_w_in.astype(_BF16), tm)
    tt = _pick_tile(seq, 512)
    cw = l1_conv_w.astype(_F32)
    cb = row(l1_conv_b)
    halfw = lambda w: (0.5 * w).astype(_BF16)
    halfb = lambda v: 0.5 * row(v)
    hf = _rglru(xr, cw, cb, halfw(l1_wa_f), halfb(l1_ba_f), halfw(l1_wx_f), halfb(l1_bx_f),
                row(l1_lam_f), bsz, seq, tt, reverse=False)
    he, cls = _rglru(xr, cw, cb, halfw(l1_wa_b), halfb(l1_ba_b), halfw(l1_wx_b), halfb(l1_bx_b),
                     row(l1_lam_b), bsz, seq, tt, reverse=True,
                     tail=(hf, gate, h, l1_w_out.astype(_BF16), row(l1_ln1_g), row(l1_ln1_b), rwt, rb))
    ea, eb, nvalid, src = _moe_plan(cls, n_tok)
    h = _moe(he, ea, eb, nvalid, src, w13(l1_w1, l1_w3), l1_w2.astype(_BF16),
             row(l1_ln2_g), row(l1_ln2_b), n_tok, d)
    return h.reshape(bsz, seq, d)
```

```python
import functools
import math

import jax
import jax.numpy as jnp
import numpy as np
from jax import lax
from jax.experimental import pallas as pl
from jax.experimental.pallas import tpu as pltpu

MLA_HEADS = 8
MLA_Q_RANK = 256
MLA_KV_RANK = 128
MLA_NOPE = 64
MLA_ROPE = 32
MLA_V = 64
MLA_QK = MLA_NOPE + MLA_ROPE
ROPE_THETA = 10000.0
SWA_HEADS = 8
SWA_KV_HEADS = 2
SWA_REP = SWA_HEADS // SWA_KV_HEADS
SWA_D = 64
WINDOW = 128
BLOCK = 128
REL_BUCKETS = 32
REL_MAX_DIST = 128
LRU_BLOCKS = 8
LRU_BW = 128
LRU_C = 8.0
CONV_LEFT = 2
N_EXPERTS = 16
N_GROUPS = 4
EXPERTS_PER_GROUP = 4
N_CLASSES = N_GROUPS * 6
DEPTH = 2
ALPHA = (2.0 * DEPTH) ** 0.25
LN_EPS = 1e-5
RMS_EPS = 1e-6
NEG_BIG = -1e30

LANE = 128
HEAD_PAD = 128
GATE_LANES = 128
VT_ROWS = 80
_LOG2E = math.log2(math.e)
MOE_TM = 256

_F32 = jnp.float32
_BF16 = jnp.bfloat16
_NT_DIMS = (((1,), (1,)), ((), ()))
_TN_DIMS = (((0,), (0,)), ((), ()))


def _dot(a, b):
    return jnp.dot(a, b, preferred_element_type=_F32)


def _dot_nt(a, b):
    return lax.dot_general(a, b, _NT_DIMS, preferred_element_type=_F32)


def _sigmoid(x):
    return 0.5 * jnp.tanh(0.5 * x) + 0.5


def _rms(x, g):
    return x * lax.rsqrt(jnp.mean(jnp.square(x), -1, keepdims=True) + RMS_EPS) * g


def _layer_norm(x, g, b):
    mu = jnp.mean(x, -1, keepdims=True)
    xc = x - mu
    var = jnp.mean(jnp.square(xc), -1, keepdims=True)
    return xc * lax.rsqrt(var + LN_EPS) * g + b


def _attn_inproj_kernel(x_ref, w_ref, wqbt_ref, wvbt_ref, qn_ref, wuqt_ref, kvn_ref, wuk_ref, wuvt_ref,
                        cost_ref, sint_ref, ccs_ref, sn_ref,
                        qt_ref, k_ref, vt_ref, qbt_ref, kb_ref, vbt_ref):
    xb = x_ref[...].astype(_BF16)
    proj = _dot(xb, w_ref[...])
    q_lat = proj[:, 0:256]
    kv_lat = proj[:, 256:384]
    kr = proj[:, 384:512]
    qn = _rms(q_lat, qn_ref[...]).astype(_BF16)
    qt = _dot_nt(wuqt_ref[...], qn) * (MLA_QK ** -0.5 * _LOG2E)
    cos_t = cost_ref[...]
    sin_t = sint_ref[...]
    tm = qt.shape[1]
    zpad = jnp.zeros((HEAD_PAD - MLA_QK, tm), _F32)
    for h in range(MLA_HEADS):
        r0 = h * HEAD_PAD
        x1 = qt[r0 + 64:r0 + 80, :]
        x2 = qt[r0 + 80:r0 + 96, :]
        blk = jnp.concatenate([qt[r0:r0 + 64, :], x1 * cos_t - x2 * sin_t, x1 * sin_t + x2 * cos_t, zpad], axis=0)
        qt_ref[0, r0:r0 + HEAD_PAD, :] = blk.astype(_BF16)
    kvn = _rms(kv_lat, kvn_ref[...]).astype(_BF16)
    kn = _dot(kvn, wuk_ref[...])
    lane = lax.broadcasted_iota(jnp.int32, kr.shape, 1)
    swapped = jnp.where(lane < 16, pltpu.roll(kr, 112, 1), pltpu.roll(kr, 16, 1))
    kpe = kr * ccs_ref[...] + swapped * sn_ref[...]
    kpe = pltpu.roll(kpe, 64, 1)
    k_ref[0] = (kn + jnp.concatenate([kpe] * MLA_HEADS, axis=1)).astype(_BF16)
    vt = _dot_nt(wuvt_ref[...], kvn).astype(_BF16)
    extra = (lax.broadcasted_iota(jnp.int32, (VT_ROWS - MLA_V, tm), 0) == 0).astype(_BF16)
    for h in range(MLA_HEADS):
        vt_ref[0, h * VT_ROWS:h * VT_ROWS + MLA_V, :] = vt[h * MLA_V:(h + 1) * MLA_V, :]
        vt_ref[0, h * VT_ROWS + MLA_V:(h + 1) * VT_ROWS, :] = extra
    kb_ref[0] = proj[:, 512:640].astype(_BF16)
    vbt_ref[0] = _dot_nt(wvbt_ref[...], xb).astype(_BF16)
    qbt = (_dot_nt(wqbt_ref[...], xb) * (SWA_D ** -0.5)).astype(_BF16)
    zhalf = jnp.zeros((SWA_D, tm), _BF16)
    for h in range(SWA_HEADS):
        g = h // SWA_REP
        real = qbt[h * SWA_D:(h + 1) * SWA_D, :]
        qbt_ref[0, h * LANE + g * SWA_D:h * LANE + (g + 1) * SWA_D, :] = real
        qbt_ref[0, h * LANE + (1 - g) * SWA_D:h * LANE + (2 - g) * SWA_D, :] = zhalf


def _attn_inproj(x2d, w, wqbt, wvbt, qn, wuqt, kvn, wuk, wuvt, cos_t, sin_t, ccs, sn, bsz, seq, tm):
    nst = seq // tm
    const = lambda shape: pl.BlockSpec(shape, lambda b, i: (0,) * len(shape))
    out_shape = (
        jax.ShapeDtypeStruct((bsz, MLA_HEADS * HEAD_PAD, seq), _BF16),
        jax.ShapeDtypeStruct((bsz, seq, MLA_HEADS * HEAD_PAD), _BF16),
        jax.ShapeDtypeStruct((bsz, MLA_HEADS * VT_ROWS, seq), _BF16),
        jax.ShapeDtypeStruct((bsz, SWA_HEADS * LANE, seq), _BF16),
        jax.ShapeDtypeStruct((bsz, seq, SWA_KV_HEADS * SWA_D), _BF16),
        jax.ShapeDtypeStruct((bsz, SWA_KV_HEADS * SWA_D, seq), _BF16),
    )
    return pl.pallas_call(
        _attn_inproj_kernel,
        out_shape=out_shape,
        grid=(bsz, nst),
        in_specs=[
            pl.BlockSpec((tm, x2d.shape[1]), lambda b, i: (b * nst + i, 0)),
            const(w.shape), const(wqbt.shape), const(wvbt.shape), const(qn.shape), const(wuqt.shape),
            const(kvn.shape), const(wuk.shape), const(wuvt.shape),
            pl.BlockSpec((16, tm), lambda b, i: (0, i)),
            pl.BlockSpec((16, tm), lambda b, i: (0, i)),
            pl.BlockSpec((tm, LANE), lambda b, i: (i, 0)),
            pl.BlockSpec((tm, LANE), lambda b, i: (i, 0)),
        ],
        out_specs=(
            pl.BlockSpec((1, MLA_HEADS * HEAD_PAD, tm), lambda b, i: (b, 0, i)),
            pl.BlockSpec((1, tm, MLA_HEADS * HEAD_PAD), lambda b, i: (b, i, 0)),
            pl.BlockSpec((1, MLA_HEADS * VT_ROWS, tm), lambda b, i: (b, 0, i)),
            pl.BlockSpec((1, SWA_HEADS * LANE, tm), lambda b, i: (b, 0, i)),
            pl.BlockSpec((1, tm, SWA_KV_HEADS * SWA_D), lambda b, i: (b, i, 0)),
            pl.BlockSpec((1, SWA_KV_HEADS * SWA_D, tm), lambda b, i: (b, 0, i)),
        ),
        name="attn_inproj",
    )(x2d, w, wqbt, wvbt, qn, wuqt, kvn, wuk, wuvt, cos_t, sin_t, ccs, sn)


def _mla_flash_kernel(qt_ref, k_ref, vt_ref, ot_ref, *scratch, tk, nstream):
    st_sc = scratch[:2 * nstream]
    p_sc = scratch[2 * nstream:4 * nstream]
    tq = qt_ref.shape[2] // nstream
    nkv = k_ref.shape[1] // tk
    assert nkv % 2 == 0
    qts = [qt_ref[0, :, s * tq:(s + 1) * tq] for s in range(nstream)]

    def keys(j):
        return k_ref[0, pl.ds(pl.multiple_of(j * tk, tk), tk), :]

    def values(j):
        return vt_ref[0, :, pl.ds(pl.multiple_of(j * tk, tk), tk)]

    def phase(j, cur, state):
        nxt = 1 - cur
        kt = keys(jnp.minimum(j + 1, nkv - 1))
        vt = values(jnp.maximum(j - 1, 0))
        out = []
        for s in range(nstream):
            m, acc = state[s]
            st_sc[2 * s + nxt][...] = _dot(kt, qts[s])
            pv = _dot(vt, p_sc[2 * s + nxt][...])
            st = st_sc[2 * s + cur][...]
            m_new = jnp.maximum(m, jnp.max(st, axis=0, keepdims=True))
            alpha = jnp.exp2(m - m_new)
            p_sc[2 * s + cur][...] = jnp.exp2(st - m_new).astype(_BF16)
            out.append((m_new, (acc + pv) * alpha))
        return out

    def body(jj, state):
        state = phase(2 * jj, 0, state)
        return phase(2 * jj + 1, 1, state)

    k0 = keys(0)
    state = []
    for s in range(nstream):
        st_sc[2 * s][...] = _dot(k0, qts[s])
        p_sc[2 * s + 1][...] = jnp.zeros((tk, tq), _BF16)
        state.append((jnp.full((1, tq), -jnp.inf, _F32), jnp.zeros((VT_ROWS, tq), _F32)))
    state = lax.fori_loop(0, nkv // 2, body, state)
    v_last = values(nkv - 1)
    for s in range(nstream):
        acc = state[s][1] + _dot(v_last, p_sc[2 * s + 1][...])
        ot_ref[0, :, s * tq:(s + 1) * tq] = (acc[0:MLA_V, :] / acc[MLA_V:MLA_V + 1, :]).astype(ot_ref.dtype)


def _mla_flash(qt, k, vt, tq, tk, nstream):
    bsz, _, seq = qt.shape
    tqs = tq * nstream
    scratch = [pltpu.VMEM((tk, tq), _F32)] * (2 * nstream) + [pltpu.VMEM((tk, tq), _BF16)] * (2 * nstream)
    return pl.pallas_call(
        functools.partial(_mla_flash_kernel, tk=tk, nstream=nstream),
        out_shape=jax.ShapeDtypeStruct((bsz, MLA_HEADS * MLA_V, seq), _BF16),
        grid=(bsz, MLA_HEADS, seq // tqs),
        in_specs=[
            pl.BlockSpec((1, HEAD_PAD, tqs), lambda b, h, i: (b, h, i)),
            pl.BlockSpec((1, seq, HEAD_PAD), lambda b, h, i: (b, 0, h)),
            pl.BlockSpec((1, VT_ROWS, seq), lambda b, h, i: (b, h, 0)),
        ],
        out_specs=pl.BlockSpec((1, MLA_V, tqs), lambda b, h, i: (b, h, i)),
        scratch_shapes=scratch,
        name="mla_flash",
    )(qt, k, vt)


def _swa_kernel(relb_ref, sinks_ref, qt_ref, k_ref, vt_ref, bucket_ref, o_ref, bias_sc, *, nblk, seq):
    first = jnp.logical_and(pl.program_id(0) == 0, pl.program_id(1) == 0)

    @pl.when(first)
    def _build_bias():
        bucket = bucket_ref[...]
        for h in range(SWA_HEADS):
            acc = jnp.full(bucket.shape, NEG_BIG, _F32)
            for bk in range(REL_BUCKETS):
                acc = jnp.where(bucket == bk, relb_ref[bk, h], acc)
            bias_sc[h] = acc

    j = pl.program_id(1)
    krow = lax.broadcasted_iota(jnp.int32, (3 * BLOCK, 1), 0)

    def block(u, c):
        n = j * nblk + u
        c0 = pl.multiple_of(u * BLOCK, BLOCK)
        w0 = pl.multiple_of(n * BLOCK, BLOCK)
        key_pos = n * BLOCK - BLOCK + krow
        emask = jnp.where(jnp.logical_and(key_pos >= 0, key_pos < seq), 0.0, NEG_BIG).astype(_F32)
        kw = k_ref[0, pl.ds(w0, 3 * BLOCK), :]
        vw = vt_ref[0, :, pl.ds(w0, 3 * BLOCK)]
        for g in range(SWA_KV_HEADS):
            heads = range(g * SWA_REP, (g + 1) * SWA_REP)
            qs = jnp.concatenate([qt_ref[0, h * LANE:(h + 1) * LANE, pl.ds(c0, BLOCK)] for h in heads], axis=1)
            bias = jnp.concatenate([bias_sc[h] for h in heads], axis=1)
            sink = jnp.concatenate([jnp.full((1, BLOCK), sinks_ref[h], _F32) for h in heads], axis=1)
            st = _dot(kw, qs) + bias + emask
            m = jnp.maximum(jnp.max(st, axis=0, keepdims=True), sink)
            p = jnp.exp(st - m)
            den = jnp.sum(p, axis=0, keepdims=True) + jnp.exp(sink - m)
            ot = _dot(vw[g * SWA_D:(g + 1) * SWA_D, :], p.astype(_BF16)) / den
            for r, h in enumerate(heads):
                o_ref[0, h * SWA_D:(h + 1) * SWA_D, pl.ds(c0, BLOCK)] = ot[:, r * BLOCK:(r + 1) * BLOCK].astype(o_ref.dtype)
        return c

    def pair(t, c):
        block(2 * t, c)
        return block(2 * t + 1, c)

    if nblk % 2 == 0:
        lax.fori_loop(0, nblk // 2, pair, 0)
    else:
        lax.fori_loop(0, nblk, block, 0)


def _swa(qbt, kpad, vtpad, bucket_t, rel_bias, sinks, bsz, seq, nblk):
    nsteps = seq // (nblk * BLOCK)
    cols = nblk * BLOCK
    return pl.pallas_call(
        functools.partial(_swa_kernel, nblk=nblk, seq=seq),
        out_shape=jax.ShapeDtypeStruct((bsz, SWA_HEADS * SWA_D, seq), _BF16),
        grid=(bsz, nsteps),
        in_specs=[
            pl.BlockSpec(memory_space=pltpu.SMEM),
            pl.BlockSpec(memory_space=pltpu.SMEM),
            pl.BlockSpec((1, SWA_HEADS * LANE, cols), lambda b, j: (b, 0, j)),
            pl.BlockSpec((1, seq + 2 * BLOCK, SWA_KV_HEADS * SWA_D), lambda b, j: (b, 0, 0)),
            pl.BlockSpec((1, SWA_KV_HEADS * SWA_D, seq + 2 * BLOCK), lambda b, j: (b, 0, 0)),
            pl.BlockSpec((3 * BLOCK, BLOCK), lambda b, j: (0, 0)),
        ],
        out_specs=pl.BlockSpec((1, SWA_HEADS * SWA_D, cols), lambda b, j: (b, 0, j)),
        scratch_shapes=[pltpu.VMEM((SWA_HEADS, 3 * BLOCK, BLOCK), _F32)],
        compiler_params=pltpu.CompilerParams(dimension_semantics=("arbitrary", "arbitrary")),
        name="swa",
    )(rel_bias, sinks, qbt, kpad, vtpad, bucket_t)


def _route(logits_t, rbias):
    sc = jax.nn.sigmoid(logits_t)
    bz = sc + rbias
    s_rows = [sc[e:e + 1, :] for e in range(N_EXPERTS)]
    b_rows = [bz[e:e + 1, :] for e in range(N_EXPERTS)]
    gsel = None
    best = None
    for g in range(N_GROUPS):
        r = b_rows[4 * g:4 * g + 4]
        gs = r[0] + r[1]
        for (i, k) in ((0, 2), (0, 3), (1, 2), (1, 3), (2, 3)):
            gs = jnp.maximum(gs, r[i] + r[k])
        if g == 0:
            gsel = jnp.zeros(gs.shape, jnp.int32)
            best = gs
        else:
            better = gs > best
            gsel = jnp.where(better, g, gsel)
            best = jnp.where(better, gs, best)

    def pick(rows, k):
        out = rows[12 + k]
        for g in (2, 1, 0):
            out = jnp.where(gsel == g, rows[4 * g + k], out)
        return out

    v = [pick(b_rows, k) for k in range(4)]
    s = [pick(s_rows, k) for k in range(4)]
    i1 = jnp.zeros(gsel.shape, jnp.int32)
    m1 = v[0]
    w1 = s[0]
    for k in range(1, 4):
        gt = v[k] > m1
        i1 = jnp.where(gt, k, i1)
        m1 = jnp.where(gt, v[k], m1)
        w1 = jnp.where(gt, s[k], w1)
    i2 = jnp.full(gsel.shape, -1, jnp.int32)
    m2 = jnp.full(m1.shape, -jnp.inf, _F32)
    w2 = jnp.zeros(m1.shape, _F32)
    for k in range(4):
        ok = jnp.logical_and(i1 != k, jnp.logical_or(i2 < 0, v[k] > m2))
        i2 = jnp.where(ok, k, i2)
        m2 = jnp.where(ok, v[k], m2)
        w2 = jnp.where(ok, s[k], w2)
    tot = w1 + w2
    g1 = w1 / tot
    g2 = w2 / tot
    first_lo = i1 < i2
    lo = jnp.where(first_lo, i1, i2)
    hi = jnp.where(first_lo, i2, i1)
    pair = jnp.where(lo == 0, hi - 1, jnp.where(lo == 1, hi + 1, 5))
    cls = gsel * 6 + pair
    return cls, jnp.where(first_lo, g1, g2), jnp.where(first_lo, g2, g1)


def _ln_router_store(z, g_ref, b_ref, rwt_ref, rb_ref, h_ref, cls_ref):
    h = _layer_norm(z, g_ref[...], b_ref[...])
    tm = h.shape[0]
    h_hi = h.astype(_BF16)
    h_lo = (h - h_hi.astype(_F32)).astype(_BF16)
    rw = rwt_ref[...]
    rw_hi = rw.astype(_BF16)
    rw_lo = (rw - rw_hi.astype(_F32)).astype(_BF16)
    part = _dot_nt(jnp.concatenate([rw_hi, rw_lo], axis=0), h_hi)
    logits_t = part[0:N_EXPERTS] + part[N_EXPERTS:2 * N_EXPERTS] + _dot_nt(rw_hi, h_lo)
    cls, g_lo, g_hi = _route(logits_t, rb_ref[...])
    rows = jnp.concatenate([g_lo, g_hi, jnp.zeros((GATE_LANES - 2, tm), _F32)], axis=0)
    d = h.shape[1]
    h_ref[:, 0:d] = h
    h_ref[:, d:d + GATE_LANES] = rows.T
    cls_ref[0] = cls


def _outproj_ln_router_kernel(ot_ref, ob_ref, x_ref, w_ref, g_ref, b_ref, rwt_ref, rb_ref, h_ref, cls_ref):
    heads_t = jnp.concatenate([ot_ref[0], ob_ref[0]], axis=0)
    mixed = lax.dot_general(heads_t, w_ref[...], _TN_DIMS, preferred_element_type=_F32)
    z = ALPHA * x_ref[...] + mixed
    _ln_router_store(z, g_ref, b_ref, rwt_ref, rb_ref, h_ref, cls_ref)


def _outproj_ln_router(ot, ob, x2d, w, g, b, rwt, rb, bsz, seq, tm):
    nst = seq // tm
    d = x2d.shape[1]
    const = lambda shape: pl.BlockSpec(shape, lambda bb, i: (0,) * len(shape))
    return pl.pallas_call(
        _outproj_ln_router_kernel,
        out_shape=(jax.ShapeDtypeStruct((bsz * seq, d + GATE_LANES), _F32),
                   jax.ShapeDtypeStruct((bsz * nst, 1, tm), jnp.int32)),
        grid=(bsz, nst),
        in_specs=[
            pl.BlockSpec((1, ot.shape[1], tm), lambda bb, i: (bb, 0, i)),
            pl.BlockSpec((1, ob.shape[1], tm), lambda bb, i: (bb, 0, i)),
            pl.BlockSpec((tm, d), lambda bb, i: (bb * nst + i, 0)),
            const(w.shape), const(g.shape), const(b.shape), const(rwt.shape), const(rb.shape),
        ],
        out_specs=(pl.BlockSpec((tm, d + GATE_LANES), lambda bb, i: (bb * nst + i, 0)),
                   pl.BlockSpec((1, 1, tm), lambda bb, i: (bb * nst + i, 0, 0))),
        name="outproj_ln_router",
    )(ot, ob, x2d, w, g, b, rwt, rb)


_SUBLANES = 8
_SCAN_RADIX = 4
_DMA_CHUNK = 32


def _moe_kernel(ea_ref, eb_ref, nv_ref, src_ref, nsrc_ref, h_hbm, w1a, w3a, w2a, w1b, w3b, w2b, g_ref, b_ref,
                out_hbm, xbuf, obuf, w13a, w2a_bf, w13b, w2b_bf, sem_in, sem_out):
    i = pl.program_id(0)
    ntiles = pl.num_programs(0)
    tm = xbuf.shape[1] * _SUBLANES
    d = obuf.shape[2]
    nv = nv_ref[i]
    nv_prev = jnp.where(i > 0, nv_ref[jnp.maximum(i - 1, 0)], 0)
    nv_next = jnp.where(i + 1 < ntiles, nv_ref[jnp.minimum(i + 1, ntiles - 1)], 0)
    slot = lax.rem(i, 2)

    def hbm_row(ref, idx):
        return ref.at[lax.shift_right_logical(idx, 3), jnp.bitwise_and(idx, _SUBLANES - 1)]

    def gather_start(idx_ref, s):
        for r in range(tm):
            idx = idx_ref[0, 0, r]
            pltpu.make_async_copy(hbm_row(h_hbm, idx), xbuf.at[s, r // _SUBLANES, r % _SUBLANES], sem_in.at[s]).start()

    def scatter_wait(n):
        for bit in range(tm.bit_length()):
            rows = 1 << bit

            @pl.when(jnp.bitwise_and(n, rows) != 0)
            def _():
                if rows >= _SUBLANES:
                    grp = pl.ds(0, rows // _SUBLANES)
                    pltpu.make_async_copy(obuf.at[grp], out_hbm.at[grp], sem_out).wait()
                else:
                    pltpu.make_async_copy(obuf.at[0, pl.ds(0, rows)], out_hbm.at[0, pl.ds(0, rows)], sem_out).wait()

    @pl.when(jnp.logical_and(i == 0, nv > 0))
    def _prologue():
        gather_start(src_ref, 0)

    @pl.when(nv_next > 0)
    def _prefetch():
        gather_start(nsrc_ref, 1 - slot)

    prev_i = jnp.maximum(i - 1, 0)
    dff = w2a.shape[0]

    def refresh(w1, w3, w2, w13_sc, w2_sc):
        w13_sc[:, 0:dff] = w1[...].astype(_BF16)
        w13_sc[:, dff:2 * dff] = w3[...].astype(_BF16)
        w2_sc[...] = w2[...].astype(_BF16)

    @pl.when(jnp.logical_and(nv > 0, jnp.logical_or(i == 0, ea_ref[i] != ea_ref[prev_i])))
    def _refresh_a():
        refresh(w1a, w3a, w2a, w13a, w2a_bf)

    @pl.when(jnp.logical_and(nv > 0, jnp.logical_or(i == 0, eb_ref[i] != eb_ref[prev_i])))
    def _refresh_b():
        refresh(w1b, w3b, w2b, w13b, w2b_bf)

    @pl.when(nv > 0)
    def _tile():
        pltpu.make_async_copy(h_hbm.at[pl.ds(0, tm // _SUBLANES)], xbuf.at[slot], sem_in.at[slot]).wait()
        xg = xbuf[slot].reshape(tm, d + GATE_LANES)
        x = xg[:, 0:d]
        gates = xg[:, d:d + GATE_LANES]
        ga = gates[:, 0:1]
        gb = gates[:, 1:2]
        xb = x.astype(_BF16)

        def expert(w13, w2):
            uv = _dot(xb, w13[...])
            dff = uv.shape[1] // 2
            u = uv[:, 0:dff]
            hh = u * _sigmoid(u) * uv[:, dff:2 * dff]
            return _dot(hh.astype(_BF16), w2[...])

        y = ga * expert(w13a, w2a_bf) + gb * expert(w13b, w2b_bf)
        z = _layer_norm(ALPHA * x + y, g_ref[...], b_ref[...])

        @pl.when(nv_prev > 0)
        def _drain_prev():
            scatter_wait(nv_prev)

        obuf[...] = z.reshape(tm // _SUBLANES, _SUBLANES, d)
        for c in range(tm // _DMA_CHUNK):
            @pl.when(nv >= (c + 1) * _DMA_CHUNK)
            def _chunk():
                for r in range(c * _DMA_CHUNK, (c + 1) * _DMA_CHUNK):
                    idx = src_ref[0, 0, r]
                    pltpu.make_async_copy(obuf.at[r // _SUBLANES, r % _SUBLANES], hbm_row(out_hbm, idx),
                                          sem_out).start()

        def group(gi, c):
            for u in range(_SUBLANES):
                idx = src_ref[0, 0, gi * _SUBLANES + u]
                pltpu.make_async_copy(obuf.at[gi, u], hbm_row(out_hbm, idx), sem_out).start()
            return c

        def single(r, c):
            idx = src_ref[0, 0, r]
            pltpu.make_async_copy(hbm_row(obuf, r), hbm_row(out_hbm, idx), sem_out).start()
            return c

        ngroups = lax.shift_right_logical(nv, 3)
        lax.fori_loop((nv // _DMA_CHUNK) * (_DMA_CHUNK // _SUBLANES), ngroups, group, 0)
        lax.fori_loop(ngroups * _SUBLANES, nv, single, 0)

        @pl.when(nv_next == 0)
        def _drain_last():
            scatter_wait(nv)


def _moe(he, ea, eb, nvalid, src, w1, w3, w2, g, b, n_tokens, d):
    ntiles = ea.shape[0]
    tm = MOE_TM
    dff = w2.shape[1]
    wspec_a = lambda shape: pl.BlockSpec((None,) + shape, lambda i, ea_r, eb_r, nv_r: (ea_r[i], 0, 0))
    wspec_b = lambda shape: pl.BlockSpec((None,) + shape, lambda i, ea_r, eb_r, nv_r: (eb_r[i], 0, 0))
    grid_spec = pltpu.PrefetchScalarGridSpec(
        num_scalar_prefetch=3,
        grid=(ntiles,),
        in_specs=[
            pl.BlockSpec((1, 1, tm), lambda i, *_: (i, 0, 0), memory_space=pltpu.SMEM),
            pl.BlockSpec((1, 1, tm), lambda i, *_: (jnp.minimum(i + 1, ntiles - 1), 0, 0), memory_space=pltpu.SMEM),
            pl.BlockSpec(memory_space=pl.ANY),
            wspec_a((d, dff)), wspec_a((d, dff)), wspec_a((dff, d)),
            wspec_b((d, dff)), wspec_b((d, dff)), wspec_b((dff, d)),
            pl.BlockSpec((1, d), lambda i, *_: (0, 0)),
            pl.BlockSpec((1, d), lambda i, *_: (0, 0)),
        ],
        out_specs=pl.BlockSpec(memory_space=pl.ANY),
        scratch_shapes=[
            pltpu.VMEM((2, tm // _SUBLANES, _SUBLANES, d + GATE_LANES), _F32),
            pltpu.VMEM((tm // _SUBLANES, _SUBLANES, d), _F32),
            pltpu.VMEM((d, 2 * dff), _BF16), pltpu.VMEM((dff, d), _BF16),
            pltpu.VMEM((d, 2 * dff), _BF16), pltpu.VMEM((dff, d), _BF16),
            pltpu.SemaphoreType.DMA((2,)),
            pltpu.SemaphoreType.DMA,
        ],
    )
    he3 = he.reshape(n_tokens // _SUBLANES, _SUBLANES, d + GATE_LANES)
    out = pl.pallas_call(
        _moe_kernel,
        out_shape=jax.ShapeDtypeStruct((n_tokens // _SUBLANES, _SUBLANES, d), _F32),
        grid_spec=grid_spec,
        compiler_params=pltpu.CompilerParams(dimension_semantics=("arbitrary",)),
        name="moe",
    )(ea, eb, nvalid, src, src, he3, w1, w3, w2, w1, w3, w2, g, b)
    return out.reshape(n_tokens, d)


def _moe_plan(cls, n_tokens):
    tm = MOE_TM
    ntiles = n_tokens // tm + N_CLASSES
    cls = cls.reshape(n_tokens)
    order = jnp.argsort(cls, stable=True).astype(jnp.int32)
    counts = jnp.sum((cls[:, None] == jnp.arange(N_CLASSES, dtype=jnp.int32)[None, :]).astype(jnp.int32), axis=0)
    cstart = jnp.cumsum(counts) - counts
    ptiles = (counts + tm - 1) // tm
    tend = jnp.cumsum(ptiles)
    tstart = tend - ptiles
    tile = jnp.arange(ntiles, dtype=jnp.int32)
    tcls = jnp.minimum(jnp.sum((tile[:, None] >= tend[None, :]).astype(jnp.int32), axis=1), N_CLASSES - 1)
    used = tile < tend[-1]
    nvalid = jnp.where(used, jnp.clip(counts[tcls] - (tile - tstart[tcls]) * tm, 0, tm), 0).astype(jnp.int32)
    r = jnp.arange(tm, dtype=jnp.int32)[None, :]
    valid = r < nvalid[:, None]
    pos = jnp.clip((cstart[tcls] + (tile - tstart[tcls]) * tm)[:, None] + r, 0, n_tokens - 1)
    src = jnp.where(valid, order[pos], 0).astype(jnp.int32).reshape(ntiles, 1, tm)
    pair_lo = jnp.array([0, 0, 0, 1, 1, 2], jnp.int32)
    pair_hi = jnp.array([1, 2, 3, 2, 3, 3], jnp.int32)
    ea = (4 * (tcls // 6) + pair_lo[tcls % 6]).astype(jnp.int32)
    eb = (4 * (tcls // 6) + pair_hi[tcls % 6]).astype(jnp.int32)
    return ea, eb, nvalid, src


def _scan_slabs(a_refs, b_refs, c_refs, h_in, reverse):
    nlev = len(a_refs)
    ks = list(range(_SCAN_RADIX))
    if reverse:
        ks.reverse()
    for l in range(nlev - 1):
        grp = a_refs[l].shape[1] // _SCAN_RADIX
        p = q = None
        for k in ks:
            sl = pl.ds(k, grp, stride=_SCAN_RADIX)
            a = a_refs[l][:, sl, :]
            b = b_refs[l][:, sl, :]
            if p is None:
                p, q = a, b
            else:
                p, q = a * p, a * q + b
                a_refs[l][:, sl, :] = p
                b_refs[l][:, sl, :] = q
        a_refs[l + 1][...] = p
        b_refs[l + 1][...] = q
    rows = a_refs[-1].shape[1]
    a = a_refs[-1][...]
    b = b_refs[-1][...]
    out = [None] * rows
    h = h_in
    for r in (range(rows - 1, -1, -1) if reverse else range(rows)):
        h = a[:, r:r + 1, :] * h + b[:, r:r + 1, :]
        out[r] = h
    h_out = h
    b_refs[-1][...] = jnp.concatenate(out, axis=1)
    edge = jnp.broadcast_to(h_in, (h_in.shape[0], _SUBLANES, h_in.shape[2]))
    for l in range(nlev - 2, -1, -1):
        grp = a_refs[l].shape[1] // _SCAN_RADIX
        c = c_refs[l]
        c[:, _SUBLANES:_SUBLANES + grp, :] = b_refs[l + 1][...]
        if reverse:
            c[:, _SUBLANES + grp:2 * _SUBLANES + grp, :] = edge
            cin = c[:, _SUBLANES + 1:_SUBLANES + 1 + grp, :]
        else:
            c[:, 0:_SUBLANES, :] = edge
            cin = c[:, _SUBLANES - 1:_SUBLANES - 1 + grp, :]
        for k in ks:
            sl = pl.ds(k, grp, stride=_SCAN_RADIX)
            b_refs[l][:, sl, :] = a_refs[l][:, sl, :] * cin + b_refs[l][:, sl, :]
    return h_out


def _rglru_kernel(*refs, reverse, final, ntile):
    if final:
        (src_ref, prev_ref, next_ref, cw_ref, cb_ref, wa_ref, ba_ref, wx_ref, bx_ref, lam_ref,
         hf_ref, wgate_ref, res_ref, wout_ref, g_ref, b_ref, rwt_ref, rb_ref,
         h_ref, cls_ref, xe_sc, carry_sc, *scan_sc) = refs
    else:
        (src_ref, prev_ref, next_ref, cw_ref, cb_ref, wa_ref, ba_ref, wx_ref, bx_ref, lam_ref, wxr_ref,
         hout_ref, xr_out_ref, xe_sc, carry_sc, *scan_sc) = refs
    nlev = (len(scan_sc) + 1) // 3
    a_refs, b_refs, c_refs = scan_sc[:nlev], scan_sc[nlev:2 * nlev], scan_sc[2 * nlev:]
    i = pl.program_id(1)
    ti = (ntile - 1 - i) if reverse else i
    tt = src_ref.shape[0]

    @pl.when(i == 0)
    def _reset():
        carry_sc[...] = jnp.zeros_like(carry_sc)

    prev = jnp.where(ti > 0, prev_ref[...], 0.0)
    nxt = jnp.where(ti < ntile - 1, next_ref[...], 0.0)
    xin = jnp.concatenate([prev, src_ref[...], nxt], axis=0)
    if final:
        xe = xin
    else:
        xe = _dot(xin.astype(_BF16), wxr_ref[...])
        xr_out_ref[...] = xe[_SUBLANES:_SUBLANES + tt, :]
    for n in range(LRU_BLOCKS):
        xe_sc[n] = xe[:, n * LRU_BW:(n + 1) * LRU_BW]
    cw = cw_ref[...]
    cb = cb_ref[...]
    half_decay = (0.5 * LRU_C) * -jnp.log(1.0 + jnp.exp(-lam_ref[...]))
    for n in range(LRU_BLOCKS):
        sl = slice(n * LRU_BW, (n + 1) * LRU_BW)
        xcn = cb[:, sl]
        for k in range(cw.shape[0]):
            xcn = xcn + cw[k:k + 1, sl] * xe_sc[n, _SUBLANES - CONV_LEFT + k:_SUBLANES - CONV_LEFT + k + tt, :]
        xb = xcn.astype(_BF16)
        tr = jnp.tanh(_dot(xb, wa_ref[n]) + ba_ref[:, sl])
        ig = 0.5 * jnp.tanh(_dot(xb, wx_ref[n]) + bx_ref[:, sl]) + 0.5
        log_a = tr * half_decay[:, sl] + half_decay[:, sl]
        th = jnp.tanh(log_a)
        num = -2.0 * th
        scale = jnp.where(num > 0.0, num * lax.rsqrt(num * (1.0 - th)), 0.0)
        a_refs[0][n] = jnp.exp(log_a)
        b_refs[0][n] = scale * (ig * xcn)
    carry_sc[...] = _scan_slabs(a_refs, b_refs, c_refs, carry_sc[...], reverse)
    if final:
        gate = _dot(res_ref[...].astype(_BF16), wgate_ref[...])
    ys = []
    for n in range(LRU_BLOCKS):
        sl = slice(n * LRU_BW, (n + 1) * LRU_BW)
        h = b_refs[0][n]
        if final:
            ys.append((hf_ref[:, sl] + h) * jax.nn.gelu(gate[:, sl], approximate=True))
        else:
            hout_ref[:, sl] = h
    if final:
        y = jnp.concatenate(ys, axis=1).astype(_BF16)
        z = ALPHA * res_ref[...] + _dot(y, wout_ref[...])
        _ln_router_store(z, g_ref, b_ref, rwt_ref, rb_ref, h_ref, cls_ref)


def _rglru(src, cw, cb, wa, ba, wx, bx, lam, bsz, seq, tt, reverse, wxr=None, tail=None):
    ntile = seq // tt
    width = cw.shape[1]
    final = tail is not None
    tidx = (lambda i: ntile - 1 - i) if reverse else (lambda i: i)
    row = lambda b, i: (b * ntile + tidx(i), 0)
    g8 = tt // 8
    nb8 = seq // 8

    def prev_map(b, i):
        return (b * nb8 + jnp.maximum(tidx(i) * g8 - 1, 0), 0)

    def next_map(b, i):
        return (b * nb8 + jnp.minimum((tidx(i) + 1) * g8, nb8 - 1), 0)

    const = lambda shape: pl.BlockSpec(shape, lambda b, i: (0,) * len(shape))
    swidth = src.shape[1]
    in_specs = [
        pl.BlockSpec((tt, swidth), row),
        pl.BlockSpec((8, swidth), prev_map),
        pl.BlockSpec((8, swidth), next_map),
        const(cw.shape), const(cb.shape), const(wa.shape), const(ba.shape), const(wx.shape), const(bx.shape),
        const(lam.shape),
    ]
    args = [src, src, src, cw, cb, wa, ba, wx, bx, lam]
    rows = [tt]
    while rows[-1] > _SUBLANES:
        assert rows[-1] % _SCAN_RADIX == 0
        rows.append(rows[-1] // _SCAN_RADIX)
    assert rows[-1] == _SUBLANES
    slab = lambda r: pltpu.VMEM((LRU_BLOCKS, r, LRU_BW), _F32)
    scratch = [slab(tt + 2 * _SUBLANES), pltpu.VMEM((LRU_BLOCKS, 1, LRU_BW), _F32)]
    scratch += [slab(r) for r in rows] * 2 + [slab(r + 2 * _SUBLANES) for r in rows[1:]]
    if final:
        hf, wgate, res, wout, g, b, rwt, rb = tail
        d = res.shape[1]
        in_specs += [pl.BlockSpec((tt, width), row), const(wgate.shape), pl.BlockSpec((tt, d), row),
                     const(wout.shape), const(g.shape), const(b.shape), const(rwt.shape), const(rb.shape)]
        args += [hf, wgate, res, wout, g, b, rwt, rb]
        out_shape = (jax.ShapeDtypeStruct((bsz * seq, d + GATE_LANES), _F32),
                     jax.ShapeDtypeStruct((bsz * ntile, 1, tt), jnp.int32))
        out_specs = (pl.BlockSpec((tt, d + GATE_LANES), row),
                     pl.BlockSpec((1, 1, tt), lambda b, i: (b * ntile + tidx(i), 0, 0)))
    else:
        in_specs.append(const(wxr.shape))
        args.append(wxr)
        out_shape = (jax.ShapeDtypeStruct((bsz * seq, width), _F32),) * 2
        out_specs = (pl.BlockSpec((tt, width), row),) * 2
    return pl.pallas_call(
        functools.partial(_rglru_kernel, reverse=reverse, final=final, ntile=ntile),
        out_shape=out_shape,
        grid=(bsz, ntile),
        in_specs=in_specs,
        out_specs=out_specs,
        scratch_shapes=scratch,
        compiler_params=pltpu.CompilerParams(dimension_semantics=("arbitrary", "arbitrary")),
        name="rglru_bwd_tail" if final else "rglru_fwd",
    )(*args)


def _rotary_tables(seq):
    half = MLA_ROPE // 2
    inv_freq = ROPE_THETA ** (-jnp.arange(half, dtype=_F32) / half)
    ang = jnp.arange(seq, dtype=_F32)[:, None] * inv_freq[None, :]
    return jnp.cos(ang), jnp.sin(ang)


def _t5_bucket(rel):
    n_side = REL_BUCKETS // 2
    max_exact = n_side // 2
    dist = jnp.abs(rel)
    far = max_exact + (jnp.log(jnp.maximum(dist, 1).astype(_F32) / max_exact)
                       / math.log(REL_MAX_DIST / max_exact) * (n_side - max_exact)).astype(jnp.int32)
    far = jnp.minimum(far, n_side - 1)
    return jnp.where(rel > 0, n_side, 0) + jnp.where(dist < max_exact, dist, far)


def _band_buckets():
    rel = jnp.arange(3 * BLOCK)[None, :] - BLOCK - jnp.arange(BLOCK)[:, None]
    ids = jnp.where(jnp.abs(rel) <= WINDOW, _t5_bucket(rel), REL_BUCKETS).astype(jnp.int32)
    return jnp.bitwise_and(ids, 2 * REL_BUCKETS - 1)


def _pick_tile(seq, want):
    t = min(want, seq)
    while seq % t:
        t //= 2
    return t


def kernel(x, rel_bias, router_w, router_bias, l0_w_in, l0_q_norm, l0_w_uq, l0_kv_norm, l0_w_ukv, l0_sinks, l0_w_out, l0_ln1_g, l0_ln1_b, l0_w1, l0_w3, l0_w2, l0_ln2_g, l0_ln2_b, l1_w_in, l1_conv_w, l1_conv_b, l1_wa_f, l1_ba_f, l1_wx_f, l1_bx_f, l1_lam_f, l1_wa_b, l1_ba_b, l1_wx_b, l1_bx_b, l1_lam_b, l1_w_out, l1_ln1_g, l1_ln1_b, l1_w1, l1_w3, l1_w2, l1_ln2_g, l1_ln2_b):
    bsz, seq, d = x.shape
    n_tok = bsz * seq
    assert seq % BLOCK == 0 and d == LRU_BLOCKS * LRU_BW and n_tok % MOE_TM == 0
    x2d = x.reshape(n_tok, d)
    row = lambda v: v.reshape(1, -1).astype(_F32)
    rwt = router_w.astype(_F32).T
    rb = router_bias.astype(_F32).reshape(N_EXPERTS, 1)

    o = np.cumsum([0, MLA_Q_RANK, MLA_KV_RANK, MLA_ROPE, SWA_HEADS * SWA_D, SWA_KV_HEADS * SWA_D,
                   SWA_KV_HEADS * SWA_D])
    w_q, w_kv, w_kr, w_qb, w_kb, w_vb = [l0_w_in[:, o[i]:o[i + 1]] for i in range(6)]
    w_in0 = jnp.concatenate([w_q, w_kv, jnp.pad(w_kr, ((0, 0), (0, LANE - MLA_ROPE))), w_kb], axis=1).astype(_BF16)
    wuq = l0_w_uq.reshape(MLA_Q_RANK, MLA_HEADS, MLA_QK)
    wuqt = jnp.pad(wuq, ((0, 0), (0, 0), (0, HEAD_PAD - MLA_QK))).reshape(MLA_Q_RANK, MLA_HEADS * HEAD_PAD).T
    wukv = l0_w_ukv.reshape(MLA_KV_RANK, MLA_HEADS, MLA_NOPE + MLA_V)
    wuk = jnp.pad(wukv[:, :, :MLA_NOPE], ((0, 0), (0, 0), (0, HEAD_PAD - MLA_NOPE))).reshape(MLA_KV_RANK, -1)
    wuvt = wukv[:, :, MLA_NOPE:].reshape(MLA_KV_RANK, MLA_HEADS * MLA_V).T
    cos, sin = _rotary_tables(seq)
    ccs = jnp.pad(jnp.concatenate([cos, cos], axis=1), ((0, 0), (0, LANE - MLA_ROPE)))
    sn = jnp.pad(jnp.concatenate([-sin, sin], axis=1), ((0, 0), (0, LANE - MLA_ROPE)))

    tm = _pick_tile(seq, 512)
    qt, k, vt, qbt, kb, vbt = _attn_inproj(
        x2d, w_in0, w_qb.T.astype(_BF16), w_vb.T.astype(_BF16), row(l0_q_norm), wuqt.astype(_BF16),
        row(l0_kv_norm), wuk.astype(_BF16), wuvt.astype(_BF16), cos.T, sin.T, ccs, sn, bsz, seq, tm)
    nstream = 8 if seq % 4096 == 0 else (2 if seq % 1024 == 0 else 1)
    ot = _mla_flash(qt, k, vt, _pick_tile(seq // nstream, 512), _pick_tile(seq // 2, 512), nstream)
    kpad = jnp.pad(kb, ((0, 0), (BLOCK, BLOCK), (0, 0)))
    vtpad = jnp.pad(vbt, ((0, 0), (0, 0), (BLOCK, BLOCK)))
    obt = _swa(qbt, kpad, vtpad, _band_buckets().T, rel_bias.astype(_F32), l0_sinks.astype(_F32), bsz, seq,
               _pick_tile(seq // BLOCK, 8))
    he, cls = _outproj_ln_router(ot, obt, x2d, l0_w_out.astype(_BF16), row(l0_ln1_g), row(l0_ln1_b), rwt, rb,
                                 bsz, seq, tm)
    ea, eb, nvalid, src = _moe_plan(cls, n_tok)
    h = _moe(he, ea, eb, nvalid, src, l0_w1, l0_w3, l0_w2,
             row(l0_ln2_g), row(l0_ln2_b), n_tok, d)

    tt = _pick_tile(seq, 512)
    w_in1 = l1_w_in.astype(_BF16)
    w_gate, w_xr = w_in1[:, :d], w_in1[:, d:]
    cw = l1_conv_w.astype(_F32)
    cb = row(l1_conv_b)
    halfw = lambda w: (0.5 * w).astype(_BF16)
    halfb = lambda v: 0.5 * row(v)
    hf, xr = _rglru(h, cw, cb, halfw(l1_wa_f), halfb(l1_ba_f), halfw(l1_wx_f), halfb(l1_bx_f),
                    row(l1_lam_f), bsz, seq, tt, reverse=False, wxr=w_xr)
    he, cls = _rglru(xr, cw, cb, halfw(l1_wa_b), halfb(l1_ba_b), halfw(l1_wx_b), halfb(l1_bx_b),
                     row(l1_lam_b), bsz, seq, tt, reverse=True,
                     tail=(hf, w_gate, h, l1_w_out.astype(_BF16), row(l1_ln1_g), row(l1_ln1_b), rwt, rb))
    ea, eb, nvalid, src = _moe_plan(cls, n_tok)
    h = _moe(he, ea, eb, nvalid, src, l1_w1, l1_w3, l1_w2,
             row(l1_ln2_g), row(l1_ln2_b), n_tok, d)
    return h.reshape(bsz, seq, d)
```

```python
import functools
import math

import jax
import jax.numpy as jnp
import numpy as np
from jax import lax
from jax.experimental import pallas as pl
from jax.experimental.pallas import tpu as pltpu

MLA_HEADS = 8
MLA_Q_RANK = 256
MLA_KV_RANK = 128
MLA_NOPE = 64
MLA_ROPE = 32
MLA_V = 64
MLA_QK = MLA_NOPE + MLA_ROPE
ROPE_THETA = 10000.0
SWA_HEADS = 8
SWA_KV_HEADS = 2
SWA_REP = SWA_HEADS // SWA_KV_HEADS
SWA_D = 64
WINDOW = 128
BLOCK = 128
REL_BUCKETS = 32
REL_MAX_DIST = 128
LRU_BLOCKS = 8
LRU_BW = 128
LRU_C = 8.0
CONV_LEFT = 2
N_EXPERTS = 16
N_GROUPS = 4
EXPERTS_PER_GROUP = 4
N_CLASSES = N_GROUPS * 6
DEPTH = 2
ALPHA = (2.0 * DEPTH) ** 0.25
LN_EPS = 1e-5
RMS_EPS = 1e-6
NEG_BIG = -1e30

LANE = 128
HEAD_PAD = 128
GATE_LANES = 128
VT_ROWS = 80
_LOG2E = math.log2(math.e)
MOE_TM = 256

_F32 = jnp.float32
_BF16 = jnp.bfloat16
_NT_DIMS = (((1,), (1,)), ((), ()))
_TN_DIMS = (((0,), (0,)), ((), ()))


def _dot(a, b):
    return jnp.dot(a, b, preferred_element_type=_F32)


def _dot_nt(a, b):
    return lax.dot_general(a, b, _NT_DIMS, preferred_element_type=_F32)


def _sigmoid(x):
    return 0.5 * jnp.tanh(0.5 * x) + 0.5


def _rms(x, g):
    return x * lax.rsqrt(jnp.mean(jnp.square(x), -1, keepdims=True) + RMS_EPS) * g


def _layer_norm(x, g, b):
    mu = jnp.mean(x, -1, keepdims=True)
    xc = x - mu
    var = jnp.mean(jnp.square(xc), -1, keepdims=True)
    return xc * lax.rsqrt(var + LN_EPS) * g + b


def _attn_inproj_kernel(x_ref, w_ref, wqbt_ref, wvbt_ref, qn_ref, wuqt_ref, kvn_ref, wuk_ref, wuvt_ref,
                        cost_ref, sint_ref, ccs_ref, sn_ref,
                        qt_ref, k_ref, vt_ref, qbt_ref, kb_ref, vbt_ref):
    xb = x_ref[...].astype(_BF16)
    proj = _dot(xb, w_ref[...])
    q_lat = proj[:, 0:256]
    kv_lat = proj[:, 256:384]
    kr = proj[:, 384:512]
    qn = _rms(q_lat, qn_ref[...]).astype(_BF16)
    qt = _dot_nt(wuqt_ref[...], qn) * (MLA_QK ** -0.5 * _LOG2E)
    cos_t = cost_ref[...]
    sin_t = sint_ref[...]
    tm = qt.shape[1]
    zpad = jnp.zeros((HEAD_PAD - MLA_QK, tm), _F32)
    for h in range(MLA_HEADS):
        r0 = h * HEAD_PAD
        x1 = qt[r0 + 64:r0 + 80, :]
        x2 = qt[r0 + 80:r0 + 96, :]
        blk = jnp.concatenate([qt[r0:r0 + 64, :], x1 * cos_t - x2 * sin_t, x1 * sin_t + x2 * cos_t, zpad], axis=0)
        qt_ref[0, r0:r0 + HEAD_PAD, :] = blk.astype(_BF16)
    kvn = _rms(kv_lat, kvn_ref[...]).astype(_BF16)
    kn = _dot(kvn, wuk_ref[...])
    lane = lax.broadcasted_iota(jnp.int32, kr.shape, 1)
    swapped = jnp.where(lane < 16, pltpu.roll(kr, 112, 1), pltpu.roll(kr, 16, 1))
    kpe = kr * ccs_ref[...] + swapped * sn_ref[...]
    kpe = pltpu.roll(kpe, 64, 1)
    k_ref[0] = (kn + jnp.concatenate([kpe] * MLA_HEADS, axis=1)).astype(_BF16)
    vt = _dot_nt(wuvt_ref[...], kvn).astype(_BF16)
    extra = (lax.broadcasted_iota(jnp.int32, (VT_ROWS - MLA_V, tm), 0) == 0).astype(_BF16)
    for h in range(MLA_HEADS):
        vt_ref[0, h * VT_ROWS:h * VT_ROWS + MLA_V, :] = vt[h * MLA_V:(h + 1) * MLA_V, :]
        vt_ref[0, h * VT_ROWS + MLA_V:(h + 1) * VT_ROWS, :] = extra
    kb_ref[0] = proj[:, 512:640].astype(_BF16)
    vbt_ref[0] = _dot_nt(wvbt_ref[...], xb).astype(_BF16)
    qbt = (_dot_nt(wqbt_ref[...], xb) * (SWA_D ** -0.5)).astype(_BF16)
    zhalf = jnp.zeros((SWA_D, tm), _BF16)
    for h in range(SWA_HEADS):
        g = h // SWA_REP
        real = qbt[h * SWA_D:(h + 1) * SWA_D, :]
        qbt_ref[0, h * LANE + g * SWA_D:h * LANE + (g + 1) * SWA_D, :] = real
        qbt_ref[0, h * LANE + (1 - g) * SWA_D:h * LANE + (2 - g) * SWA_D, :] = zhalf


def _attn_inproj(x2d, w, wqbt, wvbt, qn, wuqt, kvn, wuk, wuvt, cos_t, sin_t, ccs, sn, bsz, seq, tm):
    nst = seq // tm
    const = lambda shape: pl.BlockSpec(shape, lambda b, i: (0,) * len(shape))
    out_shape = (
        jax.ShapeDtypeStruct((bsz, MLA_HEADS * HEAD_PAD, seq), _BF16),
        jax.ShapeDtypeStruct((bsz, seq, MLA_HEADS * HEAD_PAD), _BF16),
        jax.ShapeDtypeStruct((bsz, MLA_HEADS * VT_ROWS, seq), _BF16),
        jax.ShapeDtypeStruct((bsz, SWA_HEADS * LANE, seq), _BF16),
        jax.ShapeDtypeStruct((bsz, seq, SWA_KV_HEADS * SWA_D), _BF16),
        jax.ShapeDtypeStruct((bsz, SWA_KV_HEADS * SWA_D, seq), _BF16),
    )
    return pl.pallas_call(
        _attn_inproj_kernel,
        out_shape=out_shape,
        grid=(bsz, nst),
        in_specs=[
            pl.BlockSpec((tm, x2d.shape[1]), lambda b, i: (b * nst + i, 0)),
            const(w.shape), const(wqbt.shape), const(wvbt.shape), const(qn.shape), const(wuqt.shape),
            const(kvn.shape), const(wuk.shape), const(wuvt.shape),
            pl.BlockSpec((16, tm), lambda b, i: (0, i)),
            pl.BlockSpec((16, tm), lambda b, i: (0, i)),
            pl.BlockSpec((tm, LANE), lambda b, i: (i, 0)),
            pl.BlockSpec((tm, LANE), lambda b, i: (i, 0)),
        ],
        out_specs=(
            pl.BlockSpec((1, MLA_HEADS * HEAD_PAD, tm), lambda b, i: (b, 0, i)),
            pl.BlockSpec((1, tm, MLA_HEADS * HEAD_PAD), lambda b, i: (b, i, 0)),
            pl.BlockSpec((1, MLA_HEADS * VT_ROWS, tm), lambda b, i: (b, 0, i)),
            pl.BlockSpec((1, SWA_HEADS * LANE, tm), lambda b, i: (b, 0, i)),
            pl.BlockSpec((1, tm, SWA_KV_HEADS * SWA_D), lambda b, i: (b, i, 0)),
            pl.BlockSpec((1, SWA_KV_HEADS * SWA_D, tm), lambda b, i: (b, 0, i)),
        ),
        name="attn_inproj",
    )(x2d, w, wqbt, wvbt, qn, wuqt, kvn, wuk, wuvt, cos_t, sin_t, ccs, sn)


def _mla_flash_kernel(qt_ref, k_ref, vt_ref, ot_ref, *scratch, tk, nstream):
    st_sc = scratch[:2 * nstream]
    p_sc = scratch[2 * nstream:4 * nstream]
    tq = qt_ref.shape[2] // nstream
    nkv = k_ref.shape[1] // tk
    assert nkv % 2 == 0
    qts = [qt_ref[0, :, s * tq:(s + 1) * tq] for s in range(nstream)]

    def keys(j):
        return k_ref[0, pl.ds(pl.multiple_of(j * tk, tk), tk), :]

    def values(j):
        return vt_ref[0, :, pl.ds(pl.multiple_of(j * tk, tk), tk)]

    def phase(j, cur, state):
        nxt = 1 - cur
        kt = keys(jnp.minimum(j + 1, nkv - 1))
        vt = values(jnp.maximum(j - 1, 0))
        out = []
        for s in range(nstream):
            m, acc = state[s]
            st_sc[2 * s + nxt][...] = _dot(kt, qts[s])
            pv = _dot(vt, p_sc[2 * s + nxt][...])
            st = st_sc[2 * s + cur][...]
            m_new = jnp.maximum(m, jnp.max(st, axis=0, keepdims=True))
            alpha = jnp.exp2(m - m_new)
            p_sc[2 * s + cur][...] = jnp.exp2(st - m_new).astype(_BF16)
            out.append((m_new, (acc + pv) * alpha))
        return out

    def body(jj, state):
        state = phase(2 * jj, 0, state)
        return phase(2 * jj + 1, 1, state)

    k0 = keys(0)
    state = []
    for s in range(nstream):
        st_sc[2 * s][...] = _dot(k0, qts[s])
        p_sc[2 * s + 1][...] = jnp.zeros((tk, tq), _BF16)
        state.append((jnp.full((1, tq), -jnp.inf, _F32), jnp.zeros((VT_ROWS, tq), _F32)))
    state = lax.fori_loop(0, nkv // 2, body, state)
    v_last = values(nkv - 1)
    for s in range(nstream):
        acc = state[s][1] + _dot(v_last, p_sc[2 * s + 1][...])
        ot_ref[0, :, s * tq:(s + 1) * tq] = (acc[0:MLA_V, :] / acc[MLA_V:MLA_V + 1, :]).astype(ot_ref.dtype)


def _mla_flash(qt, k, vt, tq, tk, nstream):
    bsz, _, seq = qt.shape
    tqs = tq * nstream
    scratch = [pltpu.VMEM((tk, tq), _F32)] * (2 * nstream) + [pltpu.VMEM((tk, tq), _BF16)] * (2 * nstream)
    return pl.pallas_call(
        functools.partial(_mla_flash_kernel, tk=tk, nstream=nstream),
        out_shape=jax.ShapeDtypeStruct((bsz, MLA_HEADS * MLA_V, seq), _BF16),
        grid=(bsz, MLA_HEADS, seq // tqs),
        in_specs=[
            pl.BlockSpec((1, HEAD_PAD, tqs), lambda b, h, i: (b, h, i)),
            pl.BlockSpec((1, seq, HEAD_PAD), lambda b, h, i: (b, 0, h)),
            pl.BlockSpec((1, VT_ROWS, seq), lambda b, h, i: (b, h, 0)),
        ],
        out_specs=pl.BlockSpec((1, MLA_V, tqs), lambda b, h, i: (b, h, i)),
        scratch_shapes=scratch,
        name="mla_flash",
    )(qt, k, vt)


def _swa_kernel(relb_ref, sinks_ref, qt_ref, k_ref, vt_ref, bucket_ref, o_ref, bias_sc, *, nblk, seq):
    first = jnp.logical_and(pl.program_id(0) == 0, pl.program_id(1) == 0)

    @pl.when(first)
    def _build_bias():
        bucket = bucket_ref[...]
        for h in range(SWA_HEADS):
            acc = jnp.full(bucket.shape, NEG_BIG, _F32)
            for bk in range(REL_BUCKETS):
                acc = jnp.where(bucket == bk, relb_ref[bk, h], acc)
            bias_sc[h] = acc

    j = pl.program_id(1)
    krow = lax.broadcasted_iota(jnp.int32, (3 * BLOCK, 1), 0)

    def block(u, c):
        n = j * nblk + u
        c0 = pl.multiple_of(u * BLOCK, BLOCK)
        w0 = pl.multiple_of(n * BLOCK, BLOCK)
        key_pos = n * BLOCK - BLOCK + krow
        emask = jnp.where(jnp.logical_and(key_pos >= 0, key_pos < seq), 0.0, NEG_BIG).astype(_F32)
        kw = k_ref[0, pl.ds(w0, 3 * BLOCK), :]
        vw = vt_ref[0, :, pl.ds(w0, 3 * BLOCK)]
        for g in range(SWA_KV_HEADS):
            heads = range(g * SWA_REP, (g + 1) * SWA_REP)
            qs = jnp.concatenate([qt_ref[0, h * LANE:(h + 1) * LANE, pl.ds(c0, BLOCK)] for h in heads], axis=1)
            bias = jnp.concatenate([bias_sc[h] for h in heads], axis=1)
            sink = jnp.concatenate([jnp.full((1, BLOCK), sinks_ref[h], _F32) for h in heads], axis=1)
            st = _dot(kw, qs) + bias + emask
            m = jnp.maximum(jnp.max(st, axis=0, keepdims=True), sink)
            p = jnp.exp(st - m)
            den = jnp.sum(p, axis=0, keepdims=True) + jnp.exp(sink - m)
            ot = _dot(vw[g * SWA_D:(g + 1) * SWA_D, :], p.astype(_BF16)) / den
            for r, h in enumerate(heads):
                o_ref[0, h * SWA_D:(h + 1) * SWA_D, pl.ds(c0, BLOCK)] = ot[:, r * BLOCK:(r + 1) * BLOCK].astype(o_ref.dtype)
        return c

    def pair(t, c):
        block(2 * t, c)
        return block(2 * t + 1, c)

    if nblk % 2 == 0:
        lax.fori_loop(0, nblk // 2, pair, 0)
    else:
        lax.fori_loop(0, nblk, block, 0)


def _swa(qbt, kpad, vtpad, bucket_t, rel_bias, sinks, bsz, seq, nblk):
    nsteps = seq // (nblk * BLOCK)
    cols = nblk * BLOCK
    return pl.pallas_call(
        functools.partial(_swa_kernel, nblk=nblk, seq=seq),
        out_shape=jax.ShapeDtypeStruct((bsz, SWA_HEADS * SWA_D, seq), _BF16),
        grid=(bsz, nsteps),
        in_specs=[
            pl.BlockSpec(memory_space=pltpu.SMEM),
            pl.BlockSpec(memory_space=pltpu.SMEM),
            pl.BlockSpec((1, SWA_HEADS * LANE, cols), lambda b, j: (b, 0, j)),
            pl.BlockSpec((1, seq + 2 * BLOCK, SWA_KV_HEADS * SWA_D), lambda b, j: (b, 0, 0)),
            pl.BlockSpec((1, SWA_KV_HEADS * SWA_D, seq + 2 * BLOCK), lambda b, j: (b, 0, 0)),
            pl.BlockSpec((3 * BLOCK, BLOCK), lambda b, j: (0, 0)),
        ],
        out_specs=pl.BlockSpec((1, SWA_HEADS * SWA_D, cols), lambda b, j: (b, 0, j)),
        scratch_shapes=[pltpu.VMEM((SWA_HEADS, 3 * BLOCK, BLOCK), _F32)],
        compiler_params=pltpu.CompilerParams(dimension_semantics=("arbitrary", "arbitrary")),
        name="swa",
    )(rel_bias, sinks, qbt, kpad, vtpad, bucket_t)


def _route(logits_t, rbias):
    sc = jax.nn.sigmoid(logits_t)
    bz = sc + rbias
    s_rows = [sc[e:e + 1, :] for e in range(N_EXPERTS)]
    b_rows = [bz[e:e + 1, :] for e in range(N_EXPERTS)]
    gsel = None
    best = None
    for g in range(N_GROUPS):
        r = b_rows[4 * g:4 * g + 4]
        gs = r[0] + r[1]
        for (i, k) in ((0, 2), (0, 3), (1, 2), (1, 3), (2, 3)):
            gs = jnp.maximum(gs, r[i] + r[k])
        if g == 0:
            gsel = jnp.zeros(gs.shape, jnp.int32)
            best = gs
        else:
            better = gs > best
            gsel = jnp.where(better, g, gsel)
            best = jnp.where(better, gs, best)

    def pick(rows, k):
        out = rows[12 + k]
        for g in (2, 1, 0):
            out = jnp.where(gsel == g, rows[4 * g + k], out)
        return out

    v = [pick(b_rows, k) for k in range(4)]
    s = [pick(s_rows, k) for k in range(4)]
    i1 = jnp.zeros(gsel.shape, jnp.int32)
    m1 = v[0]
    w1 = s[0]
    for k in range(1, 4):
        gt = v[k] > m1
        i1 = jnp.where(gt, k, i1)
        m1 = jnp.where(gt, v[k], m1)
        w1 = jnp.where(gt, s[k], w1)
    i2 = jnp.full(gsel.shape, -1, jnp.int32)
    m2 = jnp.full(m1.shape, -jnp.inf, _F32)
    w2 = jnp.zeros(m1.shape, _F32)
    for k in range(4):
        ok = jnp.logical_and(i1 != k, jnp.logical_or(i2 < 0, v[k] > m2))
        i2 = jnp.where(ok, k, i2)
        m2 = jnp.where(ok, v[k], m2)
        w2 = jnp.where(ok, s[k], w2)
    tot = w1 + w2
    g1 = w1 / tot
    g2 = w2 / tot
    first_lo = i1 < i2
    lo = jnp.where(first_lo, i1, i2)
    hi = jnp.where(first_lo, i2, i1)
    pair = jnp.where(lo == 0, hi - 1, jnp.where(lo == 1, hi + 1, 5))
    cls = gsel * 6 + pair
    return cls, jnp.where(first_lo, g1, g2), jnp.where(first_lo, g2, g1)


def _ln_router_store(z, g_ref, b_ref, rwt_ref, rb_ref, h_ref, cls_ref):
    h = _layer_norm(z, g_ref[...], b_ref[...])
    tm = h.shape[0]
    h_hi = h.astype(_BF16)
    h_lo = (h - h_hi.astype(_F32)).astype(_BF16)
    rw = rwt_ref[...]
    rw_hi = rw.astype(_BF16)
    rw_lo = (rw - rw_hi.astype(_F32)).astype(_BF16)
    part = _dot_nt(jnp.concatenate([rw_hi, rw_lo], axis=0), h_hi)
    logits_t = part[0:N_EXPERTS] + part[N_EXPERTS:2 * N_EXPERTS] + _dot_nt(rw_hi, h_lo)
    cls, g_lo, g_hi = _route(logits_t, rb_ref[...])
    rows = jnp.concatenate([g_lo, g_hi, jnp.zeros((GATE_LANES - 2, tm), _F32)], axis=0)
    d = h.shape[1]
    h_ref[:, 0:d] = h
    h_ref[:, d:d + GATE_LANES] = rows.T
    cls_ref[0] = cls


def _outproj_ln_router_kernel(ot_ref, ob_ref, x_ref, w_ref, g_ref, b_ref, rwt_ref, rb_ref, h_ref, cls_ref):
    heads_t = jnp.concatenate([ot_ref[0], ob_ref[0]], axis=0)
    mixed = lax.dot_general(heads_t, w_ref[...], _TN_DIMS, preferred_element_type=_F32)
    z = ALPHA * x_ref[...] + mixed
    _ln_router_store(z, g_ref, b_ref, rwt_ref, rb_ref, h_ref, cls_ref)


def _outproj_ln_router(ot, ob, x2d, w, g, b, rwt, rb, bsz, seq, tm):
    nst = seq // tm
    d = x2d.shape[1]
    const = lambda shape: pl.BlockSpec(shape, lambda bb, i: (0,) * len(shape))
    return pl.pallas_call(
        _outproj_ln_router_kernel,
        out_shape=(jax.ShapeDtypeStruct((bsz * seq, d + GATE_LANES), _F32),
                   jax.ShapeDtypeStruct((bsz * nst, 1, tm), jnp.int32)),
        grid=(bsz, nst),
        in_specs=[
            pl.BlockSpec((1, ot.shape[1], tm), lambda bb, i: (bb, 0, i)),
            pl.BlockSpec((1, ob.shape[1], tm), lambda bb, i: (bb, 0, i)),
            pl.BlockSpec((tm, d), lambda bb, i: (bb * nst + i, 0)),
            const(w.shape), const(g.shape), const(b.shape), const(rwt.shape), const(rb.shape),
        ],
        out_specs=(pl.BlockSpec((tm, d + GATE_LANES), lambda bb, i: (bb * nst + i, 0)),
                   pl.BlockSpec((1, 1, tm), lambda bb, i: (bb * nst + i, 0, 0))),
        name="outproj_ln_router",
    )(ot, ob, x2d, w, g, b, rwt, rb)


_SUBLANES = 8
_SCAN_RADIX = 4
_DMA_CHUNK = 32


def _moe_kernel(ea_ref, eb_ref, nv_ref, off_ref, src_ref, nsrc_ref, h_hbm, w1a, w3a, w2a, w1b, w3b, w2b, g_ref, b_ref,
                out_hbm, xbuf, obuf, w13a, w2a_bf, w13b, w2b_bf, sem_in, sem_out):
    i = pl.program_id(0)
    ntiles = pl.num_programs(0)
    tm = xbuf.shape[1] * _SUBLANES
    d = obuf.shape[2]
    nv = nv_ref[i]
    nv_prev = jnp.where(i > 0, nv_ref[jnp.maximum(i - 1, 0)], 0)
    nv_next = jnp.where(i + 1 < ntiles, nv_ref[jnp.minimum(i + 1, ntiles - 1)], 0)
    slot = lax.rem(i, 2)
    shift = lax.rem(off_ref[i], LANE)
    shift_next = lax.rem(off_ref[jnp.minimum(i + 1, ntiles - 1)], LANE)

    def hbm_row(ref, idx):
        return ref.at[lax.shift_right_logical(idx, 3), jnp.bitwise_and(idx, _SUBLANES - 1)]

    def gather_start(idx_ref, sh, s):
        for r in range(tm):
            idx = idx_ref[sh + r]
            pltpu.make_async_copy(hbm_row(h_hbm, idx), xbuf.at[s, r // _SUBLANES, r % _SUBLANES], sem_in.at[s]).start()

    def scatter_wait(n):
        for bit in range(tm.bit_length()):
            rows = 1 << bit

            @pl.when(jnp.bitwise_and(n, rows) != 0)
            def _():
                if rows >= _SUBLANES:
                    grp = pl.ds(0, rows // _SUBLANES)
                    pltpu.make_async_copy(obuf.at[grp], out_hbm.at[grp], sem_out).wait()
                else:
                    pltpu.make_async_copy(obuf.at[0, pl.ds(0, rows)], out_hbm.at[0, pl.ds(0, rows)], sem_out).wait()

    @pl.when(jnp.logical_and(i == 0, nv > 0))
    def _prologue():
        gather_start(src_ref, shift, 0)

    @pl.when(nv_next > 0)
    def _prefetch():
        gather_start(nsrc_ref, shift_next, 1 - slot)

    prev_i = jnp.maximum(i - 1, 0)
    dff = w2a.shape[0]

    def refresh(w1, w3, w2, w13_sc, w2_sc):
        w13_sc[:, 0:dff] = w1[...].astype(_BF16)
        w13_sc[:, dff:2 * dff] = w3[...].astype(_BF16)
        w2_sc[...] = w2[...].astype(_BF16)

    @pl.when(jnp.logical_and(nv > 0, jnp.logical_or(i == 0, ea_ref[i] != ea_ref[prev_i])))
    def _refresh_a():
        refresh(w1a, w3a, w2a, w13a, w2a_bf)

    @pl.when(jnp.logical_and(nv > 0, jnp.logical_or(i == 0, eb_ref[i] != eb_ref[prev_i])))
    def _refresh_b():
        refresh(w1b, w3b, w2b, w13b, w2b_bf)

    @pl.when(nv > 0)
    def _tile():
        pltpu.make_async_copy(h_hbm.at[pl.ds(0, tm // _SUBLANES)], xbuf.at[slot], sem_in.at[slot]).wait()
        def compute(rows):
            xg = xbuf[slot, 0:rows // _SUBLANES].reshape(rows, d + GATE_LANES)
            x = xg[:, 0:d]
            gates = xg[:, d:d + GATE_LANES]
            ga = gates[:, 0:1]
            gb = gates[:, 1:2]
            xb = x.astype(_BF16)

            def expert(w13, w2):
                uv = _dot(xb, w13[...])
                u = uv[:, 0:dff]
                hh = u * _sigmoid(u) * uv[:, dff:2 * dff]
                return _dot(hh.astype(_BF16), w2[...])

            y = ga * expert(w13a, w2a_bf) + gb * expert(w13b, w2b_bf)
            z = _layer_norm(ALPHA * x + y, g_ref[...], b_ref[...])

            @pl.when(nv_prev > 0)
            def _drain_prev():
                scatter_wait(nv_prev)

            obuf[0:rows // _SUBLANES] = z.reshape(rows // _SUBLANES, _SUBLANES, d)

        @pl.when(nv > tm // 2)
        def _full():
            compute(tm)

        @pl.when(nv <= tm // 2)
        def _half():
            compute(tm // 2)

        for c in range(tm // _DMA_CHUNK):
            @pl.when(nv >= (c + 1) * _DMA_CHUNK)
            def _chunk():
                for r in range(c * _DMA_CHUNK, (c + 1) * _DMA_CHUNK):
                    idx = src_ref[shift + r]
                    pltpu.make_async_copy(obuf.at[r // _SUBLANES, r % _SUBLANES], hbm_row(out_hbm, idx),
                                          sem_out).start()

        def group(gi, c):
            for u in range(_SUBLANES):
                idx = src_ref[shift + gi * _SUBLANES + u]
                pltpu.make_async_copy(obuf.at[gi, u], hbm_row(out_hbm, idx), sem_out).start()
            return c

        def single(r, c):
            idx = src_ref[shift + r]
            pltpu.make_async_copy(hbm_row(obuf, r), hbm_row(out_hbm, idx), sem_out).start()
            return c

        ngroups = lax.shift_right_logical(nv, 3)
        lax.fori_loop((nv // _DMA_CHUNK) * (_DMA_CHUNK // _SUBLANES), ngroups, group, 0)
        lax.fori_loop(ngroups * _SUBLANES, nv, single, 0)

        @pl.when(nv_next == 0)
        def _drain_last():
            scatter_wait(nv)


def _moe(he, ea, eb, nvalid, off, order, w1, w3, w2, g, b, n_tokens, d):
    ntiles = ea.shape[0]
    tm = MOE_TM
    dff = w2.shape[1]
    wspec_a = lambda shape: pl.BlockSpec((None,) + shape, lambda i, ea_r, eb_r, nv_r, off_r: (ea_r[i], 0, 0))
    wspec_b = lambda shape: pl.BlockSpec((None,) + shape, lambda i, ea_r, eb_r, nv_r, off_r: (eb_r[i], 0, 0))

    def window(nxt):
        def start(i, ea_r, eb_r, nv_r, off_r):
            o = off_r[jnp.minimum(i + nxt, ntiles - 1)]
            return (pl.multiple_of((o // LANE) * LANE, LANE),)

        return pl.BlockSpec((pl.Element(2 * tm),), start, memory_space=pltpu.SMEM)

    grid_spec = pltpu.PrefetchScalarGridSpec(
        num_scalar_prefetch=4,
        grid=(ntiles,),
        in_specs=[
            window(0), window(1),
            pl.BlockSpec(memory_space=pl.ANY),
            wspec_a((d, dff)), wspec_a((d, dff)), wspec_a((dff, d)),
            wspec_b((d, dff)), wspec_b((d, dff)), wspec_b((dff, d)),
            pl.BlockSpec((1, d), lambda i, *_: (0, 0)),
            pl.BlockSpec((1, d), lambda i, *_: (0, 0)),
        ],
        out_specs=pl.BlockSpec(memory_space=pl.ANY),
        scratch_shapes=[
            pltpu.VMEM((2, tm // _SUBLANES, _SUBLANES, d + GATE_LANES), _F32),
            pltpu.VMEM((tm // _SUBLANES, _SUBLANES, d), _F32),
            pltpu.VMEM((d, 2 * dff), _BF16), pltpu.VMEM((dff, d), _BF16),
            pltpu.VMEM((d, 2 * dff), _BF16), pltpu.VMEM((dff, d), _BF16),
            pltpu.SemaphoreType.DMA((2,)),
            pltpu.SemaphoreType.DMA,
        ],
    )
    he3 = he.reshape(n_tokens // _SUBLANES, _SUBLANES, d + GATE_LANES)
    out = pl.pallas_call(
        _moe_kernel,
        out_shape=jax.ShapeDtypeStruct((n_tokens // _SUBLANES, _SUBLANES, d), _F32),
        grid_spec=grid_spec,
        compiler_params=pltpu.CompilerParams(dimension_semantics=("arbitrary",)),
        name="moe",
    )(ea, eb, nvalid, off, order, order, he3, w1, w3, w2, w1, w3, w2, g, b)
    return out.reshape(n_tokens, d)


def _moe_plan(cls, n_tokens):
    tm = MOE_TM
    ntiles = n_tokens // tm + N_CLASSES
    cls = cls.reshape(n_tokens)
    order = jnp.argsort(cls, stable=True).astype(jnp.int32)
    counts = jnp.sum((cls[:, None] == jnp.arange(N_CLASSES, dtype=jnp.int32)[None, :]).astype(jnp.int32), axis=0)
    cstart = jnp.cumsum(counts) - counts
    ptiles = (counts + tm - 1) // tm
    tend = jnp.cumsum(ptiles)
    tstart = tend - ptiles
    tile = jnp.arange(ntiles, dtype=jnp.int32)
    tcls = jnp.minimum(jnp.sum((tile[:, None] >= tend[None, :]).astype(jnp.int32), axis=1), N_CLASSES - 1)
    used = tile < tend[-1]
    nvalid = jnp.where(used, jnp.clip(counts[tcls] - (tile - tstart[tcls]) * tm, 0, tm), 0).astype(jnp.int32)
    off = jnp.where(used, cstart[tcls] + (tile - tstart[tcls]) * tm, 0).astype(jnp.int32)
    order = jnp.concatenate([order, jnp.zeros((2 * tm,), jnp.int32)])
    pair_lo = jnp.array([0, 0, 0, 1, 1, 2], jnp.int32)
    pair_hi = jnp.array([1, 2, 3, 2, 3, 3], jnp.int32)
    ea = (4 * (tcls // 6) + pair_lo[tcls % 6]).astype(jnp.int32)
    eb = (4 * (tcls // 6) + pair_hi[tcls % 6]).astype(jnp.int32)
    return ea, eb, nvalid, off, order


def _scan_slabs(a_refs, b_refs, c_refs, h_in, reverse):
    nlev = len(a_refs)
    ks = list(range(_SCAN_RADIX))
    if reverse:
        ks.reverse()
    for l in range(nlev - 1):
        grp = a_refs[l].shape[1] // _SCAN_RADIX
        p = q = None
        for k in ks:
            sl = pl.ds(k, grp, stride=_SCAN_RADIX)
            a = a_refs[l][:, sl, :]
            b = b_refs[l][:, sl, :]
            if p is None:
                p, q = a, b
            else:
                p, q = a * p, a * q + b
                a_refs[l][:, sl, :] = p
                b_refs[l][:, sl, :] = q
        a_refs[l + 1][...] = p
        b_refs[l + 1][...] = q
    rows = a_refs[-1].shape[1]
    a = a_refs[-1][...]
    b = b_refs[-1][...]
    out = [None] * rows
    h = h_in
    for r in (range(rows - 1, -1, -1) if reverse else range(rows)):
        h = a[:, r:r + 1, :] * h + b[:, r:r + 1, :]
        out[r] = h
    h_out = h
    b_refs[-1][...] = jnp.concatenate(out, axis=1)
    edge = jnp.broadcast_to(h_in, (h_in.shape[0], _SUBLANES, h_in.shape[2]))
    for l in range(nlev - 2, -1, -1):
        grp = a_refs[l].shape[1] // _SCAN_RADIX
        c = c_refs[l]
        c[:, _SUBLANES:_SUBLANES + grp, :] = b_refs[l + 1][...]
        if reverse:
            c[:, _SUBLANES + grp:2 * _SUBLANES + grp, :] = edge
            cin = c[:, _SUBLANES + 1:_SUBLANES + 1 + grp, :]
        else:
            c[:, 0:_SUBLANES, :] = edge
            cin = c[:, _SUBLANES - 1:_SUBLANES - 1 + grp, :]
        for k in ks:
            sl = pl.ds(k, grp, stride=_SCAN_RADIX)
            b_refs[l][:, sl, :] = a_refs[l][:, sl, :] * cin + b_refs[l][:, sl, :]
    return h_out


def _rglru_kernel(*refs, reverse, final, ntile):
    if final:
        (src_ref, prev_ref, next_ref, cw_ref, cb_ref, wa_ref, ba_ref, wx_ref, bx_ref, lam_ref,
         hf_ref, wgate_ref, res_ref, wout_ref, g_ref, b_ref, rwt_ref, rb_ref,
         h_ref, cls_ref, xe_sc, carry_sc, *scan_sc) = refs
    else:
        (src_ref, prev_ref, next_ref, cw_ref, cb_ref, wa_ref, ba_ref, wx_ref, bx_ref, lam_ref, wxr_ref,
         hout_ref, xr_out_ref, xe_sc, carry_sc, *scan_sc) = refs
    nlev = (len(scan_sc) + 1) // 3
    a_refs, b_refs, c_refs = scan_sc[:nlev], scan_sc[nlev:2 * nlev], scan_sc[2 * nlev:]
    i = pl.program_id(1)
    ti = (ntile - 1 - i) if reverse else i
    tt = src_ref.shape[0]

    @pl.when(i == 0)
    def _reset():
        carry_sc[...] = jnp.zeros_like(carry_sc)

    prev = jnp.where(ti > 0, prev_ref[...], 0.0)
    nxt = jnp.where(ti < ntile - 1, next_ref[...], 0.0)
    xin = jnp.concatenate([prev, src_ref[...], nxt], axis=0)
    if final:
        xe = xin
    else:
        xe = _dot(xin.astype(_BF16), wxr_ref[...])
        xr_out_ref[...] = xe[_SUBLANES:_SUBLANES + tt, :]
    for n in range(LRU_BLOCKS):
        xe_sc[n] = xe[:, n * LRU_BW:(n + 1) * LRU_BW]
    cw = cw_ref[...]
    cb = cb_ref[...]
    half_decay = (0.5 * LRU_C) * -jnp.log(1.0 + jnp.exp(-lam_ref[...]))
    for n in range(LRU_BLOCKS):
        sl = slice(n * LRU_BW, (n + 1) * LRU_BW)
        xcn = cb[:, sl]
        for k in range(cw.shape[0]):
            xcn = xcn + cw[k:k + 1, sl] * xe_sc[n, _SUBLANES - CONV_LEFT + k:_SUBLANES - CONV_LEFT + k + tt, :]
        xb = xcn.astype(_BF16)
        tr = jnp.tanh(_dot(xb, wa_ref[n]) + ba_ref[:, sl])
        ig = 0.5 * jnp.tanh(_dot(xb, wx_ref[n]) + bx_ref[:, sl]) + 0.5
        log_a = tr * half_decay[:, sl] + half_decay[:, sl]
        th = jnp.tanh(log_a)
        num = -2.0 * th
        scale = jnp.where(num > 0.0, num * lax.rsqrt(num * (1.0 - th)), 0.0)
        a_refs[0][n] = jnp.exp(log_a)
        b_refs[0][n] = scale * (ig * xcn)
    carry_sc[...] = _scan_slabs(a_refs, b_refs, c_refs, carry_sc[...], reverse)
    if final:
        gate = _dot(res_ref[...].astype(_BF16), wgate_ref[...])
    ys = []
    for n in range(LRU_BLOCKS):
        sl = slice(n * LRU_BW, (n + 1) * LRU_BW)
        h = b_refs[0][n]
        if final:
            ys.append((hf_ref[:, sl] + h) * jax.nn.gelu(gate[:, sl], approximate=True))
        else:
            hout_ref[:, sl] = h
    if final:
        y = jnp.concatenate(ys, axis=1).astype(_BF16)
        z = ALPHA * res_ref[...] + _dot(y, wout_ref[...])
        _ln_router_store(z, g_ref, b_ref, rwt_ref, rb_ref, h_ref, cls_ref)


def _rglru(src, cw, cb, wa, ba, wx, bx, lam, bsz, seq, tt, reverse, wxr=None, tail=None):
    ntile = seq // tt
    width = cw.shape[1]
    final = tail is not None
    tidx = (lambda i: ntile - 1 - i) if reverse else (lambda i: i)
    row = lambda b, i: (b * ntile + tidx(i), 0)
    g8 = tt // 8
    nb8 = seq // 8

    def prev_map(b, i):
        return (b * nb8 + jnp.maximum(tidx(i) * g8 - 1, 0), 0)

    def next_map(b, i):
        return (b * nb8 + jnp.minimum((tidx(i) + 1) * g8, nb8 - 1), 0)

    const = lambda shape: pl.BlockSpec(shape, lambda b, i: (0,) * len(shape))
    swidth = src.shape[1]
    in_specs = [
        pl.BlockSpec((tt, swidth), row),
        pl.BlockSpec((8, swidth), prev_map),
        pl.BlockSpec((8, swidth), next_map),
        const(cw.shape), const(cb.shape), const(wa.shape), const(ba.shape), const(wx.shape), const(bx.shape),
        const(lam.shape),
    ]
    args = [src, src, src, cw, cb, wa, ba, wx, bx, lam]
    rows = [tt]
    while rows[-1] > _SUBLANES:
        assert rows[-1] % _SCAN_RADIX == 0
        rows.append(rows[-1] // _SCAN_RADIX)
    assert rows[-1] == _SUBLANES
    slab = lambda r: pltpu.VMEM((LRU_BLOCKS, r, LRU_BW), _F32)
    scratch = [slab(tt + 2 * _SUBLANES), pltpu.VMEM((LRU_BLOCKS, 1, LRU_BW), _F32)]
    scratch += [slab(r) for r in rows] * 2 + [slab(r + 2 * _SUBLANES) for r in rows[1:]]
    if final:
        hf, wgate, res, wout, g, b, rwt, rb = tail
        d = res.shape[1]
        in_specs += [pl.BlockSpec((tt, width), row), const(wgate.shape), pl.BlockSpec((tt, d), row),
                     const(wout.shape), const(g.shape), const(b.shape), const(rwt.shape), const(rb.shape)]
        args += [hf, wgate, res, wout, g, b, rwt, rb]
        out_shape = (jax.ShapeDtypeStruct((bsz * seq, d + GATE_LANES), _F32),
                     jax.ShapeDtypeStruct((bsz * ntile, 1, tt), jnp.int32))
        out_specs = (pl.BlockSpec((tt, d + GATE_LANES), row),
                     pl.BlockSpec((1, 1, tt), lambda b, i: (b * ntile + tidx(i), 0, 0)))
    else:
        in_specs.append(const(wxr.shape))
        args.append(wxr)
        out_shape = (jax.ShapeDtypeStruct((bsz * seq, width), _F32),) * 2
        out_specs = (pl.BlockSpec((tt, width), row),) * 2
    return pl.pallas_call(
        functools.partial(_rglru_kernel, reverse=reverse, final=final, ntile=ntile),
        out_shape=out_shape,
        grid=(bsz, ntile),
        in_specs=in_specs,
        out_specs=out_specs,
        scratch_shapes=scratch,
        compiler_params=pltpu.CompilerParams(dimension_semantics=("arbitrary", "arbitrary")),
        name="rglru_bwd_tail" if final else "rglru_fwd",
    )(*args)


def _rotary_tables(seq):
    half = MLA_ROPE // 2
    inv_freq = ROPE_THETA ** (-jnp.arange(half, dtype=_F32) / half)
    ang = jnp.arange(seq, dtype=_F32)[:, None] * inv_freq[None, :]
    return jnp.cos(ang), jnp.sin(ang)


def _t5_bucket(rel):
    n_side = REL_BUCKETS // 2
    max_exact = n_side // 2
    dist = jnp.abs(rel)
    far = max_exact + (jnp.log(jnp.maximum(dist, 1).astype(_F32) / max_exact)
                       / math.log(REL_MAX_DIST / max_exact) * (n_side - max_exact)).astype(jnp.int32)
    far = jnp.minimum(far, n_side - 1)
    return jnp.where(rel > 0, n_side, 0) + jnp.where(dist < max_exact, dist, far)


def _band_buckets():
    rel = jnp.arange(3 * BLOCK)[None, :] - BLOCK - jnp.arange(BLOCK)[:, None]
    ids = jnp.where(jnp.abs(rel) <= WINDOW, _t5_bucket(rel), REL_BUCKETS).astype(jnp.int32)
    return jnp.bitwise_and(ids, 2 * REL_BUCKETS - 1)


def _pick_tile(seq, want):
    t = min(want, seq)
    while seq % t:
        t //= 2
    return t


def kernel(x, rel_bias, router_w, router_bias, l0_w_in, l0_q_norm, l0_w_uq, l0_kv_norm, l0_w_ukv, l0_sinks, l0_w_out, l0_ln1_g, l0_ln1_b, l0_w1, l0_w3, l0_w2, l0_ln2_g, l0_ln2_b, l1_w_in, l1_conv_w, l1_conv_b, l1_wa_f, l1_ba_f, l1_wx_f, l1_bx_f, l1_lam_f, l1_wa_b, l1_ba_b, l1_wx_b, l1_bx_b, l1_lam_b, l1_w_out, l1_ln1_g, l1_ln1_b, l1_w1, l1_w3, l1_w2, l1_ln2_g, l1_ln2_b):
    bsz, seq, d = x.shape
    n_tok = bsz * seq
    assert seq % BLOCK == 0 and d == LRU_BLOCKS * LRU_BW and n_tok % MOE_TM == 0
    x2d = x.reshape(n_tok, d)
    row = lambda v: v.reshape(1, -1).astype(_F32)
    rwt = router_w.astype(_F32).T
    rb = router_bias.astype(_F32).reshape(N_EXPERTS, 1)

    o = np.cumsum([0, MLA_Q_RANK, MLA_KV_RANK, MLA_ROPE, SWA_HEADS * SWA_D, SWA_KV_HEADS * SWA_D,
                   SWA_KV_HEADS * SWA_D])
    w_q, w_kv, w_kr, w_qb, w_kb, w_vb = [l0_w_in[:, o[i]:o[i + 1]] for i in range(6)]
    w_in0 = jnp.concatenate([w_q, w_kv, jnp.pad(w_kr, ((0, 0), (0, LANE - MLA_ROPE))), w_kb], axis=1).astype(_BF16)
    wuq = l0_w_uq.reshape(MLA_Q_RANK, MLA_HEADS, MLA_QK)
    wuqt = jnp.pad(wuq, ((0, 0), (0, 0), (0, HEAD_PAD - MLA_QK))).reshape(MLA_Q_RANK, MLA_HEADS * HEAD_PAD).T
    wukv = l0_w_ukv.reshape(MLA_KV_RANK, MLA_HEADS, MLA_NOPE + MLA_V)
    wuk = jnp.pad(wukv[:, :, :MLA_NOPE], ((0, 0), (0, 0), (0, HEAD_PAD - MLA_NOPE))).reshape(MLA_KV_RANK, -1)
    wuvt = wukv[:, :, MLA_NOPE:].reshape(MLA_KV_RANK, MLA_HEADS * MLA_V).T
    cos, sin = _rotary_tables(seq)
    ccs = jnp.pad(jnp.concatenate([cos, cos], axis=1), ((0, 0), (0, LANE - MLA_ROPE)))
    sn = jnp.pad(jnp.concatenate([-sin, sin], axis=1), ((0, 0), (0, LANE - MLA_ROPE)))

    tm = _pick_tile(seq, 512)
    qt, k, vt, qbt, kb, vbt = _attn_inproj(
        x2d, w_in0, w_qb.T.astype(_BF16), w_vb.T.astype(_BF16), row(l0_q_norm), wuqt.astype(_BF16),
        row(l0_kv_norm), wuk.astype(_BF16), wuvt.astype(_BF16), cos.T, sin.T, ccs, sn, bsz, seq, tm)
    nstream = 8 if seq % 4096 == 0 else (2 if seq % 1024 == 0 else 1)
    ot = _mla_flash(qt, k, vt, _pick_tile(seq // nstream, 512), _pick_tile(seq // 2, 512), nstream)
    kpad = jnp.pad(kb, ((0, 0), (BLOCK, BLOCK), (0, 0)))
    vtpad = jnp.pad(vbt, ((0, 0), (0, 0), (BLOCK, BLOCK)))
    obt = _swa(qbt, kpad, vtpad, _band_buckets().T, rel_bias.astype(_F32), l0_sinks.astype(_F32), bsz, seq,
               _pick_tile(seq // BLOCK, 8))
    he, cls = _outproj_ln_router(ot, obt, x2d, l0_w_out.astype(_BF16), row(l0_ln1_g), row(l0_ln1_b), rwt, rb,
                                 bsz, seq, tm)
    ea, eb, nvalid, off, order = _moe_plan(cls, n_tok)
    h = _moe(he, ea, eb, nvalid, off, order, l0_w1, l0_w3, l0_w2,
             row(l0_ln2_g), row(l0_ln2_b), n_tok, d)

    tt = _pick_tile(seq, 512)
    w_in1 = l1_w_in.astype(_BF16)
    w_gate, w_xr = w_in1[:, :d], w_in1[:, d:]
    cw = l1_conv_w.astype(_F32)
    cb = row(l1_conv_b)
    halfw = lambda w: (0.5 * w).astype(_BF16)
    halfb = lambda v: 0.5 * row(v)
    hf, xr = _rglru(h, cw, cb, halfw(l1_wa_f), halfb(l1_ba_f), halfw(l1_wx_f), halfb(l1_bx_f),
                    row(l1_lam_f), bsz, seq, tt, reverse=False, wxr=w_xr)
    he, cls = _rglru(xr, cw, cb, halfw(l1_wa_b), halfb(l1_ba_b), halfw(l1_wx_b), halfb(l1_bx_b),
                     row(l1_lam_b), bsz, seq, tt, reverse=True,
                     tail=(hf, w_gate, h, l1_w_out.astype(_BF16), row(l1_ln1_g), row(l1_ln1_b), rwt, rb))
    ea, eb, nvalid, off, order = _moe_plan(cls, n_tok)
    h = _moe(he, ea, eb, nvalid, off, order, l1_w1, l1_w3, l1_w2,
             row(l1_ln2_g), row(l1_ln2_b), n_tok, d)
    return h.reshape(bsz, seq, d)
```

```python
import functools
import math

import jax
import jax.numpy as jnp
import numpy as np
from jax import lax
from jax.experimental import pallas as pl
from jax.experimental.pallas import tpu as pltpu

MLA_HEADS = 8
MLA_Q_RANK = 256
MLA_KV_RANK = 128
MLA_NOPE = 64
MLA_ROPE = 32
MLA_V = 64
MLA_QK = MLA_NOPE + MLA_ROPE
ROPE_THETA = 10000.0
SWA_HEADS = 8
SWA_KV_HEADS = 2
SWA_REP = SWA_HEADS // SWA_KV_HEADS
SWA_D = 64
WINDOW = 128
BLOCK = 128
REL_BUCKETS = 32
REL_MAX_DIST = 128
LRU_BLOCKS = 8
LRU_BW = 128
LRU_C = 8.0
CONV_LEFT = 2
N_EXPERTS = 16
N_GROUPS = 4
EXPERTS_PER_GROUP = 4
N_CLASSES = N_GROUPS * 6
DEPTH = 2
ALPHA = (2.0 * DEPTH) ** 0.25
LN_EPS = 1e-5
RMS_EPS = 1e-6
NEG_BIG = -1e30

LANE = 128
HEAD_PAD = 128
GATE_LANES = 128
VT_ROWS = 80
_LOG2E = math.log2(math.e)
MOE_TM = 256

_F32 = jnp.float32
_BF16 = jnp.bfloat16
_NT_DIMS = (((1,), (1,)), ((), ()))
_TN_DIMS = (((0,), (0,)), ((), ()))


def _dot(a, b):
    return jnp.dot(a, b, preferred_element_type=_F32)


def _dot_nt(a, b):
    return lax.dot_general(a, b, _NT_DIMS, preferred_element_type=_F32)


def _sigmoid(x):
    return 0.5 * jnp.tanh(0.5 * x) + 0.5


def _rms(x, g):
    return x * lax.rsqrt(jnp.mean(jnp.square(x), -1, keepdims=True) + RMS_EPS) * g


def _layer_norm(x, g, b):
    mu = jnp.mean(x, -1, keepdims=True)
    xc = x - mu
    var = jnp.mean(jnp.square(xc), -1, keepdims=True)
    return xc * lax.rsqrt(var + LN_EPS) * g + b


def _attn_inproj_kernel(x_ref, w_ref, wqbt_ref, wvbt_ref, qn_ref, wuqt_ref, kvn_ref, wuk_ref, wuvt_ref,
                        cost_ref, sint_ref, ccs_ref, sn_ref,
                        qt_ref, k_ref, vt_ref, qbt_ref, kb_ref, vbt_ref):
    xb = x_ref[...].astype(_BF16)
    proj = _dot(xb, w_ref[...])
    q_lat = proj[:, 0:256]
    kv_lat = proj[:, 256:384]
    kr = proj[:, 384:512]
    qn = _rms(q_lat, qn_ref[...]).astype(_BF16)
    qt = _dot_nt(wuqt_ref[...], qn) * (MLA_QK ** -0.5 * _LOG2E)
    cos_t = cost_ref[...]
    sin_t = sint_ref[...]
    tm = qt.shape[1]
    zpad = jnp.zeros((HEAD_PAD - MLA_QK, tm), _F32)
    for h in range(MLA_HEADS):
        r0 = h * HEAD_PAD
        x1 = qt[r0 + 64:r0 + 80, :]
        x2 = qt[r0 + 80:r0 + 96, :]
        blk = jnp.concatenate([qt[r0:r0 + 64, :], x1 * cos_t - x2 * sin_t, x1 * sin_t + x2 * cos_t, zpad], axis=0)
        qt_ref[0, r0:r0 + HEAD_PAD, :] = blk.astype(_BF16)
    kvn = _rms(kv_lat, kvn_ref[...]).astype(_BF16)
    kn = _dot(kvn, wuk_ref[...])
    lane = lax.broadcasted_iota(jnp.int32, kr.shape, 1)
    swapped = jnp.where(lane < 16, pltpu.roll(kr, 112, 1), pltpu.roll(kr, 16, 1))
    kpe = kr * ccs_ref[...] + swapped * sn_ref[...]
    kpe = pltpu.roll(kpe, 64, 1)
    k_ref[0] = (kn + jnp.concatenate([kpe] * MLA_HEADS, axis=1)).astype(_BF16)
    vt = _dot_nt(wuvt_ref[...], kvn).astype(_BF16)
    extra = (lax.broadcasted_iota(jnp.int32, (VT_ROWS - MLA_V, tm), 0) == 0).astype(_BF16)
    for h in range(MLA_HEADS):
        vt_ref[0, h * VT_ROWS:h * VT_ROWS + MLA_V, :] = vt[h * MLA_V:(h + 1) * MLA_V, :]
        vt_ref[0, h * VT_ROWS + MLA_V:(h + 1) * VT_ROWS, :] = extra
    kb_ref[0] = proj[:, 512:640].astype(_BF16)
    vbt_ref[0] = _dot_nt(wvbt_ref[...], xb).astype(_BF16)
    qbt = (_dot_nt(wqbt_ref[...], xb) * (SWA_D ** -0.5)).astype(_BF16)
    zhalf = jnp.zeros((SWA_D, tm), _BF16)
    for h in range(SWA_HEADS):
        g = h // SWA_REP
        real = qbt[h * SWA_D:(h + 1) * SWA_D, :]
        qbt_ref[0, h * LANE + g * SWA_D:h * LANE + (g + 1) * SWA_D, :] = real
        qbt_ref[0, h * LANE + (1 - g) * SWA_D:h * LANE + (2 - g) * SWA_D, :] = zhalf


def _attn_inproj(x2d, w, wqbt, wvbt, qn, wuqt, kvn, wuk, wuvt, cos_t, sin_t, ccs, sn, bsz, seq, tm):
    nst = seq // tm
    const = lambda shape: pl.BlockSpec(shape, lambda b, i: (0,) * len(shape))
    out_shape = (
        jax.ShapeDtypeStruct((bsz, MLA_HEADS * HEAD_PAD, seq), _BF16),
        jax.ShapeDtypeStruct((bsz, seq, MLA_HEADS * HEAD_PAD), _BF16),
        jax.ShapeDtypeStruct((bsz, MLA_HEADS * VT_ROWS, seq), _BF16),
        jax.ShapeDtypeStruct((bsz, SWA_HEADS * LANE, seq), _BF16),
        jax.ShapeDtypeStruct((bsz, seq, SWA_KV_HEADS * SWA_D), _BF16),
        jax.ShapeDtypeStruct((bsz, SWA_KV_HEADS * SWA_D, seq), _BF16),
    )
    return pl.pallas_call(
        _attn_inproj_kernel,
        out_shape=out_shape,
        grid=(bsz, nst),
        in_specs=[
            pl.BlockSpec((tm, x2d.shape[1]), lambda b, i: (b * nst + i, 0)),
            const(w.shape), const(wqbt.shape), const(wvbt.shape), const(qn.shape), const(wuqt.shape),
            const(kvn.shape), const(wuk.shape), const(wuvt.shape),
            pl.BlockSpec((16, tm), lambda b, i: (0, i)),
            pl.BlockSpec((16, tm), lambda b, i: (0, i)),
            pl.BlockSpec((tm, LANE), lambda b, i: (i, 0)),
            pl.BlockSpec((tm, LANE), lambda b, i: (i, 0)),
        ],
        out_specs=(
            pl.BlockSpec((1, MLA_HEADS * HEAD_PAD, tm), lambda b, i: (b, 0, i)),
            pl.BlockSpec((1, tm, MLA_HEADS * HEAD_PAD), lambda b, i: (b, i, 0)),
            pl.BlockSpec((1, MLA_HEADS * VT_ROWS, tm), lambda b, i: (b, 0, i)),
            pl.BlockSpec((1, SWA_HEADS * LANE, tm), lambda b, i: (b, 0, i)),
            pl.BlockSpec((1, tm, SWA_KV_HEADS * SWA_D), lambda b, i: (b, i, 0)),
            pl.BlockSpec((1, SWA_KV_HEADS * SWA_D, tm), lambda b, i: (b, 0, i)),
        ),
        name="attn_inproj",
    )(x2d, w, wqbt, wvbt, qn, wuqt, kvn, wuk, wuvt, cos_t, sin_t, ccs, sn)


def _mla_flash_kernel(qt_ref, k_ref, vt_ref, ot_ref, *scratch, tk, nstream):
    st_sc = scratch[:2 * nstream]
    p_sc = scratch[2 * nstream:4 * nstream]
    tq = qt_ref.shape[2] // nstream
    nkv = k_ref.shape[1] // tk
    assert nkv % 2 == 0
    qts = [qt_ref[0, :, s * tq:(s + 1) * tq] for s in range(nstream)]

    def keys(j):
        return k_ref[0, pl.ds(pl.multiple_of(j * tk, tk), tk), :]

    def values(j):
        return vt_ref[0, :, pl.ds(pl.multiple_of(j * tk, tk), tk)]

    def phase(j, cur, state):
        nxt = 1 - cur
        kt = keys(jnp.minimum(j + 1, nkv - 1))
        vt = values(jnp.maximum(j - 1, 0))
        out = []
        for s in range(nstream):
            m, acc = state[s]
            st_sc[2 * s + nxt][...] = _dot(kt, qts[s])
            pv = _dot(vt, p_sc[2 * s + nxt][...])
            st = st_sc[2 * s + cur][...]
            m_new = jnp.maximum(m, jnp.max(st, axis=0, keepdims=True))
            alpha = jnp.exp2(m - m_new)
            p_sc[2 * s + cur][...] = jnp.exp2(st - m_new).astype(_BF16)
            out.append((m_new, (acc + pv) * alpha))
        return out

    def body(jj, state):
        state = phase(2 * jj, 0, state)
        return phase(2 * jj + 1, 1, state)

    k0 = keys(0)
    state = []
    for s in range(nstream):
        st_sc[2 * s][...] = _dot(k0, qts[s])
        p_sc[2 * s + 1][...] = jnp.zeros((tk, tq), _BF16)
        state.append((jnp.full((1, tq), -jnp.inf, _F32), jnp.zeros((VT_ROWS, tq), _F32)))
    state = lax.fori_loop(0, nkv // 2, body, state)
    v_last = values(nkv - 1)
    for s in range(nstream):
        acc = state[s][1] + _dot(v_last, p_sc[2 * s + 1][...])
        ot_ref[0, :, s * tq:(s + 1) * tq] = (acc[0:MLA_V, :] / acc[MLA_V:MLA_V + 1, :]).astype(ot_ref.dtype)


def _mla_flash(qt, k, vt, tq, tk, nstream):
    bsz, _, seq = qt.shape
    tqs = tq * nstream
    scratch = [pltpu.VMEM((tk, tq), _F32)] * (2 * nstream) + [pltpu.VMEM((tk, tq), _BF16)] * (2 * nstream)
    return pl.pallas_call(
        functools.partial(_mla_flash_kernel, tk=tk, nstream=nstream),
        out_shape=jax.ShapeDtypeStruct((bsz, MLA_HEADS * MLA_V, seq), _BF16),
        grid=(bsz, MLA_HEADS, seq // tqs),
        in_specs=[
            pl.BlockSpec((1, HEAD_PAD, tqs), lambda b, h, i: (b, h, i)),
            pl.BlockSpec((1, seq, HEAD_PAD), lambda b, h, i: (b, 0, h)),
            pl.BlockSpec((1, VT_ROWS, seq), lambda b, h, i: (b, h, 0)),
        ],
        out_specs=pl.BlockSpec((1, MLA_V, tqs), lambda b, h, i: (b, h, i)),
        scratch_shapes=scratch,
        name="mla_flash",
    )(qt, k, vt)


def _swa_kernel(relb_ref, sinks_ref, qt_ref, k_ref, vt_ref, bucket_ref, o_ref, bias_sc, *, nblk, seq):
    first = jnp.logical_and(pl.program_id(0) == 0, pl.program_id(1) == 0)

    @pl.when(first)
    def _build_bias():
        bucket = bucket_ref[...]
        for h in range(SWA_HEADS):
            acc = jnp.full(bucket.shape, NEG_BIG, _F32)
            for bk in range(REL_BUCKETS):
                acc = jnp.where(bucket == bk, relb_ref[bk, h], acc)
            bias_sc[h] = acc

    j = pl.program_id(1)
    krow = lax.broadcasted_iota(jnp.int32, (3 * BLOCK, 1), 0)

    def block(u, c):
        n = j * nblk + u
        c0 = pl.multiple_of(u * BLOCK, BLOCK)
        w0 = pl.multiple_of(n * BLOCK, BLOCK)
        key_pos = n * BLOCK - BLOCK + krow
        emask = jnp.where(jnp.logical_and(key_pos >= 0, key_pos < seq), 0.0, NEG_BIG).astype(_F32)
        kw = k_ref[0, pl.ds(w0, 3 * BLOCK), :]
        vw = vt_ref[0, :, pl.ds(w0, 3 * BLOCK)]
        for g in range(SWA_KV_HEADS):
            heads = range(g * SWA_REP, (g + 1) * SWA_REP)
            qs = jnp.concatenate([qt_ref[0, h * LANE:(h + 1) * LANE, pl.ds(c0, BLOCK)] for h in heads], axis=1)
            bias = jnp.concatenate([bias_sc[h] for h in heads], axis=1)
            sink = jnp.concatenate([jnp.full((1, BLOCK), sinks_ref[h], _F32) for h in heads], axis=1)
            st = _dot(kw, qs) + bias + emask
            m = jnp.maximum(jnp.max(st, axis=0, keepdims=True), sink)
            p = jnp.exp(st - m)
            den = jnp.sum(p, axis=0, keepdims=True) + jnp.exp(sink - m)
            ot = _dot(vw[g * SWA_D:(g + 1) * SWA_D, :], p.astype(_BF16)) / den
            for r, h in enumerate(heads):
                o_ref[0, h * SWA_D:(h + 1) * SWA_D, pl.ds(c0, BLOCK)] = ot[:, r * BLOCK:(r + 1) * BLOCK].astype(o_ref.dtype)
        return c

    def pair(t, c):
        block(2 * t, c)
        return block(2 * t + 1, c)

    if nblk % 2 == 0:
        lax.fori_loop(0, nblk // 2, pair, 0)
    else:
        lax.fori_loop(0, nblk, block, 0)


def _swa(qbt, kpad, vtpad, bucket_t, rel_bias, sinks, bsz, seq, nblk):
    nsteps = seq // (nblk * BLOCK)
    cols = nblk * BLOCK
    return pl.pallas_call(
        functools.partial(_swa_kernel, nblk=nblk, seq=seq),
        out_shape=jax.ShapeDtypeStruct((bsz, SWA_HEADS * SWA_D, seq), _BF16),
        grid=(bsz, nsteps),
        in_specs=[
            pl.BlockSpec(memory_space=pltpu.SMEM),
            pl.BlockSpec(memory_space=pltpu.SMEM),
            pl.BlockSpec((1, SWA_HEADS * LANE, cols), lambda b, j: (b, 0, j)),
            pl.BlockSpec((1, seq + 2 * BLOCK, SWA_KV_HEADS * SWA_D), lambda b, j: (b, 0, 0)),
            pl.BlockSpec((1, SWA_KV_HEADS * SWA_D, seq + 2 * BLOCK), lambda b, j: (b, 0, 0)),
            pl.BlockSpec((3 * BLOCK, BLOCK), lambda b, j: (0, 0)),
        ],
        out_specs=pl.BlockSpec((1, SWA_HEADS * SWA_D, cols), lambda b, j: (b, 0, j)),
        scratch_shapes=[pltpu.VMEM((SWA_HEADS, 3 * BLOCK, BLOCK), _F32)],
        compiler_params=pltpu.CompilerParams(dimension_semantics=("arbitrary", "arbitrary")),
        name="swa",
    )(rel_bias, sinks, qbt, kpad, vtpad, bucket_t)


def _route(logits_t, rbias):
    sc = jax.nn.sigmoid(logits_t)
    bz = sc + rbias
    s_rows = [sc[e:e + 1, :] for e in range(N_EXPERTS)]
    b_rows = [bz[e:e + 1, :] for e in range(N_EXPERTS)]
    gsel = None
    best = None
    for g in range(N_GROUPS):
        r = b_rows[4 * g:4 * g + 4]
        gs = r[0] + r[1]
        for (i, k) in ((0, 2), (0, 3), (1, 2), (1, 3), (2, 3)):
            gs = jnp.maximum(gs, r[i] + r[k])
        if g == 0:
            gsel = jnp.zeros(gs.shape, jnp.int32)
            best = gs
        else:
            better = gs > best
            gsel = jnp.where(better, g, gsel)
            best = jnp.where(better, gs, best)

    def pick(rows, k):
        out = rows[12 + k]
        for g in (2, 1, 0):
            out = jnp.where(gsel == g, rows[4 * g + k], out)
        return out

    v = [pick(b_rows, k) for k in range(4)]
    s = [pick(s_rows, k) for k in range(4)]
    i1 = jnp.zeros(gsel.shape, jnp.int32)
    m1 = v[0]
    w1 = s[0]
    for k in range(1, 4):
        gt = v[k] > m1
        i1 = jnp.where(gt, k, i1)
        m1 = jnp.where(gt, v[k], m1)
        w1 = jnp.where(gt, s[k], w1)
    i2 = jnp.full(gsel.shape, -1, jnp.int32)
    m2 = jnp.full(m1.shape, -jnp.inf, _F32)
    w2 = jnp.zeros(m1.shape, _F32)
    for k in range(4):
        ok = jnp.logical_and(i1 != k, jnp.logical_or(i2 < 0, v[k] > m2))
        i2 = jnp.where(ok, k, i2)
        m2 = jnp.where(ok, v[k], m2)
        w2 = jnp.where(ok, s[k], w2)
    tot = w1 + w2
    g1 = w1 / tot
    g2 = w2 / tot
    first_lo = i1 < i2
    lo = jnp.where(first_lo, i1, i2)
    hi = jnp.where(first_lo, i2, i1)
    pair = jnp.where(lo == 0, hi - 1, jnp.where(lo == 1, hi + 1, 5))
    cls = gsel * 6 + pair
    return cls, jnp.where(first_lo, g1, g2), jnp.where(first_lo, g2, g1)


def _ln_router_store(z, g_ref, b_ref, rwt_ref, rb_ref, h_ref, cls_ref):
    h = _layer_norm(z, g_ref[...], b_ref[...])
    tm = h.shape[0]
    h_hi = h.astype(_BF16)
    h_lo = (h - h_hi.astype(_F32)).astype(_BF16)
    rw = rwt_ref[...]
    rw_hi = rw.astype(_BF16)
    rw_lo = (rw - rw_hi.astype(_F32)).astype(_BF16)
    part = _dot_nt(jnp.concatenate([rw_hi, rw_lo], axis=0), h_hi)
    logits_t = part[0:N_EXPERTS] + part[N_EXPERTS:2 * N_EXPERTS] + _dot_nt(rw_hi, h_lo)
    cls, g_lo, g_hi = _route(logits_t, rb_ref[...])
    rows = jnp.concatenate([g_lo, g_hi, jnp.zeros((GATE_LANES - 2, tm), _F32)], axis=0)
    d = h.shape[1]
    h_ref[:, 0:d] = h
    h_ref[:, d:d + GATE_LANES] = rows.T
    cls_ref[0] = cls


def _outproj_ln_router_kernel(ot_ref, ob_ref, x_ref, w_ref, g_ref, b_ref, rwt_ref, rb_ref, h_ref, cls_ref):
    heads_t = jnp.concatenate([ot_ref[0], ob_ref[0]], axis=0)
    mixed = lax.dot_general(heads_t, w_ref[...], _TN_DIMS, preferred_element_type=_F32)
    z = ALPHA * x_ref[...] + mixed
    _ln_router_store(z, g_ref, b_ref, rwt_ref, rb_ref, h_ref, cls_ref)


def _outproj_ln_router(ot, ob, x2d, w, g, b, rwt, rb, bsz, seq, tm):
    nst = seq // tm
    d = x2d.shape[1]
    const = lambda shape: pl.BlockSpec(shape, lambda bb, i: (0,) * len(shape))
    return pl.pallas_call(
        _outproj_ln_router_kernel,
        out_shape=(jax.ShapeDtypeStruct((bsz * seq, d + GATE_LANES), _F32),
                   jax.ShapeDtypeStruct((bsz * nst, 1, tm), jnp.int32)),
        grid=(bsz, nst),
        in_specs=[
            pl.BlockSpec((1, ot.shape[1], tm), lambda bb, i: (bb, 0, i)),
            pl.BlockSpec((1, ob.shape[1], tm), lambda bb, i: (bb, 0, i)),
            pl.BlockSpec((tm, d), lambda bb, i: (bb * nst + i, 0)),
            const(w.shape), const(g.shape), const(b.shape), const(rwt.shape), const(rb.shape),
        ],
        out_specs=(pl.BlockSpec((tm, d + GATE_LANES), lambda bb, i: (bb * nst + i, 0)),
                   pl.BlockSpec((1, 1, tm), lambda bb, i: (bb * nst + i, 0, 0))),
        name="outproj_ln_router",
    )(ot, ob, x2d, w, g, b, rwt, rb)


_SUBLANES = 8
_SCAN_RADIX = 4
_DMA_CHUNK = 32


def _moe_kernel(ea_ref, eb_ref, nv_ref, off_ref, src_ref, nsrc_ref, h_hbm, w1a, w3a, w2a, w1b, w3b, w2b, g_ref, b_ref,
                out_hbm, xbuf, obuf, w13a, w2a_bf, w13b, w2b_bf, sem_in, sem_out):
    i = pl.program_id(0)
    ntiles = pl.num_programs(0)
    tm = xbuf.shape[1] * _SUBLANES
    d = obuf.shape[2]
    nv = nv_ref[i]
    nv_prev = jnp.where(i > 0, nv_ref[jnp.maximum(i - 1, 0)], 0)
    nv_next = jnp.where(i + 1 < ntiles, nv_ref[jnp.minimum(i + 1, ntiles - 1)], 0)
    slot = lax.rem(i, 2)
    shift = lax.rem(off_ref[i], LANE)
    shift_next = lax.rem(off_ref[jnp.minimum(i + 1, ntiles - 1)], LANE)

    def hbm_row(ref, idx):
        return ref.at[lax.shift_right_logical(idx, 3), jnp.bitwise_and(idx, _SUBLANES - 1)]

    def gather_start(idx_ref, sh, s):
        for r in range(tm):
            idx = idx_ref[sh + r]
            pltpu.make_async_copy(hbm_row(h_hbm, idx), xbuf.at[s, r // _SUBLANES, r % _SUBLANES], sem_in.at[s]).start()

    def scatter_wait(n):
        for bit in range(tm.bit_length()):
            rows = 1 << bit

            @pl.when(jnp.bitwise_and(n, rows) != 0)
            def _():
                if rows >= _SUBLANES:
                    grp = pl.ds(0, rows // _SUBLANES)
                    pltpu.make_async_copy(obuf.at[grp], out_hbm.at[grp], sem_out).wait()
                else:
                    pltpu.make_async_copy(obuf.at[0, pl.ds(0, rows)], out_hbm.at[0, pl.ds(0, rows)], sem_out).wait()

    @pl.when(jnp.logical_and(i == 0, nv > 0))
    def _prologue():
        gather_start(src_ref, shift, 0)

    @pl.when(nv_next > 0)
    def _prefetch():
        gather_start(nsrc_ref, shift_next, 1 - slot)

    prev_i = jnp.maximum(i - 1, 0)
    dff = w2a.shape[0]

    def refresh(w1, w3, w2, w13_sc, w2_sc):
        w13_sc[:, 0:dff] = w1[...].astype(_BF16)
        w13_sc[:, dff:2 * dff] = w3[...].astype(_BF16)
        w2_sc[...] = w2[...].astype(_BF16)

    @pl.when(jnp.logical_and(nv > 0, jnp.logical_or(i == 0, ea_ref[i] != ea_ref[prev_i])))
    def _refresh_a():
        refresh(w1a, w3a, w2a, w13a, w2a_bf)

    @pl.when(jnp.logical_and(nv > 0, jnp.logical_or(i == 0, eb_ref[i] != eb_ref[prev_i])))
    def _refresh_b():
        refresh(w1b, w3b, w2b, w13b, w2b_bf)

    @pl.when(nv > 0)
    def _tile():
        pltpu.make_async_copy(h_hbm.at[pl.ds(0, tm // _SUBLANES)], xbuf.at[slot], sem_in.at[slot]).wait()
        def compute(rows):
            xg = xbuf[slot, 0:rows // _SUBLANES].reshape(rows, d + GATE_LANES)
            x = xg[:, 0:d]
            gates = xg[:, d:d + GATE_LANES]
            ga = gates[:, 0:1]
            gb = gates[:, 1:2]
            xb = x.astype(_BF16)

            def expert(w13, w2):
                uv = _dot(xb, w13[...])
                u = uv[:, 0:dff]
                hh = u * _sigmoid(u) * uv[:, dff:2 * dff]
                return _dot(hh.astype(_BF16), w2[...])

            y = ga * expert(w13a, w2a_bf) + gb * expert(w13b, w2b_bf)
            z = _layer_norm(ALPHA * x + y, g_ref[...], b_ref[...])

            @pl.when(nv_prev > 0)
            def _drain_prev():
                scatter_wait(nv_prev)

            obuf[0:rows // _SUBLANES] = z.reshape(rows // _SUBLANES, _SUBLANES, d)

        @pl.when(nv > tm // 2)
        def _full():
            compute(tm)

        @pl.when(nv <= tm // 2)
        def _half():
            compute(tm // 2)

        for c in range(tm // _DMA_CHUNK):
            @pl.when(nv >= (c + 1) * _DMA_CHUNK)
            def _chunk():
                for r in range(c * _DMA_CHUNK, (c + 1) * _DMA_CHUNK):
                    idx = src_ref[shift + r]
                    pltpu.make_async_copy(obuf.at[r // _SUBLANES, r % _SUBLANES], hbm_row(out_hbm, idx),
                                          sem_out).start()

        def group(gi, c):
            for u in range(_SUBLANES):
                idx = src_ref[shift + gi * _SUBLANES + u]
                pltpu.make_async_copy(obuf.at[gi, u], hbm_row(out_hbm, idx), sem_out).start()
            return c

        def single(r, c):
            idx = src_ref[shift + r]
            pltpu.make_async_copy(hbm_row(obuf, r), hbm_row(out_hbm, idx), sem_out).start()
            return c

        ngroups = lax.shift_right_logical(nv, 3)
        lax.fori_loop((nv // _DMA_CHUNK) * (_DMA_CHUNK // _SUBLANES), ngroups, group, 0)
        lax.fori_loop(ngroups * _SUBLANES, nv, single, 0)

        @pl.when(nv_next == 0)
        def _drain_last():
            scatter_wait(nv)


def _moe(he, ea, eb, nvalid, off, order, w1, w3, w2, g, b, n_tokens, d):
    ntiles = ea.shape[0]
    tm = MOE_TM
    dff = w2.shape[1]
    wspec_a = lambda shape: pl.BlockSpec((None,) + shape, lambda i, ea_r, eb_r, nv_r, off_r: (ea_r[i], 0, 0))
    wspec_b = lambda shape: pl.BlockSpec((None,) + shape, lambda i, ea_r, eb_r, nv_r, off_r: (eb_r[i], 0, 0))

    def window(nxt):
        def start(i, ea_r, eb_r, nv_r, off_r):
            o = off_r[jnp.minimum(i + nxt, ntiles - 1)]
            return (pl.multiple_of((o // LANE) * LANE, LANE),)

        return pl.BlockSpec((pl.Element(2 * tm),), start, memory_space=pltpu.SMEM)

    grid_spec = pltpu.PrefetchScalarGridSpec(
        num_scalar_prefetch=4,
        grid=(ntiles,),
        in_specs=[
            window(0), window(1),
            pl.BlockSpec(memory_space=pl.ANY),
            wspec_a((d, dff)), wspec_a((d, dff)), wspec_a((dff, d)),
            wspec_b((d, dff)), wspec_b((d, dff)), wspec_b((dff, d)),
            pl.BlockSpec((1, d), lambda i, *_: (0, 0)),
            pl.BlockSpec((1, d), lambda i, *_: (0, 0)),
        ],
        out_specs=pl.BlockSpec(memory_space=pl.ANY),
        scratch_shapes=[
            pltpu.VMEM((2, tm // _SUBLANES, _SUBLANES, d + GATE_LANES), _F32),
            pltpu.VMEM((tm // _SUBLANES, _SUBLANES, d), _F32),
            pltpu.VMEM((d, 2 * dff), _BF16), pltpu.VMEM((dff, d), _BF16),
            pltpu.VMEM((d, 2 * dff), _BF16), pltpu.VMEM((dff, d), _BF16),
            pltpu.SemaphoreType.DMA((2,)),
            pltpu.SemaphoreType.DMA,
        ],
    )
    he3 = he.reshape(n_tokens // _SUBLANES, _SUBLANES, d + GATE_LANES)
    out = pl.pallas_call(
        _moe_kernel,
        out_shape=jax.ShapeDtypeStruct((n_tokens // _SUBLANES, _SUBLANES, d), _F32),
        grid_spec=grid_spec,
        compiler_params=pltpu.CompilerParams(dimension_semantics=("arbitrary",)),
        name="moe",
    )(ea, eb, nvalid, off, order, order, he3, w1, w3, w2, w1, w3, w2, g, b)
    return out.reshape(n_tokens, d)


def _moe_plan(cls, n_tokens):
    tm = MOE_TM
    ntiles = n_tokens // tm + N_CLASSES
    cls = cls.reshape(n_tokens)
    order = jnp.argsort(cls, stable=True).astype(jnp.int32)
    counts = jnp.sum((cls[:, None] == jnp.arange(N_CLASSES, dtype=jnp.int32)[None, :]).astype(jnp.int32), axis=0)
    cstart = jnp.cumsum(counts) - counts
    ptiles = (counts + tm - 1) // tm
    tend = jnp.cumsum(ptiles)
    tstart = tend - ptiles
    tile = jnp.arange(ntiles, dtype=jnp.int32)
    tcls = jnp.minimum(jnp.sum((tile[:, None] >= tend[None, :]).astype(jnp.int32), axis=1), N_CLASSES - 1)
    used = tile < tend[-1]
    nvalid = jnp.where(used, jnp.clip(counts[tcls] - (tile - tstart[tcls]) * tm, 0, tm), 0).astype(jnp.int32)
    off = jnp.where(used, cstart[tcls] + (tile - tstart[tcls]) * tm, 0).astype(jnp.int32)
    order = jnp.concatenate([order, jnp.zeros((2 * tm,), jnp.int32)])
    pair_lo = jnp.array([0, 0, 0, 1, 1, 2], jnp.int32)
    pair_hi = jnp.array([1, 2, 3, 2, 3, 3], jnp.int32)
    ea = (4 * (tcls // 6) + pair_lo[tcls % 6]).astype(jnp.int32)
    eb = (4 * (tcls // 6) + pair_hi[tcls % 6]).astype(jnp.int32)
    return ea, eb, nvalid, off, order


def _scan_slabs(a_refs, b_refs, c_refs, h_in, reverse):
    nlev = len(a_refs)
    ks = list(range(_SCAN_RADIX))
    if reverse:
        ks.reverse()
    for l in range(nlev - 1):
        grp = a_refs[l].shape[1] // _SCAN_RADIX
        p = q = None
        for k in ks:
            sl = pl.ds(k, grp, stride=_SCAN_RADIX)
            a = a_refs[l][:, sl, :]
            b = b_refs[l][:, sl, :]
            if p is None:
                p, q = a, b
            else:
                p, q = a * p, a * q + b
                a_refs[l][:, sl, :] = p
                b_refs[l][:, sl, :] = q
        a_refs[l + 1][...] = p
        b_refs[l + 1][...] = q
    rows = a_refs[-1].shape[1]
    a = a_refs[-1][...]
    b = b_refs[-1][...]
    out = [None] * rows
    h = h_in
    for r in (range(rows - 1, -1, -1) if reverse else range(rows)):
        h = a[:, r:r + 1, :] * h + b[:, r:r + 1, :]
        out[r] = h
    h_out = h
    b_refs[-1][...] = jnp.concatenate(out, axis=1)
    edge = jnp.broadcast_to(h_in, (h_in.shape[0], _SUBLANES, h_in.shape[2]))
    for l in range(nlev - 2, -1, -1):
        grp = a_refs[l].shape[1] // _SCAN_RADIX
        c = c_refs[l]
        c[:, _SUBLANES:_SUBLANES + grp, :] = b_refs[l + 1][...]
        if reverse:
            c[:, _SUBLANES + grp:2 * _SUBLANES + grp, :] = edge
            cin = c[:, _SUBLANES + 1:_SUBLANES + 1 + grp, :]
        else:
            c[:, 0:_SUBLANES, :] = edge
            cin = c[:, _SUBLANES - 1:_SUBLANES - 1 + grp, :]
        for k in ks:
            sl = pl.ds(k, grp, stride=_SCAN_RADIX)
            b_refs[l][:, sl, :] = a_refs[l][:, sl, :] * cin + b_refs[l][:, sl, :]
    return h_out


def _rglru_kernel(*refs, reverse, final, ntile):
    if final:
        (src_ref, prev_ref, next_ref, cw_ref, cb_ref, wa_ref, ba_ref, wx_ref, bx_ref, lam_ref,
         hf_ref, wgate_ref, res_ref, wout_ref, g_ref, b_ref, rwt_ref, rb_ref,
         h_ref, cls_ref, xe_sc, carry_sc, *scan_sc) = refs
    else:
        (src_ref, prev_ref, next_ref, cw_ref, cb_ref, wa_ref, ba_ref, wx_ref, bx_ref, lam_ref, wxr_ref,
         hout_ref, xr_out_ref, xe_sc, carry_sc, *scan_sc) = refs
    nlev = (len(scan_sc) + 1) // 3
    a_refs, b_refs, c_refs = scan_sc[:nlev], scan_sc[nlev:2 * nlev], scan_sc[2 * nlev:]
    i = pl.program_id(1)
    ti = (ntile - 1 - i) if reverse else i
    tt = src_ref.shape[0]

    @pl.when(i == 0)
    def _reset():
        carry_sc[...] = jnp.zeros_like(carry_sc)

    prev = jnp.where(ti > 0, prev_ref[...], 0.0)
    nxt = jnp.where(ti < ntile - 1, next_ref[...], 0.0)
    xin = jnp.concatenate([prev, src_ref[...], nxt], axis=0)
    if final:
        xe = xin
    else:
        xe = _dot(xin.astype(_BF16), wxr_ref[...])
        xr_out_ref[...] = xe[_SUBLANES:_SUBLANES + tt, :]
    for n in range(LRU_BLOCKS):
        xe_sc[n] = xe[:, n * LRU_BW:(n + 1) * LRU_BW]
    cw = cw_ref[...]
    cb = cb_ref[...]
    half_decay = (0.5 * LRU_C) * -jnp.log(1.0 + jnp.exp(-lam_ref[...]))
    for n in range(LRU_BLOCKS):
        sl = slice(n * LRU_BW, (n + 1) * LRU_BW)
        xcn = cb[:, sl]
        for k in range(cw.shape[0]):
            xcn = xcn + cw[k:k + 1, sl] * xe_sc[n, _SUBLANES - CONV_LEFT + k:_SUBLANES - CONV_LEFT + k + tt, :]
        xb = xcn.astype(_BF16)
        tr = jnp.tanh(_dot(xb, wa_ref[n]) + ba_ref[:, sl])
        ig = 0.5 * jnp.tanh(_dot(xb, wx_ref[n]) + bx_ref[:, sl]) + 0.5
        log_a = tr * half_decay[:, sl] + half_decay[:, sl]
        th = jnp.tanh(log_a)
        num = -2.0 * th
        scale = jnp.where(num > 0.0, num * lax.rsqrt(num * (1.0 - th)), 0.0)
        a_refs[0][n] = jnp.exp(log_a)
        b_refs[0][n] = scale * (ig * xcn)
    carry_sc[...] = _scan_slabs(a_refs, b_refs, c_refs, carry_sc[...], reverse)
    if final:
        gate = _dot(res_ref[...].astype(_BF16), wgate_ref[...])
    ys = []
    for n in range(LRU_BLOCKS):
        sl = slice(n * LRU_BW, (n + 1) * LRU_BW)
        h = b_refs[0][n]
        if final:
            ys.append((hf_ref[:, sl] + h) * jax.nn.gelu(gate[:, sl], approximate=True))
        else:
            hout_ref[:, sl] = h
    if final:
        y = jnp.concatenate(ys, axis=1).astype(_BF16)
        z = ALPHA * res_ref[...] + _dot(y, wout_ref[...])
        _ln_router_store(z, g_ref, b_ref, rwt_ref, rb_ref, h_ref, cls_ref)


def _rglru(src, cw, cb, wa, ba, wx, bx, lam, bsz, seq, tt, reverse, wxr=None, tail=None):
    ntile = seq // tt
    width = cw.shape[1]
    final = tail is not None
    tidx = (lambda i: ntile - 1 - i) if reverse else (lambda i: i)
    row = lambda b, i: (b * ntile + tidx(i), 0)
    g8 = tt // 8
    nb8 = seq // 8

    def prev_map(b, i):
        return (b * nb8 + jnp.maximum(tidx(i) * g8 - 1, 0), 0)

    def next_map(b, i):
        return (b * nb8 + jnp.minimum((tidx(i) + 1) * g8, nb8 - 1), 0)

    const = lambda shape: pl.BlockSpec(shape, lambda b, i: (0,) * len(shape))
    swidth = src.shape[1]
    in_specs = [
        pl.BlockSpec((tt, swidth), row),
        pl.BlockSpec((8, swidth), prev_map),
        pl.BlockSpec((8, swidth), next_map),
        const(cw.shape), const(cb.shape), const(wa.shape), const(ba.shape), const(wx.shape), const(bx.shape),
        const(lam.shape),
    ]
    args = [src, src, src, cw, cb, wa, ba, wx, bx, lam]
    rows = [tt]
    while rows[-1] > _SUBLANES:
        assert rows[-1] % _SCAN_RADIX == 0
        rows.append(rows[-1] // _SCAN_RADIX)
    assert rows[-1] == _SUBLANES
    slab = lambda r: pltpu.VMEM((LRU_BLOCKS, r, LRU_BW), _F32)
    scratch = [slab(tt + 2 * _SUBLANES), pltpu.VMEM((LRU_BLOCKS, 1, LRU_BW), _F32)]
    scratch += [slab(r) for r in rows] * 2 + [slab(r + 2 * _SUBLANES) for r in rows[1:]]
    if final:
        hf, wgate, res, wout, g, b, rwt, rb = tail
        d = res.shape[1]
        in_specs += [pl.BlockSpec((tt, width), row), const(wgate.shape), pl.BlockSpec((tt, d), row),
                     const(wout.shape), const(g.shape), const(b.shape), const(rwt.shape), const(rb.shape)]
        args += [hf, wgate, res, wout, g, b, rwt, rb]
        out_shape = (jax.ShapeDtypeStruct((bsz * seq, d + GATE_LANES), _F32),
                     jax.ShapeDtypeStruct((bsz * ntile, 1, tt), jnp.int32))
        out_specs = (pl.BlockSpec((tt, d + GATE_LANES), row),
                     pl.BlockSpec((1, 1, tt), lambda b, i: (b * ntile + tidx(i), 0, 0)))
    else:
        in_specs.append(const(wxr.shape))
        args.append(wxr)
        out_shape = (jax.ShapeDtypeStruct((bsz * seq, width), _F32),) * 2
        out_specs = (pl.BlockSpec((tt, width), row),) * 2
    return pl.pallas_call(
        functools.partial(_rglru_kernel, reverse=reverse, final=final, ntile=ntile),
        out_shape=out_shape,
        grid=(bsz, ntile),
        in_specs=in_specs,
        out_specs=out_specs,
        scratch_shapes=scratch,
        compiler_params=pltpu.CompilerParams(dimension_semantics=("arbitrary", "arbitrary")),
        name="rglru_bwd_tail" if final else "rglru_fwd",
    )(*args)


def _rotary_tables(seq):
    half = MLA_ROPE // 2
    inv_freq = ROPE_THETA ** (-jnp.arange(half, dtype=_F32) / half)
    ang = jnp.arange(seq, dtype=_F32)[:, None] * inv_freq[None, :]
    return jnp.cos(ang), jnp.sin(ang)


def _t5_bucket(rel):
    n_side = REL_BUCKETS // 2
    max_exact = n_side // 2
    dist = jnp.abs(rel)
    far = max_exact + (jnp.log(jnp.maximum(dist, 1).astype(_F32) / max_exact)
                       / math.log(REL_MAX_DIST / max_exact) * (n_side - max_exact)).astype(jnp.int32)
    far = jnp.minimum(far, n_side - 1)
    return jnp.where(rel > 0, n_side, 0) + jnp.where(dist < max_exact, dist, far)


def _band_buckets():
    rel = jnp.arange(3 * BLOCK)[None, :] - BLOCK - jnp.arange(BLOCK)[:, None]
    ids = jnp.where(jnp.abs(rel) <= WINDOW, _t5_bucket(rel), REL_BUCKETS).astype(jnp.int32)
    return jnp.bitwise_and(ids, 2 * REL_BUCKETS - 1)


def _pick_tile(seq, want):
    t = min(want, seq)
    while seq % t:
        t //= 2
    return t


def kernel(x, rel_bias, router_w, router_bias, l0_w_in, l0_q_norm, l0_w_uq, l0_kv_norm, l0_w_ukv, l0_sinks, l0_w_out, l0_ln1_g, l0_ln1_b, l0_w1, l0_w3, l0_w2, l0_ln2_g, l0_ln2_b, l1_w_in, l1_conv_w, l1_conv_b, l1_wa_f, l1_ba_f, l1_wx_f, l1_bx_f, l1_lam_f, l1_wa_b, l1_ba_b, l1_wx_b, l1_bx_b, l1_lam_b, l1_w_out, l1_ln1_g, l1_ln1_b, l1_w1, l1_w3, l1_w2, l1_ln2_g, l1_ln2_b):
    bsz, seq, d = x.shape
    n_tok = bsz * seq
    assert seq % BLOCK == 0 and d == LRU_BLOCKS * LRU_BW and n_tok % MOE_TM == 0
    x2d = x.reshape(n_tok, d)
    row = lambda v: v.reshape(1, -1).astype(_F32)
    rwt = router_w.astype(_F32).T
    rb = router_bias.astype(_F32).reshape(N_EXPERTS, 1)

    o = np.cumsum([0, MLA_Q_RANK, MLA_KV_RANK, MLA_ROPE, SWA_HEADS * SWA_D, SWA_KV_HEADS * SWA_D,
                   SWA_KV_HEADS * SWA_D])
    w_q, w_kv, w_kr, w_qb, w_kb, w_vb = [l0_w_in[:, o[i]:o[i + 1]] for i in range(6)]
    w_in0 = jnp.concatenate([w_q, w_kv, jnp.pad(w_kr, ((0, 0), (0, LANE - MLA_ROPE))), w_kb], axis=1).astype(_BF16)
    wuq = l0_w_uq.reshape(MLA_Q_RANK, MLA_HEADS, MLA_QK)
    wuqt = jnp.pad(wuq, ((0, 0), (0, 0), (0, HEAD_PAD - MLA_QK))).reshape(MLA_Q_RANK, MLA_HEADS * HEAD_PAD).T
    wukv = l0_w_ukv.reshape(MLA_KV_RANK, MLA_HEADS, MLA_NOPE + MLA_V)
    wuk = jnp.pad(wukv[:, :, :MLA_NOPE], ((0, 0), (0, 0), (0, HEAD_PAD - MLA_NOPE))).reshape(MLA_KV_RANK, -1)
    wuvt = wukv[:, :, MLA_NOPE:].reshape(MLA_KV_RANK, MLA_HEADS * MLA_V).T
    cos, sin = _rotary_tables(seq)
    ccs = jnp.pad(jnp.concatenate([cos, cos], axis=1), ((0, 0), (0, LANE - MLA_ROPE)))
    sn = jnp.pad(jnp.concatenate([-sin, sin], axis=1), ((0, 0), (0, LANE - MLA_ROPE)))

    tm = _pick_tile(seq, 512)
    qt, k, vt, qbt, kb, vbt = _attn_inproj(
        x2d, w_in0, w_qb.T.astype(_BF16), w_vb.T.astype(_BF16), row(l0_q_norm), wuqt.astype(_BF16),
        row(l0_kv_norm), wuk.astype(_BF16), wuvt.astype(_BF16), cos.T, sin.T, ccs, sn, bsz, seq, tm)
    nstream = 16 if seq % 4096 == 0 else (2 if seq % 1024 == 0 else 1)
    ot = _mla_flash(qt, k, vt, _pick_tile(seq // nstream, 256), _pick_tile(seq // 2, 512), nstream)
    kpad = jnp.pad(kb, ((0, 0), (BLOCK, BLOCK), (0, 0)))
    vtpad = jnp.pad(vbt, ((0, 0), (0, 0), (BLOCK, BLOCK)))
    obt = _swa(qbt, kpad, vtpad, _band_buckets().T, rel_bias.astype(_F32), l0_sinks.astype(_F32), bsz, seq,
               _pick_tile(seq // BLOCK, 8))
    he, cls = _outproj_ln_router(ot, obt, x2d, l0_w_out.astype(_BF16), row(l0_ln1_g), row(l0_ln1_b), rwt, rb,
                                 bsz, seq, tm)
    ea, eb, nvalid, off, order = _moe_plan(cls, n_tok)
    h = _moe(he, ea, eb, nvalid, off, order, l0_w1, l0_w3, l0_w2,
             row(l0_ln2_g), row(l0_ln2_b), n_tok, d)

    tt = _pick_tile(seq, 512)
    w_in1 = l1_w_in.astype(_BF16)
    w_gate, w_xr = w_in1[:, :d], w_in1[:, d:]
    cw = l1_conv_w.astype(_F32)
    cb = row(l1_conv_b)
    halfw = lambda w: (0.5 * w).astype(_BF16)
    halfb = lambda v: 0.5 * row(v)
    hf, xr = _rglru(h, cw, cb, halfw(l1_wa_f), halfb(l1_ba_f), halfw(l1_wx_f), halfb(l1_bx_f),
                    row(l1_lam_f), bsz, seq, tt, reverse=False, wxr=w_xr)
    he, cls = _rglru(xr, cw, cb, halfw(l1_wa_b), halfb(l1_ba_b), halfw(l1_wx_b), halfb(l1_bx_b),
                     row(l1_lam_b), bsz, seq, tt, reverse=True,
                     tail=(hf, w_gate, h, l1_w_out.astype(_BF16), row(l1_ln1_g), row(l1_ln1_b), rwt, rb))
    ea, eb, nvalid, off, order = _moe_plan(cls, n_tok)
    h = _moe(he, ea, eb, nvalid, off, order, l1_w1, l1_w3, l1_w2,
             row(l1_ln2_g), row(l1_ln2_b), n_tok, d)
    return h.reshape(bsz, seq, d)
```

```python
import functools
import math

import jax
import jax.numpy as jnp
import numpy as np
from jax import lax
from jax.experimental import pallas as pl
from jax.experimental.pallas import tpu as pltpu

MLA_HEADS = 8
MLA_Q_RANK = 256
MLA_KV_RANK = 128
MLA_NOPE = 64
MLA_ROPE = 32
MLA_V = 64
MLA_QK = MLA_NOPE + MLA_ROPE
ROPE_THETA = 10000.0
SWA_HEADS = 8
SWA_KV_HEADS = 2
SWA_REP = SWA_HEADS // SWA_KV_HEADS
SWA_D = 64
WINDOW = 128
BLOCK = 128
REL_BUCKETS = 32
REL_MAX_DIST = 128
LRU_BLOCKS = 8
LRU_BW = 128
LRU_C = 8.0
CONV_LEFT = 2
N_EXPERTS = 16
N_GROUPS = 4
EXPERTS_PER_GROUP = 4
N_CLASSES = N_GROUPS * 6
DEPTH = 2
ALPHA = (2.0 * DEPTH) ** 0.25
LN_EPS = 1e-5
RMS_EPS = 1e-6
NEG_BIG = -1e30

LANE = 128
HEAD_PAD = 128
GATE_LANES = 128
VT_ROWS = 80
_LOG2E = math.log2(math.e)
MOE_TM = 256

_F32 = jnp.float32
_BF16 = jnp.bfloat16
_NT_DIMS = (((1,), (1,)), ((), ()))
_TN_DIMS = (((0,), (0,)), ((), ()))


def _dot(a, b):
    return jnp.dot(a, b, preferred_element_type=_F32)


def _dot_nt(a, b):
    return lax.dot_general(a, b, _NT_DIMS, preferred_element_type=_F32)


def _sigmoid(x):
    return 0.5 * jnp.tanh(0.5 * x) + 0.5


def _rms(x, g):
    return x * lax.rsqrt(jnp.mean(jnp.square(x), -1, keepdims=True) + RMS_EPS) * g


def _layer_norm(x, g, b):
    mu = jnp.mean(x, -1, keepdims=True)
    xc = x - mu
    var = jnp.mean(jnp.square(xc), -1, keepdims=True)
    return xc * lax.rsqrt(var + LN_EPS) * g + b


def _attn_inproj_kernel(x_ref, w_ref, wqbt_ref, wvbt_ref, qn_ref, wuqt_ref, kvn_ref, wuk_ref, wuvt_ref,
                        cost_ref, sint_ref, ccs_ref, sn_ref,
                        qt_ref, k_ref, vt_ref, qbt_ref, kb_ref, vbt_ref):
    xb = x_ref[...].astype(_BF16)
    proj = _dot(xb, w_ref[...])
    q_lat = proj[:, 0:256]
    kv_lat = proj[:, 256:384]
    kr = proj[:, 384:512]
    qn = _rms(q_lat, qn_ref[...]).astype(_BF16)
    qt = _dot_nt(wuqt_ref[...], qn) * (MLA_QK ** -0.5 * _LOG2E)
    cos_t = cost_ref[...]
    sin_t = sint_ref[...]
    tm = qt.shape[1]
    zpad = jnp.zeros((HEAD_PAD - MLA_QK, tm), _F32)
    for h in range(MLA_HEADS):
        r0 = h * HEAD_PAD
        x1 = qt[r0 + 64:r0 + 80, :]
        x2 = qt[r0 + 80:r0 + 96, :]
        blk = jnp.concatenate([qt[r0:r0 + 64, :], x1 * cos_t - x2 * sin_t, x1 * sin_t + x2 * cos_t, zpad], axis=0)
        qt_ref[0, r0:r0 + HEAD_PAD, :] = blk.astype(_BF16)
    kvn = _rms(kv_lat, kvn_ref[...]).astype(_BF16)
    kn = _dot(kvn, wuk_ref[...])
    lane = lax.broadcasted_iota(jnp.int32, kr.shape, 1)
    swapped = jnp.where(lane < 16, pltpu.roll(kr, 112, 1), pltpu.roll(kr, 16, 1))
    kpe = kr * ccs_ref[...] + swapped * sn_ref[...]
    kpe = pltpu.roll(kpe, 64, 1)
    k_ref[0] = (kn + jnp.concatenate([kpe] * MLA_HEADS, axis=1)).astype(_BF16)
    vt = _dot_nt(wuvt_ref[...], kvn).astype(_BF16)
    extra = (lax.broadcasted_iota(jnp.int32, (VT_ROWS - MLA_V, tm), 0) == 0).astype(_BF16)
    for h in range(MLA_HEADS):
        vt_ref[0, h * VT_ROWS:h * VT_ROWS + MLA_V, :] = vt[h * MLA_V:(h + 1) * MLA_V, :]
        vt_ref[0, h * VT_ROWS + MLA_V:(h + 1) * VT_ROWS, :] = extra
    kb_ref[0] = proj[:, 512:640].astype(_BF16)
    vbt_ref[0] = _dot_nt(wvbt_ref[...], xb).astype(_BF16)
    qbt = (_dot_nt(wqbt_ref[...], xb) * (SWA_D ** -0.5)).astype(_BF16)
    zhalf = jnp.zeros((SWA_D, tm), _BF16)
    for h in range(SWA_HEADS):
        g = h // SWA_REP
        real = qbt[h * SWA_D:(h + 1) * SWA_D, :]
        qbt_ref[0, h * LANE + g * SWA_D:h * LANE + (g + 1) * SWA_D, :] = real
        qbt_ref[0, h * LANE + (1 - g) * SWA_D:h * LANE + (2 - g) * SWA_D, :] = zhalf


def _attn_inproj(x2d, w, wqbt, wvbt, qn, wuqt, kvn, wuk, wuvt, cos_t, sin_t, ccs, sn, bsz, seq, tm):
    nst = seq // tm
    const = lambda shape: pl.BlockSpec(shape, lambda b, i: (0,) * len(shape))
    out_shape = (
        jax.ShapeDtypeStruct((bsz, MLA_HEADS * HEAD_PAD, seq), _BF16),
        jax.ShapeDtypeStruct((bsz, seq, MLA_HEADS * HEAD_PAD), _BF16),
        jax.ShapeDtypeStruct((bsz, MLA_HEADS * VT_ROWS, seq), _BF16),
        jax.ShapeDtypeStruct((bsz, SWA_HEADS * LANE, seq), _BF16),
        jax.ShapeDtypeStruct((bsz, seq, SWA_KV_HEADS * SWA_D), _BF16),
        jax.ShapeDtypeStruct((bsz, SWA_KV_HEADS * SWA_D, seq), _BF16),
    )
    return pl.pallas_call(
        _attn_inproj_kernel,
        out_shape=out_shape,
        grid=(bsz, nst),
        in_specs=[
            pl.BlockSpec((tm, x2d.shape[1]), lambda b, i: (b * nst + i, 0)),
            const(w.shape), const(wqbt.shape), const(wvbt.shape), const(qn.shape), const(wuqt.shape),
            const(kvn.shape), const(wuk.shape), const(wuvt.shape),
            pl.BlockSpec((16, tm), lambda b, i: (0, i)),
            pl.BlockSpec((16, tm), lambda b, i: (0, i)),
            pl.BlockSpec((tm, LANE), lambda b, i: (i, 0)),
            pl.BlockSpec((tm, LANE), lambda b, i: (i, 0)),
        ],
        out_specs=(
            pl.BlockSpec((1, MLA_HEADS * HEAD_PAD, tm), lambda b, i: (b, 0, i)),
            pl.BlockSpec((1, tm, MLA_HEADS * HEAD_PAD), lambda b, i: (b, i, 0)),
            pl.BlockSpec((1, MLA_HEADS * VT_ROWS, tm), lambda b, i: (b, 0, i)),
            pl.BlockSpec((1, SWA_HEADS * LANE, tm), lambda b, i: (b, 0, i)),
            pl.BlockSpec((1, tm, SWA_KV_HEADS * SWA_D), lambda b, i: (b, i, 0)),
            pl.BlockSpec((1, SWA_KV_HEADS * SWA_D, tm), lambda b, i: (b, 0, i)),
        ),
        name="attn_inproj",
    )(x2d, w, wqbt, wvbt, qn, wuqt, kvn, wuk, wuvt, cos_t, sin_t, ccs, sn)


def _mla_flash_kernel(qt_ref, k_ref, vt_ref, ot_ref, *scratch, tk, nstream):
    st_sc = scratch[:2 * nstream]
    p_sc = scratch[2 * nstream:4 * nstream]
    tq = qt_ref.shape[2] // nstream
    nkv = k_ref.shape[1] // tk
    assert nkv % 2 == 0
    qts = [qt_ref[0, :, s * tq:(s + 1) * tq] for s in range(nstream)]

    def keys(j):
        return k_ref[0, pl.ds(pl.multiple_of(j * tk, tk), tk), :]

    def values(j):
        return vt_ref[0, :, pl.ds(pl.multiple_of(j * tk, tk), tk)]

    def phase(j, cur, state):
        nxt = 1 - cur
        kt = keys(jnp.minimum(j + 1, nkv - 1))
        vt = values(jnp.maximum(j - 1, 0))
        out = []
        for s in range(nstream):
            m, acc = state[s]
            st_sc[2 * s + nxt][...] = _dot(kt, qts[s])
            pv = _dot(vt, p_sc[2 * s + nxt][...])
            st = st_sc[2 * s + cur][...]
            m_new = jnp.maximum(m, jnp.max(st, axis=0, keepdims=True))
            alpha = jnp.exp2(m - m_new)
            p_sc[2 * s + cur][...] = jnp.exp2(st - m_new).astype(_BF16)
            out.append((m_new, (acc + pv) * alpha))
        return out

    def body(jj, state):
        state = phase(2 * jj, 0, state)
        return phase(2 * jj + 1, 1, state)

    k0 = keys(0)
    state = []
    for s in range(nstream):
        st_sc[2 * s][...] = _dot(k0, qts[s])
        p_sc[2 * s + 1][...] = jnp.zeros((tk, tq), _BF16)
        state.append((jnp.full((1, tq), -jnp.inf, _F32), jnp.zeros((VT_ROWS, tq), _F32)))
    state = lax.fori_loop(0, nkv // 2, body, state)
    v_last = values(nkv - 1)
    for s in range(nstream):
        acc = state[s][1] + _dot(v_last, p_sc[2 * s + 1][...])
        ot_ref[0, :, s * tq:(s + 1) * tq] = (acc[0:MLA_V, :] / acc[MLA_V:MLA_V + 1, :]).astype(ot_ref.dtype)


def _mla_flash(qt, k, vt, tq, tk, nstream):
    bsz, _, seq = qt.shape
    tqs = tq * nstream
    scratch = [pltpu.VMEM((tk, tq), _F32)] * (2 * nstream) + [pltpu.VMEM((tk, tq), _BF16)] * (2 * nstream)
    return pl.pallas_call(
        functools.partial(_mla_flash_kernel, tk=tk, nstream=nstream),
        out_shape=jax.ShapeDtypeStruct((bsz, MLA_HEADS * MLA_V, seq), _BF16),
        grid=(bsz, MLA_HEADS, seq // tqs),
        in_specs=[
            pl.BlockSpec((1, HEAD_PAD, tqs), lambda b, h, i: (b, h, i)),
            pl.BlockSpec((1, seq, HEAD_PAD), lambda b, h, i: (b, 0, h)),
            pl.BlockSpec((1, VT_ROWS, seq), lambda b, h, i: (b, h, 0)),
        ],
        out_specs=pl.BlockSpec((1, MLA_V, tqs), lambda b, h, i: (b, h, i)),
        scratch_shapes=scratch,
        name="mla_flash",
    )(qt, k, vt)


def _swa_kernel(relb_ref, sinks_ref, qt_ref, k_ref, vt_ref, bucket_ref, o_ref, bias_sc, *, nblk, seq):
    first = jnp.logical_and(pl.program_id(0) == 0, pl.program_id(1) == 0)

    @pl.when(first)
    def _build_bias():
        bucket = bucket_ref[...]
        for h in range(SWA_HEADS):
            acc = jnp.full(bucket.shape, NEG_BIG, _F32)
            for bk in range(REL_BUCKETS):
                acc = jnp.where(bucket == bk, relb_ref[bk, h], acc)
            bias_sc[h] = acc

    j = pl.program_id(1)
    krow = lax.broadcasted_iota(jnp.int32, (3 * BLOCK, 1), 0)

    def block(u, c):
        n = j * nblk + u
        c0 = pl.multiple_of(u * BLOCK, BLOCK)
        w0 = pl.multiple_of(n * BLOCK, BLOCK)
        key_pos = n * BLOCK - BLOCK + krow
        emask = jnp.where(jnp.logical_and(key_pos >= 0, key_pos < seq), 0.0, NEG_BIG).astype(_F32)
        kw = k_ref[0, pl.ds(w0, 3 * BLOCK), :]
        vw = vt_ref[0, :, pl.ds(w0, 3 * BLOCK)]
        for g in range(SWA_KV_HEADS):
            heads = range(g * SWA_REP, (g + 1) * SWA_REP)
            qs = jnp.concatenate([qt_ref[0, h * LANE:(h + 1) * LANE, pl.ds(c0, BLOCK)] for h in heads], axis=1)
            bias = jnp.concatenate([bias_sc[h] for h in heads], axis=1)
            sink = jnp.concatenate([jnp.full((1, BLOCK), sinks_ref[h], _F32) for h in heads], axis=1)
            st = _dot(kw, qs) + bias + emask
            m = jnp.maximum(jnp.max(st, axis=0, keepdims=True), sink)
            p = jnp.exp(st - m)
            den = jnp.sum(p, axis=0, keepdims=True) + jnp.exp(sink - m)
            ot = _dot(vw[g * SWA_D:(g + 1) * SWA_D, :], p.astype(_BF16)) / den
            for r, h in enumerate(heads):
                o_ref[0, h * SWA_D:(h + 1) * SWA_D, pl.ds(c0, BLOCK)] = ot[:, r * BLOCK:(r + 1) * BLOCK].astype(o_ref.dtype)
        return c

    def pair(t, c):
        block(2 * t, c)
        return block(2 * t + 1, c)

    def quad(t, c):
        pair(2 * t, c)
        return pair(2 * t + 1, c)

    if nblk % 4 == 0:
        lax.fori_loop(0, nblk // 4, quad, 0)
    elif nblk % 2 == 0:
        lax.fori_loop(0, nblk // 2, pair, 0)
    else:
        lax.fori_loop(0, nblk, block, 0)


def _swa(qbt, kpad, vtpad, bucket_t, rel_bias, sinks, bsz, seq, nblk):
    nsteps = seq // (nblk * BLOCK)
    cols = nblk * BLOCK
    return pl.pallas_call(
        functools.partial(_swa_kernel, nblk=nblk, seq=seq),
        out_shape=jax.ShapeDtypeStruct((bsz, SWA_HEADS * SWA_D, seq), _BF16),
        grid=(bsz, nsteps),
        in_specs=[
            pl.BlockSpec(memory_space=pltpu.SMEM),
            pl.BlockSpec(memory_space=pltpu.SMEM),
            pl.BlockSpec((1, SWA_HEADS * LANE, cols), lambda b, j: (b, 0, j)),
            pl.BlockSpec((1, seq + 2 * BLOCK, SWA_KV_HEADS * SWA_D), lambda b, j: (b, 0, 0)),
            pl.BlockSpec((1, SWA_KV_HEADS * SWA_D, seq + 2 * BLOCK), lambda b, j: (b, 0, 0)),
            pl.BlockSpec((3 * BLOCK, BLOCK), lambda b, j: (0, 0)),
        ],
        out_specs=pl.BlockSpec((1, SWA_HEADS * SWA_D, cols), lambda b, j: (b, 0, j)),
        scratch_shapes=[pltpu.VMEM((SWA_HEADS, 3 * BLOCK, BLOCK), _F32)],
        compiler_params=pltpu.CompilerParams(dimension_semantics=("arbitrary", "arbitrary")),
        name="swa",
    )(rel_bias, sinks, qbt, kpad, vtpad, bucket_t)


def _route(logits_t, rbias):
    sc = jax.nn.sigmoid(logits_t)
    bz = sc + rbias
    s_rows = [sc[e:e + 1, :] for e in range(N_EXPERTS)]
    b_rows = [bz[e:e + 1, :] for e in range(N_EXPERTS)]
    gsel = None
    best = None
    for g in range(N_GROUPS):
        r = b_rows[4 * g:4 * g + 4]
        gs = r[0] + r[1]
        for (i, k) in ((0, 2), (0, 3), (1, 2), (1, 3), (2, 3)):
            gs = jnp.maximum(gs, r[i] + r[k])
        if g == 0:
            gsel = jnp.zeros(gs.shape, jnp.int32)
            best = gs
        else:
            better = gs > best
            gsel = jnp.where(better, g, gsel)
            best = jnp.where(better, gs, best)

    def pick(rows, k):
        out = rows[12 + k]
        for g in (2, 1, 0):
            out = jnp.where(gsel == g, rows[4 * g + k], out)
        return out

    v = [pick(b_rows, k) for k in range(4)]
    s = [pick(s_rows, k) for k in range(4)]
    i1 = jnp.zeros(gsel.shape, jnp.int32)
    m1 = v[0]
    w1 = s[0]
    for k in range(1, 4):
        gt = v[k] > m1
        i1 = jnp.where(gt, k, i1)
        m1 = jnp.where(gt, v[k], m1)
        w1 = jnp.where(gt, s[k], w1)
    i2 = jnp.full(gsel.shape, -1, jnp.int32)
    m2 = jnp.full(m1.shape, -jnp.inf, _F32)
    w2 = jnp.zeros(m1.shape, _F32)
    for k in range(4):
        ok = jnp.logical_and(i1 != k, jnp.logical_or(i2 < 0, v[k] > m2))
        i2 = jnp.where(ok, k, i2)
        m2 = jnp.where(ok, v[k], m2)
        w2 = jnp.where(ok, s[k], w2)
    tot = w1 + w2
    g1 = w1 / tot
    g2 = w2 / tot
    first_lo = i1 < i2
    lo = jnp.where(first_lo, i1, i2)
    hi = jnp.where(first_lo, i2, i1)
    pair = jnp.where(lo == 0, hi - 1, jnp.where(lo == 1, hi + 1, 5))
    cls = gsel * 6 + pair
    return cls, jnp.where(first_lo, g1, g2), jnp.where(first_lo, g2, g1)


def _ln_router_store(z, g_ref, b_ref, rwt_ref, rb_ref, h_ref, cls_ref):
    h = _layer_norm(z, g_ref[...], b_ref[...])
    tm = h.shape[0]
    h_hi = h.astype(_BF16)
    h_lo = (h - h_hi.astype(_F32)).astype(_BF16)
    rw = rwt_ref[...]
    rw_hi = rw.astype(_BF16)
    rw_lo = (rw - rw_hi.astype(_F32)).astype(_BF16)
    part = _dot_nt(jnp.concatenate([rw_hi, rw_lo], axis=0), h_hi)
    logits_t = part[0:N_EXPERTS] + part[N_EXPERTS:2 * N_EXPERTS] + _dot_nt(rw_hi, h_lo)
    cls, g_lo, g_hi = _route(logits_t, rb_ref[...])
    rows = jnp.concatenate([g_lo, g_hi, jnp.zeros((GATE_LANES - 2, tm), _F32)], axis=0)
    d = h.shape[1]
    h_ref[:, 0:d] = h
    h_ref[:, d:d + GATE_LANES] = rows.T
    cls_ref[0] = cls


def _outproj_ln_router_kernel(ot_ref, ob_ref, x_ref, w_ref, g_ref, b_ref, rwt_ref, rb_ref, h_ref, cls_ref):
    heads_t = jnp.concatenate([ot_ref[0], ob_ref[0]], axis=0)
    mixed = lax.dot_general(heads_t, w_ref[...], _TN_DIMS, preferred_element_type=_F32)
    z = ALPHA * x_ref[...] + mixed
    _ln_router_store(z, g_ref, b_ref, rwt_ref, rb_ref, h_ref, cls_ref)


def _outproj_ln_router(ot, ob, x2d, w, g, b, rwt, rb, bsz, seq, tm):
    nst = seq // tm
    d = x2d.shape[1]
    const = lambda shape: pl.BlockSpec(shape, lambda bb, i: (0,) * len(shape))
    return pl.pallas_call(
        _outproj_ln_router_kernel,
        out_shape=(jax.ShapeDtypeStruct((bsz * seq, d + GATE_LANES), _F32),
                   jax.ShapeDtypeStruct((bsz * nst, 1, tm), jnp.int32)),
        grid=(bsz, nst),
        in_specs=[
            pl.BlockSpec((1, ot.shape[1], tm), lambda bb, i: (bb, 0, i)),
            pl.BlockSpec((1, ob.shape[1], tm), lambda bb, i: (bb, 0, i)),
            pl.BlockSpec((tm, d), lambda bb, i: (bb * nst + i, 0)),
            const(w.shape), const(g.shape), const(b.shape), const(rwt.shape), const(rb.shape),
        ],
        out_specs=(pl.BlockSpec((tm, d + GATE_LANES), lambda bb, i: (bb * nst + i, 0)),
                   pl.BlockSpec((1, 1, tm), lambda bb, i: (bb * nst + i, 0, 0))),
        name="outproj_ln_router",
    )(ot, ob, x2d, w, g, b, rwt, rb)


_SUBLANES = 8
_SCAN_RADIX = 4
_DMA_CHUNK = 32


def _moe_kernel(ea_ref, eb_ref, nv_ref, off_ref, src_ref, nsrc_ref, h_hbm, w1a, w3a, w2a, w1b, w3b, w2b, g_ref, b_ref,
                out_hbm, xbuf, obuf, w13a, w2a_bf, w13b, w2b_bf, sem_in, sem_out):
    i = pl.program_id(0)
    ntiles = pl.num_programs(0)
    tm = xbuf.shape[1] * _SUBLANES
    d = obuf.shape[2]
    nv = nv_ref[i]
    nv_prev = jnp.where(i > 0, nv_ref[jnp.maximum(i - 1, 0)], 0)
    nv_next = jnp.where(i + 1 < ntiles, nv_ref[jnp.minimum(i + 1, ntiles - 1)], 0)
    slot = lax.rem(i, 2)
    shift = lax.rem(off_ref[i], LANE)
    shift_next = lax.rem(off_ref[jnp.minimum(i + 1, ntiles - 1)], LANE)

    def hbm_row(ref, idx):
        return ref.at[lax.shift_right_logical(idx, 3), jnp.bitwise_and(idx, _SUBLANES - 1)]

    def gather_start(idx_ref, sh, s):
        for r in range(tm):
            idx = idx_ref[sh + r]
            pltpu.make_async_copy(hbm_row(h_hbm, idx), xbuf.at[s, r // _SUBLANES, r % _SUBLANES], sem_in.at[s]).start()

    def scatter_wait(n):
        for bit in range(tm.bit_length()):
            rows = 1 << bit

            @pl.when(jnp.bitwise_and(n, rows) != 0)
            def _():
                if rows >= _SUBLANES:
                    grp = pl.ds(0, rows // _SUBLANES)
                    pltpu.make_async_copy(obuf.at[grp], out_hbm.at[grp], sem_out).wait()
                else:
                    pltpu.make_async_copy(obuf.at[0, pl.ds(0, rows)], out_hbm.at[0, pl.ds(0, rows)], sem_out).wait()

    @pl.when(jnp.logical_and(i == 0, nv > 0))
    def _prologue():
        gather_start(src_ref, shift, 0)

    @pl.when(nv_next > 0)
    def _prefetch():
        gather_start(nsrc_ref, shift_next, 1 - slot)

    prev_i = jnp.maximum(i - 1, 0)
    dff = w2a.shape[0]

    def refresh(w1, w3, w2, w13_sc, w2_sc):
        w13_sc[:, 0:dff] = w1[...].astype(_BF16)
        w13_sc[:, dff:2 * dff] = w3[...].astype(_BF16)
        w2_sc[...] = w2[...].astype(_BF16)

    @pl.when(jnp.logical_and(nv > 0, jnp.logical_or(i == 0, ea_ref[i] != ea_ref[prev_i])))
    def _refresh_a():
        refresh(w1a, w3a, w2a, w13a, w2a_bf)

    @pl.when(jnp.logical_and(nv > 0, jnp.logical_or(i == 0, eb_ref[i] != eb_ref[prev_i])))
    def _refresh_b():
        refresh(w1b, w3b, w2b, w13b, w2b_bf)

    @pl.when(nv > 0)
    def _tile():
        pltpu.make_async_copy(h_hbm.at[pl.ds(0, tm // _SUBLANES)], xbuf.at[slot], sem_in.at[slot]).wait()
        def compute(rows):
            xg = xbuf[slot, 0:rows // _SUBLANES].reshape(rows, d + GATE_LANES)
            x = xg[:, 0:d]
            gates = xg[:, d:d + GATE_LANES]
            ga = gates[:, 0:1]
            gb = gates[:, 1:2]
            xb = x.astype(_BF16)

            def expert(w13, w2):
                uv = _dot(xb, w13[...])
                u = uv[:, 0:dff]
                hh = u * _sigmoid(u) * uv[:, dff:2 * dff]
                return _dot(hh.astype(_BF16), w2[...])

            y = ga * expert(w13a, w2a_bf) + gb * expert(w13b, w2b_bf)
            z = _layer_norm(ALPHA * x + y, g_ref[...], b_ref[...])

            @pl.when(nv_prev > 0)
            def _drain_prev():
                scatter_wait(nv_prev)

            obuf[0:rows // _SUBLANES] = z.reshape(rows // _SUBLANES, _SUBLANES, d)

        @pl.when(nv > tm // 2)
        def _full():
            compute(tm)

        @pl.when(nv <= tm // 2)
        def _half():
            compute(tm // 2)

        for c in range(tm // _DMA_CHUNK):
            @pl.when(nv >= (c + 1) * _DMA_CHUNK)
            def _chunk():
                for r in range(c * _DMA_CHUNK, (c + 1) * _DMA_CHUNK):
                    idx = src_ref[shift + r]
                    pltpu.make_async_copy(obuf.at[r // _SUBLANES, r % _SUBLANES], hbm_row(out_hbm, idx),
                                          sem_out).start()

        def group(gi, c):
            for u in range(_SUBLANES):
                idx = src_ref[shift + gi * _SUBLANES + u]
                pltpu.make_async_copy(obuf.at[gi, u], hbm_row(out_hbm, idx), sem_out).start()
            return c

        def single(r, c):
            idx = src_ref[shift + r]
            pltpu.make_async_copy(hbm_row(obuf, r), hbm_row(out_hbm, idx), sem_out).start()
            return c

        ngroups = lax.shift_right_logical(nv, 3)
        lax.fori_loop((nv // _DMA_CHUNK) * (_DMA_CHUNK // _SUBLANES), ngroups, group, 0)
        lax.fori_loop(ngroups * _SUBLANES, nv, single, 0)

        @pl.when(nv_next == 0)
        def _drain_last():
            scatter_wait(nv)


def _moe(he, ea, eb, nvalid, off, order, w1, w3, w2, g, b, n_tokens, d):
    ntiles = ea.shape[0]
    tm = MOE_TM
    dff = w2.shape[1]
    wspec_a = lambda shape: pl.BlockSpec((None,) + shape, lambda i, ea_r, eb_r, nv_r, off_r: (ea_r[i], 0, 0))
    wspec_b = lambda shape: pl.BlockSpec((None,) + shape, lambda i, ea_r, eb_r, nv_r, off_r: (eb_r[i], 0, 0))

    def window(nxt):
        def start(i, ea_r, eb_r, nv_r, off_r):
            o = off_r[jnp.minimum(i + nxt, ntiles - 1)]
            return (pl.multiple_of((o // LANE) * LANE, LANE),)

        return pl.BlockSpec((pl.Element(2 * tm),), start, memory_space=pltpu.SMEM)

    grid_spec = pltpu.PrefetchScalarGridSpec(
        num_scalar_prefetch=4,
        grid=(ntiles,),
        in_specs=[
            window(0), window(1),
            pl.BlockSpec(memory_space=pl.ANY),
            wspec_a((d, dff)), wspec_a((d, dff)), wspec_a((dff, d)),
            wspec_b((d, dff)), wspec_b((d, dff)), wspec_b((dff, d)),
            pl.BlockSpec((1, d), lambda i, *_: (0, 0)),
            pl.BlockSpec((1, d), lambda i, *_: (0, 0)),
        ],
        out_specs=pl.BlockSpec(memory_space=pl.ANY),
        scratch_shapes=[
            pltpu.VMEM((2, tm // _SUBLANES, _SUBLANES, d + GATE_LANES), _F32),
            pltpu.VMEM((tm // _SUBLANES, _SUBLANES, d), _F32),
            pltpu.VMEM((d, 2 * dff), _BF16), pltpu.VMEM((dff, d), _BF16),
            pltpu.VMEM((d, 2 * dff), _BF16), pltpu.VMEM((dff, d), _BF16),
            pltpu.SemaphoreType.DMA((2,)),
            pltpu.SemaphoreType.DMA,
        ],
    )
    he3 = he.reshape(n_tokens // _SUBLANES, _SUBLANES, d + GATE_LANES)
    out = pl.pallas_call(
        _moe_kernel,
        out_shape=jax.ShapeDtypeStruct((n_tokens // _SUBLANES, _SUBLANES, d), _F32),
        grid_spec=grid_spec,
        compiler_params=pltpu.CompilerParams(dimension_semantics=("arbitrary",)),
        name="moe",
    )(ea, eb, nvalid, off, order, order, he3, w1, w3, w2, w1, w3, w2, g, b)
    return out.reshape(n_tokens, d)


def _moe_plan(cls, n_tokens):
    tm = MOE_TM
    ntiles = n_tokens // tm + N_CLASSES
    cls = cls.reshape(n_tokens)
    order = jnp.argsort(cls, stable=True).astype(jnp.int32)
    counts = jnp.sum((cls[:, None] == jnp.arange(N_CLASSES, dtype=jnp.int32)[None, :]).astype(jnp.int32), axis=0)
    cstart = jnp.cumsum(counts) - counts
    ptiles = (counts + tm - 1) // tm
    tend = jnp.cumsum(ptiles)
    tstart = tend - ptiles
    tile = jnp.arange(ntiles, dtype=jnp.int32)
    tcls = jnp.minimum(jnp.sum((tile[:, None] >= tend[None, :]).astype(jnp.int32), axis=1), N_CLASSES - 1)
    used = tile < tend[-1]
    nvalid = jnp.where(used, jnp.clip(counts[tcls] - (tile - tstart[tcls]) * tm, 0, tm), 0).astype(jnp.int32)
    off = jnp.where(used, cstart[tcls] + (tile - tstart[tcls]) * tm, 0).astype(jnp.int32)
    order = jnp.concatenate([order, jnp.zeros((2 * tm,), jnp.int32)])
    pair_lo = jnp.array([0, 0, 0, 1, 1, 2], jnp.int32)
    pair_hi = jnp.array([1, 2, 3, 2, 3, 3], jnp.int32)
    ea = (4 * (tcls // 6) + pair_lo[tcls % 6]).astype(jnp.int32)
    eb = (4 * (tcls // 6) + pair_hi[tcls % 6]).astype(jnp.int32)
    return ea, eb, nvalid, off, order


def _scan_slabs(a_refs, b_refs, c_refs, h_in, reverse):
    nlev = len(a_refs)
    ks = list(range(_SCAN_RADIX))
    if reverse:
        ks.reverse()
    for l in range(nlev - 1):
        grp = a_refs[l].shape[1] // _SCAN_RADIX
        p = q = None
        for k in ks:
            sl = pl.ds(k, grp, stride=_SCAN_RADIX)
            a = a_refs[l][:, sl, :]
            b = b_refs[l][:, sl, :]
            if p is None:
                p, q = a, b
            else:
                p, q = a * p, a * q + b
                a_refs[l][:, sl, :] = p
                b_refs[l][:, sl, :] = q
        a_refs[l + 1][...] = p
        b_refs[l + 1][...] = q
    rows = a_refs[-1].shape[1]
    a = a_refs[-1][...]
    b = b_refs[-1][...]
    out = [None] * rows
    h = h_in
    for r in (range(rows - 1, -1, -1) if reverse else range(rows)):
        h = a[:, r:r + 1, :] * h + b[:, r:r + 1, :]
        out[r] = h
    h_out = h
    b_refs[-1][...] = jnp.concatenate(out, axis=1)
    edge = jnp.broadcast_to(h_in, (h_in.shape[0], _SUBLANES, h_in.shape[2]))
    for l in range(nlev - 2, -1, -1):
        grp = a_refs[l].shape[1] // _SCAN_RADIX
        c = c_refs[l]
        c[:, _SUBLANES:_SUBLANES + grp, :] = b_refs[l + 1][...]
        if reverse:
            c[:, _SUBLANES + grp:2 * _SUBLANES + grp, :] = edge
            cin = c[:, _SUBLANES + 1:_SUBLANES + 1 + grp, :]
        else:
            c[:, 0:_SUBLANES, :] = edge
            cin = c[:, _SUBLANES - 1:_SUBLANES - 1 + grp, :]
        for k in ks:
            sl = pl.ds(k, grp, stride=_SCAN_RADIX)
            b_refs[l][:, sl, :] = a_refs[l][:, sl, :] * cin + b_refs[l][:, sl, :]
    return h_out


def _rglru_kernel(*refs, reverse, final, ntile):
    if final:
        (src_ref, prev_ref, next_ref, cw_ref, cb_ref, wa_ref, ba_ref, wx_ref, bx_ref, lam_ref,
         hf_ref, wgate_ref, res_ref, wout_ref, g_ref, b_ref, rwt_ref, rb_ref,
         h_ref, cls_ref, xe_sc, carry_sc, *scan_sc) = refs
    else:
        (src_ref, prev_ref, next_ref, cw_ref, cb_ref, wa_ref, ba_ref, wx_ref, bx_ref, lam_ref, wxr_ref,
         hout_ref, xr_out_ref, xe_sc, carry_sc, *scan_sc) = refs
    nlev = (len(scan_sc) + 1) // 3
    a_refs, b_refs, c_refs = scan_sc[:nlev], scan_sc[nlev:2 * nlev], scan_sc[2 * nlev:]
    i = pl.program_id(1)
    ti = (ntile - 1 - i) if reverse else i
    tt = src_ref.shape[0]

    @pl.when(i == 0)
    def _reset():
        carry_sc[...] = jnp.zeros_like(carry_sc)

    prev = jnp.where(ti > 0, prev_ref[...], 0.0)
    nxt = jnp.where(ti < ntile - 1, next_ref[...], 0.0)
    xin = jnp.concatenate([prev, src_ref[...], nxt], axis=0)
    if final:
        xe = xin
    else:
        xe = _dot(xin.astype(_BF16), wxr_ref[...])
        xr_out_ref[...] = xe[_SUBLANES:_SUBLANES + tt, :]
    for n in range(LRU_BLOCKS):
        xe_sc[n] = xe[:, n * LRU_BW:(n + 1) * LRU_BW]
    cw = cw_ref[...]
    cb = cb_ref[...]
    half_decay = (0.5 * LRU_C) * -jnp.log(1.0 + jnp.exp(-lam_ref[...]))
    for n in range(LRU_BLOCKS):
        sl = slice(n * LRU_BW, (n + 1) * LRU_BW)
        xcn = cb[:, sl]
        for k in range(cw.shape[0]):
            xcn = xcn + cw[k:k + 1, sl] * xe_sc[n, _SUBLANES - CONV_LEFT + k:_SUBLANES - CONV_LEFT + k + tt, :]
        xb = xcn.astype(_BF16)
        tr = jnp.tanh(_dot(xb, wa_ref[n]) + ba_ref[:, sl])
        ig = 0.5 * jnp.tanh(_dot(xb, wx_ref[n]) + bx_ref[:, sl]) + 0.5
        log_a = tr * half_decay[:, sl] + half_decay[:, sl]
        th = jnp.tanh(log_a)
        num = -2.0 * th
        scale = jnp.where(num > 0.0, num * lax.rsqrt(num * (1.0 - th)), 0.0)
        a_refs[0][n] = jnp.exp(log_a)
        b_refs[0][n] = scale * (ig * xcn)
    carry_sc[...] = _scan_slabs(a_refs, b_refs, c_refs, carry_sc[...], reverse)
    if final:
        gate = _dot(res_ref[...].astype(_BF16), wgate_ref[...])
    ys = []
    for n in range(LRU_BLOCKS):
        sl = slice(n * LRU_BW, (n + 1) * LRU_BW)
        h = b_refs[0][n]
        if final:
            ys.append((hf_ref[:, sl] + h) * jax.nn.gelu(gate[:, sl], approximate=True))
        else:
            hout_ref[:, sl] = h
    if final:
        y = jnp.concatenate(ys, axis=1).astype(_BF16)
        z = ALPHA * res_ref[...] + _dot(y, wout_ref[...])
        _ln_router_store(z, g_ref, b_ref, rwt_ref, rb_ref, h_ref, cls_ref)


def _rglru(src, cw, cb, wa, ba, wx, bx, lam, bsz, seq, tt, reverse, wxr=None, tail=None):
    ntile = seq // tt
    width = cw.shape[1]
    final = tail is not None
    tidx = (lambda i: ntile - 1 - i) if reverse else (lambda i: i)
    row = lambda b, i: (b * ntile + tidx(i), 0)
    g8 = tt // 8
    nb8 = seq // 8

    def prev_map(b, i):
        return (b * nb8 + jnp.maximum(tidx(i) * g8 - 1, 0), 0)

    def next_map(b, i):
        return (b * nb8 + jnp.minimum((tidx(i) + 1) * g8, nb8 - 1), 0)

    const = lambda shape: pl.BlockSpec(shape, lambda b, i: (0,) * len(shape))
    swidth = src.shape[1]
    in_specs = [
        pl.BlockSpec((tt, swidth), row),
        pl.BlockSpec((8, swidth), prev_map),
        pl.BlockSpec((8, swidth), next_map),
        const(cw.shape), const(cb.shape), const(wa.shape), const(ba.shape), const(wx.shape), const(bx.shape),
        const(lam.shape),
    ]
    args = [src, src, src, cw, cb, wa, ba, wx, bx, lam]
    rows = [tt]
    while rows[-1] > _SUBLANES:
        assert rows[-1] % _SCAN_RADIX == 0
        rows.append(rows[-1] // _SCAN_RADIX)
    assert rows[-1] == _SUBLANES
    slab = lambda r: pltpu.VMEM((LRU_BLOCKS, r, LRU_BW), _F32)
    scratch = [slab(tt + 2 * _SUBLANES), pltpu.VMEM((LRU_BLOCKS, 1, LRU_BW), _F32)]
    scratch += [slab(r) for r in rows] * 2 + [slab(r + 2 * _SUBLANES) for r in rows[1:]]
    if final:
        hf, wgate, res, wout, g, b, rwt, rb = tail
        d = res.shape[1]
        in_specs += [pl.BlockSpec((tt, width), row), const(wgate.shape), pl.BlockSpec((tt, d), row),
                     const(wout.shape), const(g.shape), const(b.shape), const(rwt.shape), const(rb.shape)]
        args += [hf, wgate, res, wout, g, b, rwt, rb]
        out_shape = (jax.ShapeDtypeStruct((bsz * seq, d + GATE_LANES), _F32),
                     jax.ShapeDtypeStruct((bsz * ntile, 1, tt), jnp.int32))
        out_specs = (pl.BlockSpec((tt, d + GATE_LANES), row),
                     pl.BlockSpec((1, 1, tt), lambda b, i: (b * ntile + tidx(i), 0, 0)))
    else:
        in_specs.append(const(wxr.shape))
        args.append(wxr)
        out_shape = (jax.ShapeDtypeStruct((bsz * seq, width), _F32),) * 2
        out_specs = (pl.BlockSpec((tt, width), row),) * 2
    return pl.pallas_call(
        functools.partial(_rglru_kernel, reverse=reverse, final=final, ntile=ntile),
        out_shape=out_shape,
        grid=(bsz, ntile),
        in_specs=in_specs,
        out_specs=out_specs,
        scratch_shapes=scratch,
        compiler_params=pltpu.CompilerParams(dimension_semantics=("arbitrary", "arbitrary")),
        name="rglru_bwd_tail" if final else "rglru_fwd",
    )(*args)


def _rotary_tables(seq):
    half = MLA_ROPE // 2
    inv_freq = ROPE_THETA ** (-jnp.arange(half, dtype=_F32) / half)
    ang = jnp.arange(seq, dtype=_F32)[:, None] * inv_freq[None, :]
    return jnp.cos(ang), jnp.sin(ang)


def _t5_bucket(rel):
    n_side = REL_BUCKETS // 2
    max_exact = n_side // 2
    dist = jnp.abs(rel)
    far = max_exact + (jnp.log(jnp.maximum(dist, 1).astype(_F32) / max_exact)
                       / math.log(REL_MAX_DIST / max_exact) * (n_side - max_exact)).astype(jnp.int32)
    far = jnp.minimum(far, n_side - 1)
    return jnp.where(rel > 0, n_side, 0) + jnp.where(dist < max_exact, dist, far)


def _band_buckets():
    rel = jnp.arange(3 * BLOCK)[None, :] - BLOCK - jnp.arange(BLOCK)[:, None]
    ids = jnp.where(jnp.abs(rel) <= WINDOW, _t5_bucket(rel), REL_BUCKETS).astype(jnp.int32)
    return jnp.bitwise_and(ids, 2 * REL_BUCKETS - 1)


def _pick_tile(seq, want):
    t = min(want, seq)
    while seq % t:
        t //= 2
    return t


def kernel(x, rel_bias, router_w, router_bias, l0_w_in, l0_q_norm, l0_w_uq, l0_kv_norm, l0_w_ukv, l0_sinks, l0_w_out, l0_ln1_g, l0_ln1_b, l0_w1, l0_w3, l0_w2, l0_ln2_g, l0_ln2_b, l1_w_in, l1_conv_w, l1_conv_b, l1_wa_f, l1_ba_f, l1_wx_f, l1_bx_f, l1_lam_f, l1_wa_b, l1_ba_b, l1_wx_b, l1_bx_b, l1_lam_b, l1_w_out, l1_ln1_g, l1_ln1_b, l1_w1, l1_w3, l1_w2, l1_ln2_g, l1_ln2_b):
    bsz, seq, d = x.shape
    n_tok = bsz * seq
    assert seq % BLOCK == 0 and d == LRU_BLOCKS * LRU_BW and n_tok % MOE_TM == 0
    x2d = x.reshape(n_tok, d)
    row = lambda v: v.reshape(1, -1).astype(_F32)
    rwt = router_w.astype(_F32).T
    rb = router_bias.astype(_F32).reshape(N_EXPERTS, 1)

    o = np.cumsum([0, MLA_Q_RANK, MLA_KV_RANK, MLA_ROPE, SWA_HEADS * SWA_D, SWA_KV_HEADS * SWA_D,
                   SWA_KV_HEADS * SWA_D])
    w_q, w_kv, w_kr, w_qb, w_kb, w_vb = [l0_w_in[:, o[i]:o[i + 1]] for i in range(6)]
    w_in0 = jnp.concatenate([w_q, w_kv, jnp.pad(w_kr, ((0, 0), (0, LANE - MLA_ROPE))), w_kb], axis=1).astype(_BF16)
    wuq = l0_w_uq.reshape(MLA_Q_RANK, MLA_HEADS, MLA_QK)
    wuqt = jnp.pad(wuq, ((0, 0), (0, 0), (0, HEAD_PAD - MLA_QK))).reshape(MLA_Q_RANK, MLA_HEADS * HEAD_PAD).T
    wukv = l0_w_ukv.reshape(MLA_KV_RANK, MLA_HEADS, MLA_NOPE + MLA_V)
    wuk = jnp.pad(wukv[:, :, :MLA_NOPE], ((0, 0), (0, 0), (0, HEAD_PAD - MLA_NOPE))).reshape(MLA_KV_RANK, -1)
    wuvt = wukv[:, :, MLA_NOPE:].reshape(MLA_KV_RANK, MLA_HEADS * MLA_V).T
    cos, sin = _rotary_tables(seq)
    ccs = jnp.pad(jnp.concatenate([cos, cos], axis=1), ((0, 0), (0, LANE - MLA_ROPE)))
    sn = jnp.pad(jnp.concatenate([-sin, sin], axis=1), ((0, 0), (0, LANE - MLA_ROPE)))

    tm = _pick_tile(seq, 512)
    qt, k, vt, qbt, kb, vbt = _attn_inproj(
        x2d, w_in0, w_qb.T.astype(_BF16), w_vb.T.astype(_BF16), row(l0_q_norm), wuqt.astype(_BF16),
        row(l0_kv_norm), wuk.astype(_BF16), wuvt.astype(_BF16), cos.T, sin.T, ccs, sn, bsz, seq, tm)
    nstream = 32 if seq % 8192 == 0 else (2 if seq % 1024 == 0 else 1)
    ot = _mla_flash(qt, k, vt, _pick_tile(seq // nstream, 256), _pick_tile(seq // 2, 512), nstream)
    kpad = jnp.pad(kb, ((0, 0), (BLOCK, BLOCK), (0, 0)))
    vtpad = jnp.pad(vbt, ((0, 0), (0, 0), (BLOCK, BLOCK)))
    obt = _swa(qbt, kpad, vtpad, _band_buckets().T, rel_bias.astype(_F32), l0_sinks.astype(_F32), bsz, seq,
               _pick_tile(seq // BLOCK, 8))
    he, cls = _outproj_ln_router(ot, obt, x2d, l0_w_out.astype(_BF16), row(l0_ln1_g), row(l0_ln1_b), rwt, rb,
                                 bsz, seq, tm)
    ea, eb, nvalid, off, order = _moe_plan(cls, n_tok)
    h = _moe(he, ea, eb, nvalid, off, order, l0_w1, l0_w3, l0_w2,
             row(l0_ln2_g), row(l0_ln2_b), n_tok, d)

    tt = _pick_tile(seq, 512)
    w_in1 = l1_w_in.astype(_BF16)
    w_gate, w_xr = w_in1[:, :d], w_in1[:, d:]
    cw = l1_conv_w.astype(_F32)
    cb = row(l1_conv_b)
    halfw = lambda w: (0.5 * w).astype(_BF16)
    halfb = lambda v: 0.5 * row(v)
    hf, xr = _rglru(h, cw, cb, halfw(l1_wa_f), halfb(l1_ba_f), halfw(l1_wx_f), halfb(l1_bx_f),
                    row(l1_lam_f), bsz, seq, tt, reverse=False, wxr=w_xr)
    he, cls = _rglru(xr, cw, cb, halfw(l1_wa_b), halfb(l1_ba_b), halfw(l1_wx_b), halfb(l1_bx_b),
                     row(l1_lam_b), bsz, seq, tt, reverse=True,
                     tail=(hf, w_gate, h, l1_w_out.astype(_BF16), row(l1_ln1_g), row(l1_ln1_b), rwt, rb))
    ea, eb, nvalid, off, order = _moe_plan(cls, n_tok)
    h = _moe(he, ea, eb, nvalid, off, order, l1_w1, l1_w3, l1_w2,
             row(l1_ln2_g), row(l1_ln2_b), n_tok, d)
    return h.reshape(bsz, seq, d)
```

```python
import functools
import math

import jax
import jax.numpy as jnp
import numpy as np
from jax import lax
from jax.experimental import pallas as pl
from jax.experimental.pallas import tpu as pltpu

MLA_HEADS = 8
MLA_Q_RANK = 256
MLA_KV_RANK = 128
MLA_NOPE = 64
MLA_ROPE = 32
MLA_V = 64
MLA_QK = MLA_NOPE + MLA_ROPE
ROPE_THETA = 10000.0
SWA_HEADS = 8
SWA_KV_HEADS = 2
SWA_REP = SWA_HEADS // SWA_KV_HEADS
SWA_D = 64
WINDOW = 128
BLOCK = 128
REL_BUCKETS = 32
REL_MAX_DIST = 128
LRU_BLOCKS = 8
LRU_BW = 128
LRU_C = 8.0
CONV_LEFT = 2
N_EXPERTS = 16
N_GROUPS = 4
EXPERTS_PER_GROUP = 4
N_CLASSES = N_GROUPS * 6
DEPTH = 2
ALPHA = (2.0 * DEPTH) ** 0.25
LN_EPS = 1e-5
RMS_EPS = 1e-6
NEG_BIG = -1e30

LANE = 128
HEAD_PAD = 128
GATE_LANES = 128
VT_ROWS = 80
_LOG2E = math.log2(math.e)

MOE_TM = 256
ROW_TILE = 512
FLASH_TQ = 256
FLASH_TK = 512
FLASH_STREAMS = 32
SWA_BLOCKS = 8

_F32 = jnp.float32
_BF16 = jnp.bfloat16
_NT_DIMS = (((1,), (1,)), ((), ()))
_TN_DIMS = (((0,), (0,)), ((), ()))


def _dot(a, b):
    return jnp.dot(a, b, preferred_element_type=_F32)


def _dot_nt(a, b):
    return lax.dot_general(a, b, _NT_DIMS, preferred_element_type=_F32)


def _sigmoid(x):
    return 0.5 * jnp.tanh(0.5 * x) + 0.5


def _rms(x, g):
    return x * lax.rsqrt(jnp.mean(jnp.square(x), -1, keepdims=True) + RMS_EPS) * g


def _layer_norm(x, g, b):
    mu = jnp.mean(x, -1, keepdims=True)
    xc = x - mu
    var = jnp.mean(jnp.square(xc), -1, keepdims=True)
    return xc * lax.rsqrt(var + LN_EPS) * g + b


def _attn_inproj_kernel(x_ref, w_ref, wqbt_ref, wvbt_ref, qn_ref, wuqt_ref, kvn_ref, wuk_ref, wuvt_ref,
                        cost_ref, sint_ref, ccs_ref, sn_ref,
                        qt_ref, k_ref, vt_ref, qbt_ref, kb_ref, vbt_ref):
    xb = x_ref[...].astype(_BF16)
    proj = _dot(xb, w_ref[...])
    q_lat = proj[:, 0:256]
    kv_lat = proj[:, 256:384]
    kr = proj[:, 384:512]
    qn = _rms(q_lat, qn_ref[...]).astype(_BF16)
    qt = _dot_nt(wuqt_ref[...], qn) * (MLA_QK ** -0.5 * _LOG2E)
    cos_t = cost_ref[...]
    sin_t = sint_ref[...]
    tm = qt.shape[1]
    zpad = jnp.zeros((HEAD_PAD - MLA_QK, tm), _F32)
    for h in range(MLA_HEADS):
        r0 = h * HEAD_PAD
        x1 = qt[r0 + 64:r0 + 80, :]
        x2 = qt[r0 + 80:r0 + 96, :]
        blk = jnp.concatenate([qt[r0:r0 + 64, :], x1 * cos_t - x2 * sin_t, x1 * sin_t + x2 * cos_t, zpad], axis=0)
        qt_ref[0, r0:r0 + HEAD_PAD, :] = blk.astype(_BF16)
    kvn = _rms(kv_lat, kvn_ref[...]).astype(_BF16)
    kn = _dot(kvn, wuk_ref[...])
    lane = lax.broadcasted_iota(jnp.int32, kr.shape, 1)
    swapped = jnp.where(lane < 16, pltpu.roll(kr, 112, 1), pltpu.roll(kr, 16, 1))
    kpe = kr * ccs_ref[...] + swapped * sn_ref[...]
    kpe = pltpu.roll(kpe, 64, 1)
    k_ref[0] = (kn + jnp.concatenate([kpe] * MLA_HEADS, axis=1)).astype(_BF16)
    vt = _dot_nt(wuvt_ref[...], kvn).astype(_BF16)
    extra = (lax.broadcasted_iota(jnp.int32, (VT_ROWS - MLA_V, tm), 0) == 0).astype(_BF16)
    for h in range(MLA_HEADS):
        vt_ref[0, h * VT_ROWS:h * VT_ROWS + MLA_V, :] = vt[h * MLA_V:(h + 1) * MLA_V, :]
        vt_ref[0, h * VT_ROWS + MLA_V:(h + 1) * VT_ROWS, :] = extra
    kb_ref[0] = proj[:, 512:640].astype(_BF16)
    vbt_ref[0] = _dot_nt(wvbt_ref[...], xb).astype(_BF16)
    qbt = (_dot_nt(wqbt_ref[...], xb) * (SWA_D ** -0.5)).astype(_BF16)
    zhalf = jnp.zeros((SWA_D, tm), _BF16)
    for h in range(SWA_HEADS):
        g = h // SWA_REP
        real = qbt[h * SWA_D:(h + 1) * SWA_D, :]
        qbt_ref[0, h * LANE + g * SWA_D:h * LANE + (g + 1) * SWA_D, :] = real
        qbt_ref[0, h * LANE + (1 - g) * SWA_D:h * LANE + (2 - g) * SWA_D, :] = zhalf


def _attn_inproj(x2d, w, wqbt, wvbt, qn, wuqt, kvn, wuk, wuvt, cos_t, sin_t, ccs, sn, bsz, seq, tm):
    nst = seq // tm
    const = lambda shape: pl.BlockSpec(shape, lambda b, i: (0,) * len(shape))
    out_shape = (
        jax.ShapeDtypeStruct((bsz, MLA_HEADS * HEAD_PAD, seq), _BF16),
        jax.ShapeDtypeStruct((bsz, seq, MLA_HEADS * HEAD_PAD), _BF16),
        jax.ShapeDtypeStruct((bsz, MLA_HEADS * VT_ROWS, seq), _BF16),
        jax.ShapeDtypeStruct((bsz, SWA_HEADS * LANE, seq), _BF16),
        jax.ShapeDtypeStruct((bsz, seq, SWA_KV_HEADS * SWA_D), _BF16),
        jax.ShapeDtypeStruct((bsz, SWA_KV_HEADS * SWA_D, seq), _BF16),
    )
    return pl.pallas_call(
        _attn_inproj_kernel,
        out_shape=out_shape,
        grid=(bsz, nst),
        in_specs=[
            pl.BlockSpec((tm, x2d.shape[1]), lambda b, i: (b * nst + i, 0)),
            const(w.shape), const(wqbt.shape), const(wvbt.shape), const(qn.shape), const(wuqt.shape),
            const(kvn.shape), const(wuk.shape), const(wuvt.shape),
            pl.BlockSpec((16, tm), lambda b, i: (0, i)),
            pl.BlockSpec((16, tm), lambda b, i: (0, i)),
            pl.BlockSpec((tm, LANE), lambda b, i: (i, 0)),
            pl.BlockSpec((tm, LANE), lambda b, i: (i, 0)),
        ],
        out_specs=(
            pl.BlockSpec((1, MLA_HEADS * HEAD_PAD, tm), lambda b, i: (b, 0, i)),
            pl.BlockSpec((1, tm, MLA_HEADS * HEAD_PAD), lambda b, i: (b, i, 0)),
            pl.BlockSpec((1, MLA_HEADS * VT_ROWS, tm), lambda b, i: (b, 0, i)),
            pl.BlockSpec((1, SWA_HEADS * LANE, tm), lambda b, i: (b, 0, i)),
            pl.BlockSpec((1, tm, SWA_KV_HEADS * SWA_D), lambda b, i: (b, i, 0)),
            pl.BlockSpec((1, SWA_KV_HEADS * SWA_D, tm), lambda b, i: (b, 0, i)),
        ),
        name="attn_inproj",
    )(x2d, w, wqbt, wvbt, qn, wuqt, kvn, wuk, wuvt, cos_t, sin_t, ccs, sn)


def _mla_flash_kernel(qt_ref, k_ref, vt_ref, ot_ref, *scratch, tk, nstream):
    st_sc = scratch[:2 * nstream]
    p_sc = scratch[2 * nstream:4 * nstream]
    tq = qt_ref.shape[2] // nstream
    nkv = k_ref.shape[1] // tk
    assert nkv % 2 == 0
    qts = [qt_ref[0, :, s * tq:(s + 1) * tq] for s in range(nstream)]

    def keys(j):
        return k_ref[0, pl.ds(pl.multiple_of(j * tk, tk), tk), :]

    def values(j):
        return vt_ref[0, :, pl.ds(pl.multiple_of(j * tk, tk), tk)]

    def phase(j, cur, state):
        nxt = 1 - cur
        kt = keys(jnp.minimum(j + 1, nkv - 1))
        vt = values(jnp.maximum(j - 1, 0))
        out = []
        for s in range(nstream):
            m, acc = state[s]
            st_sc[2 * s + nxt][...] = _dot(kt, qts[s])
            pv = _dot(vt, p_sc[2 * s + nxt][...])
            st = st_sc[2 * s + cur][...]
            m_new = jnp.maximum(m, jnp.max(st, axis=0, keepdims=True))
            alpha = jnp.exp2(m - m_new)
            p_sc[2 * s + cur][...] = jnp.exp2(st - m_new).astype(_BF16)
            out.append((m_new, (acc + pv) * alpha))
        return out

    def body(jj, state):
        state = phase(2 * jj, 0, state)
        return phase(2 * jj + 1, 1, state)

    k0 = keys(0)
    state = []
    for s in range(nstream):
        st_sc[2 * s][...] = _dot(k0, qts[s])
        p_sc[2 * s + 1][...] = jnp.zeros((tk, tq), _BF16)
        state.append((jnp.full((1, tq), -jnp.inf, _F32), jnp.zeros((VT_ROWS, tq), _F32)))
    state = lax.fori_loop(0, nkv // 2, body, state)
    v_last = values(nkv - 1)
    for s in range(nstream):
        acc = state[s][1] + _dot(v_last, p_sc[2 * s + 1][...])
        ot_ref[0, :, s * tq:(s + 1) * tq] = (acc[0:MLA_V, :] / acc[MLA_V:MLA_V + 1, :]).astype(ot_ref.dtype)


def _mla_flash(qt, k, vt, tq, tk, nstream):
    bsz, _, seq = qt.shape
    tqs = tq * nstream
    scratch = [pltpu.VMEM((tk, tq), _F32)] * (2 * nstream) + [pltpu.VMEM((tk, tq), _BF16)] * (2 * nstream)
    return pl.pallas_call(
        functools.partial(_mla_flash_kernel, tk=tk, nstream=nstream),
        out_shape=jax.ShapeDtypeStruct((bsz, MLA_HEADS * MLA_V, seq), _BF16),
        grid=(bsz, MLA_HEADS, seq // tqs),
        in_specs=[
            pl.BlockSpec((1, HEAD_PAD, tqs), lambda b, h, i: (b, h, i)),
            pl.BlockSpec((1, seq, HEAD_PAD), lambda b, h, i: (b, 0, h)),
            pl.BlockSpec((1, VT_ROWS, seq), lambda b, h, i: (b, h, 0)),
        ],
        out_specs=pl.BlockSpec((1, MLA_V, tqs), lambda b, h, i: (b, h, i)),
        scratch_shapes=scratch,
        name="mla_flash",
    )(qt, k, vt)


def _swa_kernel(relb_ref, sinks_ref, qt_ref, k_ref, vt_ref, bucket_ref, o_ref, bias_sc, *, nblk, seq):
    first = jnp.logical_and(pl.program_id(0) == 0, pl.program_id(1) == 0)

    @pl.when(first)
    def _build_bias():
        bucket = bucket_ref[...]
        for h in range(SWA_HEADS):
            acc = jnp.full(bucket.shape, NEG_BIG, _F32)
            for bk in range(REL_BUCKETS):
                acc = jnp.where(bucket == bk, relb_ref[bk, h], acc)
            bias_sc[h] = acc

    j = pl.program_id(1)
    krow = lax.broadcasted_iota(jnp.int32, (3 * BLOCK, 1), 0)

    def block(u, c):
        n = j * nblk + u
        c0 = pl.multiple_of(u * BLOCK, BLOCK)
        w0 = pl.multiple_of(n * BLOCK, BLOCK)
        key_pos = n * BLOCK - BLOCK + krow
        emask = jnp.where(jnp.logical_and(key_pos >= 0, key_pos < seq), 0.0, NEG_BIG).astype(_F32)
        kw = k_ref[0, pl.ds(w0, 3 * BLOCK), :]
        vw = vt_ref[0, :, pl.ds(w0, 3 * BLOCK)]
        for g in range(SWA_KV_HEADS):
            heads = range(g * SWA_REP, (g + 1) * SWA_REP)
            qs = jnp.concatenate([qt_ref[0, h * LANE:(h + 1) * LANE, pl.ds(c0, BLOCK)] for h in heads], axis=1)
            bias = jnp.concatenate([bias_sc[h] for h in heads], axis=1)
            sink = jnp.concatenate([jnp.full((1, BLOCK), sinks_ref[h], _F32) for h in heads], axis=1)
            st = _dot(kw, qs) + bias + emask
            m = jnp.maximum(jnp.max(st, axis=0, keepdims=True), sink)
            p = jnp.exp(st - m)
            den = jnp.sum(p, axis=0, keepdims=True) + jnp.exp(sink - m)
            ot = _dot(vw[g * SWA_D:(g + 1) * SWA_D, :], p.astype(_BF16)) / den
            for r, h in enumerate(heads):
                o_ref[0, h * SWA_D:(h + 1) * SWA_D, pl.ds(c0, BLOCK)] = ot[:, r * BLOCK:(r + 1) * BLOCK].astype(o_ref.dtype)
        return c

    def pair(t, c):
        block(2 * t, c)
        return block(2 * t + 1, c)

    def quad(t, c):
        pair(2 * t, c)
        return pair(2 * t + 1, c)

    if nblk % 4 == 0:
        lax.fori_loop(0, nblk // 4, quad, 0)
    elif nblk % 2 == 0:
        lax.fori_loop(0, nblk // 2, pair, 0)
    else:
        lax.fori_loop(0, nblk, block, 0)


def _swa(qbt, kpad, vtpad, bucket_t, rel_bias, sinks, bsz, seq, nblk):
    nsteps = seq // (nblk * BLOCK)
    cols = nblk * BLOCK
    return pl.pallas_call(
        functools.partial(_swa_kernel, nblk=nblk, seq=seq),
        out_shape=jax.ShapeDtypeStruct((bsz, SWA_HEADS * SWA_D, seq), _BF16),
        grid=(bsz, nsteps),
        in_specs=[
            pl.BlockSpec(memory_space=pltpu.SMEM),
            pl.BlockSpec(memory_space=pltpu.SMEM),
            pl.BlockSpec((1, SWA_HEADS * LANE, cols), lambda b, j: (b, 0, j)),
            pl.BlockSpec((1, seq + 2 * BLOCK, SWA_KV_HEADS * SWA_D), lambda b, j: (b, 0, 0)),
            pl.BlockSpec((1, SWA_KV_HEADS * SWA_D, seq + 2 * BLOCK), lambda b, j: (b, 0, 0)),
            pl.BlockSpec((3 * BLOCK, BLOCK), lambda b, j: (0, 0)),
        ],
        out_specs=pl.BlockSpec((1, SWA_HEADS * SWA_D, cols), lambda b, j: (b, 0, j)),
        scratch_shapes=[pltpu.VMEM((SWA_HEADS, 3 * BLOCK, BLOCK), _F32)],
        compiler_params=pltpu.CompilerParams(dimension_semantics=("arbitrary", "arbitrary")),
        name="swa",
    )(rel_bias, sinks, qbt, kpad, vtpad, bucket_t)


def _route(logits_t, rbias):
    sc = jax.nn.sigmoid(logits_t)
    bz = sc + rbias
    s_rows = [sc[e:e + 1, :] for e in range(N_EXPERTS)]
    b_rows = [bz[e:e + 1, :] for e in range(N_EXPERTS)]
    gsel = None
    best = None
    for g in range(N_GROUPS):
        r = b_rows[4 * g:4 * g + 4]
        gs = r[0] + r[1]
        for (i, k) in ((0, 2), (0, 3), (1, 2), (1, 3), (2, 3)):
            gs = jnp.maximum(gs, r[i] + r[k])
        if g == 0:
            gsel = jnp.zeros(gs.shape, jnp.int32)
            best = gs
        else:
            better = gs > best
            gsel = jnp.where(better, g, gsel)
            best = jnp.where(better, gs, best)

    def pick(rows, k):
        out = rows[12 + k]
        for g in (2, 1, 0):
            out = jnp.where(gsel == g, rows[4 * g + k], out)
        return out

    v = [pick(b_rows, k) for k in range(4)]
    s = [pick(s_rows, k) for k in range(4)]
    i1 = jnp.zeros(gsel.shape, jnp.int32)
    m1 = v[0]
    w1 = s[0]
    for k in range(1, 4):
        gt = v[k] > m1
        i1 = jnp.where(gt, k, i1)
        m1 = jnp.where(gt, v[k], m1)
        w1 = jnp.where(gt, s[k], w1)
    i2 = jnp.full(gsel.shape, -1, jnp.int32)
    m2 = jnp.full(m1.shape, -jnp.inf, _F32)
    w2 = jnp.zeros(m1.shape, _F32)
    for k in range(4):
        ok = jnp.logical_and(i1 != k, jnp.logical_or(i2 < 0, v[k] > m2))
        i2 = jnp.where(ok, k, i2)
        m2 = jnp.where(ok, v[k], m2)
        w2 = jnp.where(ok, s[k], w2)
    tot = w1 + w2
    g1 = w1 / tot
    g2 = w2 / tot
    first_lo = i1 < i2
    lo = jnp.where(first_lo, i1, i2)
    hi = jnp.where(first_lo, i2, i1)
    pair = jnp.where(lo == 0, hi - 1, jnp.where(lo == 1, hi + 1, 5))
    cls = gsel * 6 + pair
    return cls, jnp.where(first_lo, g1, g2), jnp.where(first_lo, g2, g1)


def _ln_router_store(z, g_ref, b_ref, rwt_ref, rb_ref, h_ref, cls_ref):
    h = _layer_norm(z, g_ref[...], b_ref[...])
    tm = h.shape[0]
    h_hi = h.astype(_BF16)
    h_lo = (h - h_hi.astype(_F32)).astype(_BF16)
    rw = rwt_ref[...]
    rw_hi = rw.astype(_BF16)
    rw_lo = (rw - rw_hi.astype(_F32)).astype(_BF16)
    part = _dot_nt(jnp.concatenate([rw_hi, rw_lo], axis=0), h_hi)
    logits_t = part[0:N_EXPERTS] + part[N_EXPERTS:2 * N_EXPERTS] + _dot_nt(rw_hi, h_lo)
    cls, g_lo, g_hi = _route(logits_t, rb_ref[...])
    rows = jnp.concatenate([g_lo, g_hi, jnp.zeros((GATE_LANES - 2, tm), _F32)], axis=0)
    d = h.shape[1]
    h_ref[:, 0:d] = h
    h_ref[:, d:d + GATE_LANES] = rows.T
    cls_ref[0] = cls


def _outproj_ln_router_kernel(ot_ref, ob_ref, x_ref, w_ref, g_ref, b_ref, rwt_ref, rb_ref, h_ref, cls_ref):
    heads_t = jnp.concatenate([ot_ref[0], ob_ref[0]], axis=0)
    mixed = lax.dot_general(heads_t, w_ref[...], _TN_DIMS, preferred_element_type=_F32)
    z = ALPHA * x_ref[...] + mixed
    _ln_router_store(z, g_ref, b_ref, rwt_ref, rb_ref, h_ref, cls_ref)


def _outproj_ln_router(ot, ob, x2d, w, g, b, rwt, rb, bsz, seq, tm):
    nst = seq // tm
    d = x2d.shape[1]
    const = lambda shape: pl.BlockSpec(shape, lambda bb, i: (0,) * len(shape))
    return pl.pallas_call(
        _outproj_ln_router_kernel,
        out_shape=(jax.ShapeDtypeStruct((bsz * seq, d + GATE_LANES), _F32),
                   jax.ShapeDtypeStruct((bsz * nst, 1, tm), jnp.int32)),
        grid=(bsz, nst),
        in_specs=[
            pl.BlockSpec((1, ot.shape[1], tm), lambda bb, i: (bb, 0, i)),
            pl.BlockSpec((1, ob.shape[1], tm), lambda bb, i: (bb, 0, i)),
            pl.BlockSpec((tm, d), lambda bb, i: (bb * nst + i, 0)),
            const(w.shape), const(g.shape), const(b.shape), const(rwt.shape), const(rb.shape),
        ],
        out_specs=(pl.BlockSpec((tm, d + GATE_LANES), lambda bb, i: (bb * nst + i, 0)),
                   pl.BlockSpec((1, 1, tm), lambda bb, i: (bb * nst + i, 0, 0))),
        name="outproj_ln_router",
    )(ot, ob, x2d, w, g, b, rwt, rb)


_SUBLANES = 8
_SCAN_RADIX = 4
_DMA_CHUNK = 32


def _moe_kernel(ea_ref, eb_ref, nv_ref, off_ref, src_ref, nsrc_ref, h_hbm, w1a, w3a, w2a, w1b, w3b, w2b, g_ref, b_ref,
                out_hbm, xbuf, obuf, w13a, w2a_bf, w13b, w2b_bf, sem_in, sem_out):
    i = pl.program_id(0)
    ntiles = pl.num_programs(0)
    tm = xbuf.shape[1] * _SUBLANES
    d = obuf.shape[2]
    nv = nv_ref[i]
    nv_prev = jnp.where(i > 0, nv_ref[jnp.maximum(i - 1, 0)], 0)
    nv_next = jnp.where(i + 1 < ntiles, nv_ref[jnp.minimum(i + 1, ntiles - 1)], 0)
    slot = lax.rem(i, 2)
    shift = lax.rem(off_ref[i], LANE)
    shift_next = lax.rem(off_ref[jnp.minimum(i + 1, ntiles - 1)], LANE)

    def hbm_row(ref, idx):
        return ref.at[lax.shift_right_logical(idx, 3), jnp.bitwise_and(idx, _SUBLANES - 1)]

    def gather_start(idx_ref, sh, s):
        for r in range(tm):
            idx = idx_ref[sh + r]
            pltpu.make_async_copy(hbm_row(h_hbm, idx), xbuf.at[s, r // _SUBLANES, r % _SUBLANES], sem_in.at[s]).start()

    def scatter_wait(n):
        for bit in range(tm.bit_length()):
            rows = 1 << bit

            @pl.when(jnp.bitwise_and(n, rows) != 0)
            def _():
                if rows >= _SUBLANES:
                    grp = pl.ds(0, rows // _SUBLANES)
                    pltpu.make_async_copy(obuf.at[grp], out_hbm.at[grp], sem_out).wait()
                else:
                    pltpu.make_async_copy(obuf.at[0, pl.ds(0, rows)], out_hbm.at[0, pl.ds(0, rows)], sem_out).wait()

    @pl.when(jnp.logical_and(i == 0, nv > 0))
    def _prologue():
        gather_start(src_ref, shift, 0)

    @pl.when(nv_next > 0)
    def _prefetch():
        gather_start(nsrc_ref, shift_next, 1 - slot)

    prev_i = jnp.maximum(i - 1, 0)
    dff = w2a.shape[0]

    def refresh(w1, w3, w2, w13_sc, w2_sc):
        w13_sc[:, 0:dff] = w1[...].astype(_BF16)
        w13_sc[:, dff:2 * dff] = w3[...].astype(_BF16)
        w2_sc[...] = w2[...].astype(_BF16)

    @pl.when(jnp.logical_and(nv > 0, jnp.logical_or(i == 0, ea_ref[i] != ea_ref[prev_i])))
    def _refresh_a():
        refresh(w1a, w3a, w2a, w13a, w2a_bf)

    @pl.when(jnp.logical_and(nv > 0, jnp.logical_or(i == 0, eb_ref[i] != eb_ref[prev_i])))
    def _refresh_b():
        refresh(w1b, w3b, w2b, w13b, w2b_bf)

    @pl.when(nv > 0)
    def _tile():
        pltpu.make_async_copy(h_hbm.at[pl.ds(0, tm // _SUBLANES)], xbuf.at[slot], sem_in.at[slot]).wait()
        def compute(rows):
            xg = xbuf[slot, 0:rows // _SUBLANES].reshape(rows, d + GATE_LANES)
            x = xg[:, 0:d]
            gates = xg[:, d:d + GATE_LANES]
            ga = gates[:, 0:1]
            gb = gates[:, 1:2]
            xb = x.astype(_BF16)

            def expert(w13, w2):
                uv = _dot(xb, w13[...])
                u = uv[:, 0:dff]
                hh = u * _sigmoid(u) * uv[:, dff:2 * dff]
                return _dot(hh.astype(_BF16), w2[...])

            y = ga * expert(w13a, w2a_bf) + gb * expert(w13b, w2b_bf)
            z = _layer_norm(ALPHA * x + y, g_ref[...], b_ref[...])

            @pl.when(nv_prev > 0)
            def _drain_prev():
                scatter_wait(nv_prev)

            obuf[0:rows // _SUBLANES] = z.reshape(rows // _SUBLANES, _SUBLANES, d)

        @pl.when(nv > tm // 2)
        def _full():
            compute(tm)

        @pl.when(nv <= tm // 2)
        def _half():
            compute(tm // 2)

        for c in range(tm // _DMA_CHUNK):
            @pl.when(nv >= (c + 1) * _DMA_CHUNK)
            def _chunk():
                for r in range(c * _DMA_CHUNK, (c + 1) * _DMA_CHUNK):
                    idx = src_ref[shift + r]
                    pltpu.make_async_copy(obuf.at[r // _SUBLANES, r % _SUBLANES], hbm_row(out_hbm, idx),
                                          sem_out).start()

        def group(gi, c):
            for u in range(_SUBLANES):
                idx = src_ref[shift + gi * _SUBLANES + u]
                pltpu.make_async_copy(obuf.at[gi, u], hbm_row(out_hbm, idx), sem_out).start()
            return c

        def single(r, c):
            idx = src_ref[shift + r]
            pltpu.make_async_copy(hbm_row(obuf, r), hbm_row(out_hbm, idx), sem_out).start()
            return c

        ngroups = lax.shift_right_logical(nv, 3)
        lax.fori_loop((nv // _DMA_CHUNK) * (_DMA_CHUNK // _SUBLANES), ngroups, group, 0)
        lax.fori_loop(ngroups * _SUBLANES, nv, single, 0)

        @pl.when(nv_next == 0)
        def _drain_last():
            scatter_wait(nv)


def _moe(he, ea, eb, nvalid, off, order, w1, w3, w2, g, b, n_tokens, d):
    ntiles = ea.shape[0]
    tm = MOE_TM
    dff = w2.shape[1]
    wspec_a = lambda shape: pl.BlockSpec((None,) + shape, lambda i, ea_r, eb_r, nv_r, off_r: (ea_r[i], 0, 0))
    wspec_b = lambda shape: pl.BlockSpec((None,) + shape, lambda i, ea_r, eb_r, nv_r, off_r: (eb_r[i], 0, 0))

    def window(nxt):
        def start(i, ea_r, eb_r, nv_r, off_r):
            o = off_r[jnp.minimum(i + nxt, ntiles - 1)]
            return (pl.multiple_of((o // LANE) * LANE, LANE),)

        return pl.BlockSpec((pl.Element(2 * tm),), start, memory_space=pltpu.SMEM)

    grid_spec = pltpu.PrefetchScalarGridSpec(
        num_scalar_prefetch=4,
        grid=(ntiles,),
        in_specs=[
            window(0), window(1),
            pl.BlockSpec(memory_space=pl.ANY),
            wspec_a((d, dff)), wspec_a((d, dff)), wspec_a((dff, d)),
            wspec_b((d, dff)), wspec_b((d, dff)), wspec_b((dff, d)),
            pl.BlockSpec((1, d), lambda i, *_: (0, 0)),
            pl.BlockSpec((1, d), lambda i, *_: (0, 0)),
        ],
        out_specs=pl.BlockSpec(memory_space=pl.ANY),
        scratch_shapes=[
            pltpu.VMEM((2, tm // _SUBLANES, _SUBLANES, d + GATE_LANES), _F32),
            pltpu.VMEM((tm // _SUBLANES, _SUBLANES, d), _F32),
            pltpu.VMEM((d, 2 * dff), _BF16), pltpu.VMEM((dff, d), _BF16),
            pltpu.VMEM((d, 2 * dff), _BF16), pltpu.VMEM((dff, d), _BF16),
            pltpu.SemaphoreType.DMA((2,)),
            pltpu.SemaphoreType.DMA,
        ],
    )
    he3 = he.reshape(n_tokens // _SUBLANES, _SUBLANES, d + GATE_LANES)
    out = pl.pallas_call(
        _moe_kernel,
        out_shape=jax.ShapeDtypeStruct((n_tokens // _SUBLANES, _SUBLANES, d), _F32),
        grid_spec=grid_spec,
        compiler_params=pltpu.CompilerParams(dimension_semantics=("arbitrary",)),
        name="moe",
    )(ea, eb, nvalid, off, order, order, he3, w1, w3, w2, w1, w3, w2, g, b)
    return out.reshape(n_tokens, d)


def _moe_plan(cls, n_tokens):
    tm = MOE_TM
    ntiles = n_tokens // tm + N_CLASSES
    cls = cls.reshape(n_tokens)
    order = jnp.argsort(cls, stable=True).astype(jnp.int32)
    counts = jnp.sum((cls[:, None] == jnp.arange(N_CLASSES, dtype=jnp.int32)[None, :]).astype(jnp.int32), axis=0)
    cstart = jnp.cumsum(counts) - counts
    ptiles = (counts + tm - 1) // tm
    tend = jnp.cumsum(ptiles)
    tstart = tend - ptiles
    tile = jnp.arange(ntiles, dtype=jnp.int32)
    tcls = jnp.minimum(jnp.sum((tile[:, None] >= tend[None, :]).astype(jnp.int32), axis=1), N_CLASSES - 1)
    used = tile < tend[-1]
    nvalid = jnp.where(used, jnp.clip(counts[tcls] - (tile - tstart[tcls]) * tm, 0, tm), 0).astype(jnp.int32)
    off = jnp.where(used, cstart[tcls] + (tile - tstart[tcls]) * tm, 0).astype(jnp.int32)
    order = jnp.concatenate([order, jnp.zeros((2 * tm,), jnp.int32)])
    pair_lo = jnp.array([0, 0, 0, 1, 1, 2], jnp.int32)
    pair_hi = jnp.array([1, 2, 3, 2, 3, 3], jnp.int32)
    ea = (4 * (tcls // 6) + pair_lo[tcls % 6]).astype(jnp.int32)
    eb = (4 * (tcls // 6) + pair_hi[tcls % 6]).astype(jnp.int32)
    return ea, eb, nvalid, off, order


def _scan_slabs(a_refs, b_refs, c_refs, h_in, reverse):
    nlev = len(a_refs)
    ks = list(range(_SCAN_RADIX))
    if reverse:
        ks.reverse()
    for l in range(nlev - 1):
        grp = a_refs[l].shape[1] // _SCAN_RADIX
        p = q = None
        for k in ks:
            sl = pl.ds(k, grp, stride=_SCAN_RADIX)
            a = a_refs[l][:, sl, :]
            b = b_refs[l][:, sl, :]
            if p is None:
                p, q = a, b
            else:
                p, q = a * p, a * q + b
                a_refs[l][:, sl, :] = p
                b_refs[l][:, sl, :] = q
        a_refs[l + 1][...] = p
        b_refs[l + 1][...] = q
    rows = a_refs[-1].shape[1]
    a = a_refs[-1][...]
    b = b_refs[-1][...]
    out = [None] * rows
    h = h_in
    for r in (range(rows - 1, -1, -1) if reverse else range(rows)):
        h = a[:, r:r + 1, :] * h + b[:, r:r + 1, :]
        out[r] = h
    h_out = h
    b_refs[-1][...] = jnp.concatenate(out, axis=1)
    edge = jnp.broadcast_to(h_in, (h_in.shape[0], _SUBLANES, h_in.shape[2]))
    for l in range(nlev - 2, -1, -1):
        grp = a_refs[l].shape[1] // _SCAN_RADIX
        c = c_refs[l]
        c[:, _SUBLANES:_SUBLANES + grp, :] = b_refs[l + 1][...]
        if reverse:
            c[:, _SUBLANES + grp:2 * _SUBLANES + grp, :] = edge
            cin = c[:, _SUBLANES + 1:_SUBLANES + 1 + grp, :]
        else:
            c[:, 0:_SUBLANES, :] = edge
            cin = c[:, _SUBLANES - 1:_SUBLANES - 1 + grp, :]
        for k in ks:
            sl = pl.ds(k, grp, stride=_SCAN_RADIX)
            b_refs[l][:, sl, :] = a_refs[l][:, sl, :] * cin + b_refs[l][:, sl, :]
    return h_out


def _rglru_kernel(*refs, reverse, final, ntile):
    if final:
        (src_ref, prev_ref, next_ref, cw_ref, cb_ref, wa_ref, ba_ref, wx_ref, bx_ref, lam_ref,
         hf_ref, wgate_ref, res_ref, wout_ref, g_ref, b_ref, rwt_ref, rb_ref,
         h_ref, cls_ref, xe_sc, carry_sc, *scan_sc) = refs
    else:
        (src_ref, prev_ref, next_ref, cw_ref, cb_ref, wa_ref, ba_ref, wx_ref, bx_ref, lam_ref, wxr_ref,
         hout_ref, xr_out_ref, xe_sc, carry_sc, *scan_sc) = refs
    nlev = (len(scan_sc) + 1) // 3
    a_refs, b_refs, c_refs = scan_sc[:nlev], scan_sc[nlev:2 * nlev], scan_sc[2 * nlev:]
    i = pl.program_id(1)
    ti = (ntile - 1 - i) if reverse else i
    tt = src_ref.shape[0]

    @pl.when(i == 0)
    def _reset():
        carry_sc[...] = jnp.zeros_like(carry_sc)

    prev = jnp.where(ti > 0, prev_ref[...], 0.0)
    nxt = jnp.where(ti < ntile - 1, next_ref[...], 0.0)
    xin = jnp.concatenate([prev, src_ref[...], nxt], axis=0)
    if final:
        xe = xin
    else:
        xe = _dot(xin.astype(_BF16), wxr_ref[...])
        xr_out_ref[...] = xe[_SUBLANES:_SUBLANES + tt, :]
    for n in range(LRU_BLOCKS):
        xe_sc[n] = xe[:, n * LRU_BW:(n + 1) * LRU_BW]
    cw = cw_ref[...]
    cb = cb_ref[...]
    half_decay = (0.5 * LRU_C) * -jnp.log(1.0 + jnp.exp(-lam_ref[...]))
    for n in range(LRU_BLOCKS):
        sl = slice(n * LRU_BW, (n + 1) * LRU_BW)
        xcn = cb[:, sl]
        for k in range(cw.shape[0]):
            xcn = xcn + cw[k:k + 1, sl] * xe_sc[n, _SUBLANES - CONV_LEFT + k:_SUBLANES - CONV_LEFT + k + tt, :]
        xb = xcn.astype(_BF16)
        tr = jnp.tanh(_dot(xb, wa_ref[n]) + ba_ref[:, sl])
        ig = 0.5 * jnp.tanh(_dot(xb, wx_ref[n]) + bx_ref[:, sl]) + 0.5
        log_a = tr * half_decay[:, sl] + half_decay[:, sl]
        th = jnp.tanh(log_a)
        num = -2.0 * th
        scale = jnp.where(num > 0.0, num * lax.rsqrt(num * (1.0 - th)), 0.0)
        a_refs[0][n] = jnp.exp(log_a)
        b_refs[0][n] = scale * (ig * xcn)
    carry_sc[...] = _scan_slabs(a_refs, b_refs, c_refs, carry_sc[...], reverse)
    if final:
        gate = _dot(res_ref[...].astype(_BF16), wgate_ref[...])
    ys = []
    for n in range(LRU_BLOCKS):
        sl = slice(n * LRU_BW, (n + 1) * LRU_BW)
        h = b_refs[0][n]
        if final:
            ys.append((hf_ref[:, sl] + h) * jax.nn.gelu(gate[:, sl], approximate=True))
        else:
            hout_ref[:, sl] = h
    if final:
        y = jnp.concatenate(ys, axis=1).astype(_BF16)
        z = ALPHA * res_ref[...] + _dot(y, wout_ref[...])
        _ln_router_store(z, g_ref, b_ref, rwt_ref, rb_ref, h_ref, cls_ref)


def _rglru(src, cw, cb, wa, ba, wx, bx, lam, bsz, seq, tt, reverse, wxr=None, tail=None):
    ntile = seq // tt
    width = cw.shape[1]
    final = tail is not None
    tidx = (lambda i: ntile - 1 - i) if reverse else (lambda i: i)
    row = lambda b, i: (b * ntile + tidx(i), 0)
    g8 = tt // 8
    nb8 = seq // 8

    def prev_map(b, i):
        return (b * nb8 + jnp.maximum(tidx(i) * g8 - 1, 0), 0)

    def next_map(b, i):
        return (b * nb8 + jnp.minimum((tidx(i) + 1) * g8, nb8 - 1), 0)

    const = lambda shape: pl.BlockSpec(shape, lambda b, i: (0,) * len(shape))
    swidth = src.shape[1]
    in_specs = [
        pl.BlockSpec((tt, swidth), row),
        pl.BlockSpec((8, swidth), prev_map),
        pl.BlockSpec((8, swidth), next_map),
        const(cw.shape), const(cb.shape), const(wa.shape), const(ba.shape), const(wx.shape), const(bx.shape),
        const(lam.shape),
    ]
    args = [src, src, src, cw, cb, wa, ba, wx, bx, lam]
    rows = [tt]
    while rows[-1] > _SUBLANES:
        assert rows[-1] % _SCAN_RADIX == 0
        rows.append(rows[-1] // _SCAN_RADIX)
    assert rows[-1] == _SUBLANES
    slab = lambda r: pltpu.VMEM((LRU_BLOCKS, r, LRU_BW), _F32)
    scratch = [slab(tt + 2 * _SUBLANES), pltpu.VMEM((LRU_BLOCKS, 1, LRU_BW), _F32)]
    scratch += [slab(r) for r in rows] * 2 + [slab(r + 2 * _SUBLANES) for r in rows[1:]]
    if final:
        hf, wgate, res, wout, g, b, rwt, rb = tail
        d = res.shape[1]
        in_specs += [pl.BlockSpec((tt, width), row), const(wgate.shape), pl.BlockSpec((tt, d), row),
                     const(wout.shape), const(g.shape), const(b.shape), const(rwt.shape), const(rb.shape)]
        args += [hf, wgate, res, wout, g, b, rwt, rb]
        out_shape = (jax.ShapeDtypeStruct((bsz * seq, d + GATE_LANES), _F32),
                     jax.ShapeDtypeStruct((bsz * ntile, 1, tt), jnp.int32))
        out_specs = (pl.BlockSpec((tt, d + GATE_LANES), row),
                     pl.BlockSpec((1, 1, tt), lambda b, i: (b * ntile + tidx(i), 0, 0)))
    else:
        in_specs.append(const(wxr.shape))
        args.append(wxr)
        out_shape = (jax.ShapeDtypeStruct((bsz * seq, width), _F32),) * 2
        out_specs = (pl.BlockSpec((tt, width), row),) * 2
    return pl.pallas_call(
        functools.partial(_rglru_kernel, reverse=reverse, final=final, ntile=ntile),
        out_shape=out_shape,
        grid=(bsz, ntile),
        in_specs=in_specs,
        out_specs=out_specs,
        scratch_shapes=scratch,
        compiler_params=pltpu.CompilerParams(dimension_semantics=("arbitrary", "arbitrary")),
        name="rglru_bwd_tail" if final else "rglru_fwd",
    )(*args)


def _rotary_tables(seq):
    half = MLA_ROPE // 2
    inv_freq = ROPE_THETA ** (-jnp.arange(half, dtype=_F32) / half)
    ang = jnp.arange(seq, dtype=_F32)[:, None] * inv_freq[None, :]
    return jnp.cos(ang), jnp.sin(ang)


def _t5_bucket(rel):
    n_side = REL_BUCKETS // 2
    max_exact = n_side // 2
    dist = jnp.abs(rel)
    far = max_exact + (jnp.log(jnp.maximum(dist, 1).astype(_F32) / max_exact)
                       / math.log(REL_MAX_DIST / max_exact) * (n_side - max_exact)).astype(jnp.int32)
    far = jnp.minimum(far, n_side - 1)
    return jnp.where(rel > 0, n_side, 0) + jnp.where(dist < max_exact, dist, far)


def _band_buckets():
    rel = jnp.arange(3 * BLOCK)[None, :] - BLOCK - jnp.arange(BLOCK)[:, None]
    ids = jnp.where(jnp.abs(rel) <= WINDOW, _t5_bucket(rel), REL_BUCKETS).astype(jnp.int32)
    return jnp.bitwise_and(ids, 2 * REL_BUCKETS - 1)


def _pick_tile(seq, want):
    t = min(want, seq)
    while seq % t:
        t //= 2
    return t


def kernel(x, rel_bias, router_w, router_bias, l0_w_in, l0_q_norm, l0_w_uq, l0_kv_norm, l0_w_ukv, l0_sinks, l0_w_out, l0_ln1_g, l0_ln1_b, l0_w1, l0_w3, l0_w2, l0_ln2_g, l0_ln2_b, l1_w_in, l1_conv_w, l1_conv_b, l1_wa_f, l1_ba_f, l1_wx_f, l1_bx_f, l1_lam_f, l1_wa_b, l1_ba_b, l1_wx_b, l1_bx_b, l1_lam_b, l1_w_out, l1_ln1_g, l1_ln1_b, l1_w1, l1_w3, l1_w2, l1_ln2_g, l1_ln2_b):
    bsz, seq, d = x.shape
    n_tok = bsz * seq
    assert seq % BLOCK == 0 and d == LRU_BLOCKS * LRU_BW and n_tok % MOE_TM == 0
    x2d = x.reshape(n_tok, d)
    row = lambda v: v.reshape(1, -1).astype(_F32)
    rwt = router_w.astype(_F32).T
    rb = router_bias.astype(_F32).reshape(N_EXPERTS, 1)

    o = np.cumsum([0, MLA_Q_RANK, MLA_KV_RANK, MLA_ROPE, SWA_HEADS * SWA_D, SWA_KV_HEADS * SWA_D,
                   SWA_KV_HEADS * SWA_D])
    w_q, w_kv, w_kr, w_qb, w_kb, w_vb = [l0_w_in[:, o[i]:o[i + 1]] for i in range(6)]
    w_in0 = jnp.concatenate([w_q, w_kv, jnp.pad(w_kr, ((0, 0), (0, LANE - MLA_ROPE))), w_kb], axis=1).astype(_BF16)
    wuq = l0_w_uq.reshape(MLA_Q_RANK, MLA_HEADS, MLA_QK)
    wuqt = jnp.pad(wuq, ((0, 0), (0, 0), (0, HEAD_PAD - MLA_QK))).reshape(MLA_Q_RANK, MLA_HEADS * HEAD_PAD).T
    wukv = l0_w_ukv.reshape(MLA_KV_RANK, MLA_HEADS, MLA_NOPE + MLA_V)
    wuk = jnp.pad(wukv[:, :, :MLA_NOPE], ((0, 0), (0, 0), (0, HEAD_PAD - MLA_NOPE))).reshape(MLA_KV_RANK, -1)
    wuvt = wukv[:, :, MLA_NOPE:].reshape(MLA_KV_RANK, MLA_HEADS * MLA_V).T
    cos, sin = _rotary_tables(seq)
    ccs = jnp.pad(jnp.concatenate([cos, cos], axis=1), ((0, 0), (0, LANE - MLA_ROPE)))
    sn = jnp.pad(jnp.concatenate([-sin, sin], axis=1), ((0, 0), (0, LANE - MLA_ROPE)))

    tm = _pick_tile(seq, ROW_TILE)
    qt, k, vt, qbt, kb, vbt = _attn_inproj(
        x2d, w_in0, w_qb.T.astype(_BF16), w_vb.T.astype(_BF16), row(l0_q_norm), wuqt.astype(_BF16),
        row(l0_kv_norm), wuk.astype(_BF16), wuvt.astype(_BF16), cos.T, sin.T, ccs, sn, bsz, seq, tm)
    nstream = max(1, min(FLASH_STREAMS, seq // FLASH_TQ))
    ot = _mla_flash(qt, k, vt, _pick_tile(seq // nstream, FLASH_TQ), _pick_tile(seq // 2, FLASH_TK), nstream)
    kpad = jnp.pad(kb, ((0, 0), (BLOCK, BLOCK), (0, 0)))
    vtpad = jnp.pad(vbt, ((0, 0), (0, 0), (BLOCK, BLOCK)))
    obt = _swa(qbt, kpad, vtpad, _band_buckets().T, rel_bias.astype(_F32), l0_sinks.astype(_F32), bsz, seq,
               _pick_tile(seq // BLOCK, SWA_BLOCKS))
    he, cls = _outproj_ln_router(ot, obt, x2d, l0_w_out.astype(_BF16), row(l0_ln1_g), row(l0_ln1_b), rwt, rb,
                                 bsz, seq, tm)
    ea, eb, nvalid, off, order = _moe_plan(cls, n_tok)
    h = _moe(he, ea, eb, nvalid, off, order, l0_w1, l0_w3, l0_w2,
             row(l0_ln2_g), row(l0_ln2_b), n_tok, d)

    tt = _pick_tile(seq, ROW_TILE)
    w_in1 = l1_w_in.astype(_BF16)
    w_gate, w_xr = w_in1[:, :d], w_in1[:, d:]
    cw = l1_conv_w.astype(_F32)
    cb = row(l1_conv_b)
    halfw = lambda w: (0.5 * w).astype(_BF16)
    halfb = lambda v: 0.5 * row(v)
    hf, xr = _rglru(h, cw, cb, halfw(l1_wa_f), halfb(l1_ba_f), halfw(l1_wx_f), halfb(l1_bx_f),
                    row(l1_lam_f), bsz, seq, tt, reverse=False, wxr=w_xr)
    he, cls = _rglru(xr, cw, cb, halfw(l1_wa_b), halfb(l1_ba_b), halfw(l1_wx_b), halfb(l1_bx_b),
                     row(l1_lam_b), bsz, seq, tt, reverse=True,
                     tail=(hf, w_gate, h, l1_w_out.astype(_BF16), row(l1_ln1_g), row(l1_ln1_b), rwt, rb))
    ea, eb, nvalid, off, order = _moe_plan(cls, n_tok)
    h = _moe(he, ea, eb, nvalid, off, order, l1_w1, l1_w3, l1_w2,
             row(l1_ln2_g), row(l1_ln2_b), n_tok, d)
    return h.reshape(bsz, seq, d)
```

```python
import functools
import math

import jax
import jax.numpy as jnp
import numpy as np
from jax import lax
from jax.experimental import pallas as pl
from jax.experimental.pallas import tpu as pltpu

MLA_HEADS = 8
MLA_Q_RANK = 256
MLA_KV_RANK = 128
MLA_NOPE = 64
MLA_ROPE = 32
MLA_V = 64
MLA_QK = MLA_NOPE + MLA_ROPE
ROPE_THETA = 10000.0
SWA_HEADS = 8
SWA_KV_HEADS = 2
SWA_REP = SWA_HEADS // SWA_KV_HEADS
SWA_D = 64
WINDOW = 128
BLOCK = 128
REL_BUCKETS = 32
REL_MAX_DIST = 128
LRU_BLOCKS = 8
LRU_BW = 128
LRU_C = 8.0
CONV_LEFT = 2
N_EXPERTS = 16
N_GROUPS = 4
EXPERTS_PER_GROUP = 4
N_CLASSES = N_GROUPS * 6
DEPTH = 2
ALPHA = (2.0 * DEPTH) ** 0.25
LN_EPS = 1e-5
RMS_EPS = 1e-6
NEG_BIG = -1e30

LANE = 128
HEAD_PAD = 128
GATE_LANES = 128
VT_ROWS = 80
_LOG2E = math.log2(math.e)

MOE_TM = 256
ROW_TILE = 512
FLASH_TQ = 256
FLASH_TK = 512
FLASH_STREAMS = 32
SWA_BLOCKS = 8

_F32 = jnp.float32
_BF16 = jnp.bfloat16
_NT_DIMS = (((1,), (1,)), ((), ()))
_TN_DIMS = (((0,), (0,)), ((), ()))


def _dot(a, b):
    return jnp.dot(a, b, preferred_element_type=_F32)


def _dot_nt(a, b):
    return lax.dot_general(a, b, _NT_DIMS, preferred_element_type=_F32)


def _sigmoid(x):
    return 0.5 * jnp.tanh(0.5 * x) + 0.5


def _rms(x, g):
    return x * lax.rsqrt(jnp.mean(jnp.square(x), -1, keepdims=True) + RMS_EPS) * g


def _layer_norm(x, g, b):
    mu = jnp.mean(x, -1, keepdims=True)
    xc = x - mu
    var = jnp.mean(jnp.square(xc), -1, keepdims=True)
    return xc * lax.rsqrt(var + LN_EPS) * g + b


def _attn_inproj_kernel(x_ref, w_ref, wqbt_ref, wvbt_ref, qn_ref, wuqt_ref, kvn_ref, wuk_ref, wuvt_ref,
                        cost_ref, sint_ref, ccs_ref, sn_ref,
                        qt_ref, k_ref, vt_ref, qbt_ref, kb_ref, vbt_ref):
    xb = x_ref[...].astype(_BF16)
    proj = _dot(xb, w_ref[...])
    q_lat = proj[:, 0:256]
    kv_lat = proj[:, 256:384]
    kr = proj[:, 384:512]
    qn = _rms(q_lat, qn_ref[...]).astype(_BF16)
    qt = _dot_nt(wuqt_ref[...], qn) * (MLA_QK ** -0.5 * _LOG2E)
    cos_t = cost_ref[...]
    sin_t = sint_ref[...]
    tm = qt.shape[1]
    zpad = jnp.zeros((HEAD_PAD - MLA_QK, tm), _F32)
    for h in range(MLA_HEADS):
        r0 = h * HEAD_PAD
        x1 = qt[r0 + 64:r0 + 80, :]
        x2 = qt[r0 + 80:r0 + 96, :]
        blk = jnp.concatenate([qt[r0:r0 + 64, :], x1 * cos_t - x2 * sin_t, x1 * sin_t + x2 * cos_t, zpad], axis=0)
        qt_ref[0, r0:r0 + HEAD_PAD, :] = blk.astype(_BF16)
    kvn = _rms(kv_lat, kvn_ref[...]).astype(_BF16)
    kn = _dot(kvn, wuk_ref[...])
    lane = lax.broadcasted_iota(jnp.int32, kr.shape, 1)
    swapped = jnp.where(lane < 16, pltpu.roll(kr, 112, 1), pltpu.roll(kr, 16, 1))
    kpe = kr * ccs_ref[...] + swapped * sn_ref[...]
    kpe = pltpu.roll(kpe, 64, 1)
    k_ref[0] = (kn + jnp.concatenate([kpe] * MLA_HEADS, axis=1)).astype(_BF16)
    vt = _dot_nt(wuvt_ref[...], kvn).astype(_BF16)
    extra = (lax.broadcasted_iota(jnp.int32, (VT_ROWS - MLA_V, tm), 0) == 0).astype(_BF16)
    for h in range(MLA_HEADS):
        vt_ref[0, h * VT_ROWS:h * VT_ROWS + MLA_V, :] = vt[h * MLA_V:(h + 1) * MLA_V, :]
        vt_ref[0, h * VT_ROWS + MLA_V:(h + 1) * VT_ROWS, :] = extra
    kb_ref[0] = proj[:, 512:640].astype(_BF16)
    vbt_ref[0] = _dot_nt(wvbt_ref[...], xb).astype(_BF16)
    qbt = (_dot_nt(wqbt_ref[...], xb) * (SWA_D ** -0.5)).astype(_BF16)
    zhalf = jnp.zeros((SWA_D, tm), _BF16)
    for h in range(SWA_HEADS):
        g = h // SWA_REP
        real = qbt[h * SWA_D:(h + 1) * SWA_D, :]
        qbt_ref[0, h * LANE + g * SWA_D:h * LANE + (g + 1) * SWA_D, :] = real
        qbt_ref[0, h * LANE + (1 - g) * SWA_D:h * LANE + (2 - g) * SWA_D, :] = zhalf


def _attn_inproj(x2d, w, wqbt, wvbt, qn, wuqt, kvn, wuk, wuvt, cos_t, sin_t, ccs, sn, bsz, seq, tm):
    nst = seq // tm
    const = lambda shape: pl.BlockSpec(shape, lambda b, i: (0,) * len(shape))
    out_shape = (
        jax.ShapeDtypeStruct((bsz, MLA_HEADS * HEAD_PAD, seq), _BF16),
        jax.ShapeDtypeStruct((bsz, seq, MLA_HEADS * HEAD_PAD), _BF16),
        jax.ShapeDtypeStruct((bsz, MLA_HEADS * VT_ROWS, seq), _BF16),
        jax.ShapeDtypeStruct((bsz, SWA_HEADS * LANE, seq), _BF16),
        jax.ShapeDtypeStruct((bsz, seq, SWA_KV_HEADS * SWA_D), _BF16),
        jax.ShapeDtypeStruct((bsz, SWA_KV_HEADS * SWA_D, seq), _BF16),
    )
    return pl.pallas_call(
        _attn_inproj_kernel,
        out_shape=out_shape,
        grid=(bsz, nst),
        in_specs=[
            pl.BlockSpec((tm, x2d.shape[1]), lambda b, i: (b * nst + i, 0)),
            const(w.shape), const(wqbt.shape), const(wvbt.shape), const(qn.shape), const(wuqt.shape),
            const(kvn.shape), const(wuk.shape), const(wuvt.shape),
            pl.BlockSpec((16, tm), lambda b, i: (0, i)),
            pl.BlockSpec((16, tm), lambda b, i: (0, i)),
            pl.BlockSpec((tm, LANE), lambda b, i: (i, 0)),
            pl.BlockSpec((tm, LANE), lambda b, i: (i, 0)),
        ],
        out_specs=(
            pl.BlockSpec((1, MLA_HEADS * HEAD_PAD, tm), lambda b, i: (b, 0, i)),
            pl.BlockSpec((1, tm, MLA_HEADS * HEAD_PAD), lambda b, i: (b, i, 0)),
            pl.BlockSpec((1, MLA_HEADS * VT_ROWS, tm), lambda b, i: (b, 0, i)),
            pl.BlockSpec((1, SWA_HEADS * LANE, tm), lambda b, i: (b, 0, i)),
            pl.BlockSpec((1, tm, SWA_KV_HEADS * SWA_D), lambda b, i: (b, i, 0)),
            pl.BlockSpec((1, SWA_KV_HEADS * SWA_D, tm), lambda b, i: (b, 0, i)),
        ),
        name="attn_inproj",
    )(x2d, w, wqbt, wvbt, qn, wuqt, kvn, wuk, wuvt, cos_t, sin_t, ccs, sn)


def _mla_flash_kernel(qt_ref, k_ref, vt_ref, ot_ref, *scratch, tk, nstream):
    st_sc = scratch[:2 * nstream]
    p_sc = scratch[2 * nstream:4 * nstream]
    tq = qt_ref.shape[2] // nstream
    nkv = k_ref.shape[1] // tk
    assert nkv % 2 == 0
    qts = [qt_ref[0, :, s * tq:(s + 1) * tq] for s in range(nstream)]

    def keys(j):
        return k_ref[0, pl.ds(pl.multiple_of(j * tk, tk), tk), :]

    def values(j):
        return vt_ref[0, :, pl.ds(pl.multiple_of(j * tk, tk), tk)]

    def phase(j, cur, state):
        nxt = 1 - cur
        kt = keys(jnp.minimum(j + 1, nkv - 1))
        vt = values(jnp.maximum(j - 1, 0))
        out = []
        for s in range(nstream):
            m, acc = state[s]
            st_sc[2 * s + nxt][...] = _dot(kt, qts[s])
            pv = _dot(vt, p_sc[2 * s + nxt][...])
            st = st_sc[2 * s + cur][...]
            m_new = jnp.maximum(m, jnp.max(st, axis=0, keepdims=True))
            alpha = jnp.exp2(m - m_new)
            p_sc[2 * s + cur][...] = jnp.exp2(st - m_new).astype(_BF16)
            out.append((m_new, (acc + pv) * alpha))
        return out

    def body(jj, state):
        state = phase(2 * jj, 0, state)
        return phase(2 * jj + 1, 1, state)

    k0 = keys(0)
    state = []
    for s in range(nstream):
        st_sc[2 * s][...] = _dot(k0, qts[s])
        p_sc[2 * s + 1][...] = jnp.zeros((tk, tq), _BF16)
        state.append((jnp.full((1, tq), -jnp.inf, _F32), jnp.zeros((VT_ROWS, tq), _F32)))
    state = lax.fori_loop(0, nkv // 2, body, state)
    v_last = values(nkv - 1)
    for s in range(nstream):
        acc = state[s][1] + _dot(v_last, p_sc[2 * s + 1][...])
        ot_ref[0, :, s * tq:(s + 1) * tq] = (acc[0:MLA_V, :] / acc[MLA_V:MLA_V + 1, :]).astype(ot_ref.dtype)


def _mla_flash(qt, k, vt, tq, tk, nstream):
    bsz, _, seq = qt.shape
    tqs = tq * nstream
    scratch = [pltpu.VMEM((tk, tq), _F32)] * (2 * nstream) + [pltpu.VMEM((tk, tq), _BF16)] * (2 * nstream)
    return pl.pallas_call(
        functools.partial(_mla_flash_kernel, tk=tk, nstream=nstream),
        out_shape=jax.ShapeDtypeStruct((bsz, MLA_HEADS * MLA_V, seq), _BF16),
        grid=(bsz, MLA_HEADS, seq // tqs),
        in_specs=[
            pl.BlockSpec((1, HEAD_PAD, tqs), lambda b, h, i: (b, h, i)),
            pl.BlockSpec((1, seq, HEAD_PAD), lambda b, h, i: (b, 0, h)),
            pl.BlockSpec((1, VT_ROWS, seq), lambda b, h, i: (b, h, 0)),
        ],
        out_specs=pl.BlockSpec((1, MLA_V, tqs), lambda b, h, i: (b, h, i)),
        scratch_shapes=scratch,
        name="mla_flash",
    )(qt, k, vt)


def _swa_kernel(relb_ref, sinks_ref, qt_ref, k_ref, vt_ref, bucket_ref, o_ref, bias_sc, *, nblk, seq):
    first = jnp.logical_and(pl.program_id(0) == 0, pl.program_id(1) == 0)

    @pl.when(first)
    def _build_bias():
        bucket = bucket_ref[...]
        for h in range(SWA_HEADS):
            acc = jnp.full(bucket.shape, NEG_BIG, _F32)
            for bk in range(REL_BUCKETS):
                acc = jnp.where(bucket == bk, relb_ref[bk, h], acc)
            bias_sc[h] = acc

    j = pl.program_id(1)
    krow = lax.broadcasted_iota(jnp.int32, (3 * BLOCK, 1), 0)

    def block(u, c):
        n = j * nblk + u
        c0 = pl.multiple_of(u * BLOCK, BLOCK)
        w0 = pl.multiple_of(n * BLOCK, BLOCK)
        key_pos = n * BLOCK - BLOCK + krow
        emask = jnp.where(jnp.logical_and(key_pos >= 0, key_pos < seq), 0.0, NEG_BIG).astype(_F32)
        kw = k_ref[0, pl.ds(w0, 3 * BLOCK), :]
        vw = vt_ref[0, :, pl.ds(w0, 3 * BLOCK)]
        for g in range(SWA_KV_HEADS):
            heads = range(g * SWA_REP, (g + 1) * SWA_REP)
            qs = jnp.concatenate([qt_ref[0, h * LANE:(h + 1) * LANE, pl.ds(c0, BLOCK)] for h in heads], axis=1)
            bias = jnp.concatenate([bias_sc[h] for h in heads], axis=1)
            sink = jnp.concatenate([jnp.full((1, BLOCK), sinks_ref[h], _F32) for h in heads], axis=1)
            st = _dot(kw, qs) + bias + emask
            m = jnp.maximum(jnp.max(st, axis=0, keepdims=True), sink)
            p = jnp.exp(st - m)
            den = jnp.sum(p, axis=0, keepdims=True) + jnp.exp(sink - m)
            ot = _dot(vw[g * SWA_D:(g + 1) * SWA_D, :], p.astype(_BF16)) / den
            for r, h in enumerate(heads):
                o_ref[0, h * SWA_D:(h + 1) * SWA_D, pl.ds(c0, BLOCK)] = ot[:, r * BLOCK:(r + 1) * BLOCK].astype(o_ref.dtype)
        return c

    def pair(t, c):
        block(2 * t, c)
        return block(2 * t + 1, c)

    def quad(t, c):
        pair(2 * t, c)
        return pair(2 * t + 1, c)

    if nblk % 4 == 0:
        lax.fori_loop(0, nblk // 4, quad, 0)
    elif nblk % 2 == 0:
        lax.fori_loop(0, nblk // 2, pair, 0)
    else:
        lax.fori_loop(0, nblk, block, 0)


def _swa(qbt, kpad, vtpad, bucket_t, rel_bias, sinks, bsz, seq, nblk):
    nsteps = seq // (nblk * BLOCK)
    cols = nblk * BLOCK
    return pl.pallas_call(
        functools.partial(_swa_kernel, nblk=nblk, seq=seq),
        out_shape=jax.ShapeDtypeStruct((bsz, SWA_HEADS * SWA_D, seq), _BF16),
        grid=(bsz, nsteps),
        in_specs=[
            pl.BlockSpec(memory_space=pltpu.SMEM),
            pl.BlockSpec(memory_space=pltpu.SMEM),
            pl.BlockSpec((1, SWA_HEADS * LANE, cols), lambda b, j: (b, 0, j)),
            pl.BlockSpec((1, seq + 2 * BLOCK, SWA_KV_HEADS * SWA_D), lambda b, j: (b, 0, 0)),
            pl.BlockSpec((1, SWA_KV_HEADS * SWA_D, seq + 2 * BLOCK), lambda b, j: (b, 0, 0)),
            pl.BlockSpec((3 * BLOCK, BLOCK), lambda b, j: (0, 0)),
        ],
        out_specs=pl.BlockSpec((1, SWA_HEADS * SWA_D, cols), lambda b, j: (b, 0, j)),
        scratch_shapes=[pltpu.VMEM((SWA_HEADS, 3 * BLOCK, BLOCK), _F32)],
        compiler_params=pltpu.CompilerParams(dimension_semantics=("arbitrary", "arbitrary")),
        name="swa",
    )(rel_bias, sinks, qbt, kpad, vtpad, bucket_t)


def _route(logits_t, rbias):
    sc = jax.nn.sigmoid(logits_t)
    bz = sc + rbias
    s_rows = [sc[e:e + 1, :] for e in range(N_EXPERTS)]
    b_rows = [bz[e:e + 1, :] for e in range(N_EXPERTS)]
    gsel = None
    best = None
    for g in range(N_GROUPS):
        r = b_rows[4 * g:4 * g + 4]
        gs = r[0] + r[1]
        for (i, k) in ((0, 2), (0, 3), (1, 2), (1, 3), (2, 3)):
            gs = jnp.maximum(gs, r[i] + r[k])
        if g == 0:
            gsel = jnp.zeros(gs.shape, jnp.int32)
            best = gs
        else:
            better = gs > best
            gsel = jnp.where(better, g, gsel)
            best = jnp.where(better, gs, best)

    def pick(rows, k):
        out = rows[12 + k]
        for g in (2, 1, 0):
            out = jnp.where(gsel == g, rows[4 * g + k], out)
        return out

    v = [pick(b_rows, k) for k in range(4)]
    s = [pick(s_rows, k) for k in range(4)]
    i1 = jnp.zeros(gsel.shape, jnp.int32)
    m1 = v[0]
    w1 = s[0]
    for k in range(1, 4):
        gt = v[k] > m1
        i1 = jnp.where(gt, k, i1)
        m1 = jnp.where(gt, v[k], m1)
        w1 = jnp.where(gt, s[k], w1)
    i2 = jnp.full(gsel.shape, -1, jnp.int32)
    m2 = jnp.full(m1.shape, -jnp.inf, _F32)
    w2 = jnp.zeros(m1.shape, _F32)
    for k in range(4):
        ok = jnp.logical_and(i1 != k, jnp.logical_or(i2 < 0, v[k] > m2))
        i2 = jnp.where(ok, k, i2)
        m2 = jnp.where(ok, v[k], m2)
        w2 = jnp.where(ok, s[k], w2)
    tot = w1 + w2
    g1 = w1 / tot
    g2 = w2 / tot
    first_lo = i1 < i2
    lo = jnp.where(first_lo, i1, i2)
    hi = jnp.where(first_lo, i2, i1)
    pair = jnp.where(lo == 0, hi - 1, jnp.where(lo == 1, hi + 1, 5))
    cls = gsel * 6 + pair
    return cls, jnp.where(first_lo, g1, g2), jnp.where(first_lo, g2, g1)


def _ln_router_store(z, g_ref, b_ref, rwt_ref, rb_ref, h_ref, cls_ref):
    h = _layer_norm(z, g_ref[...], b_ref[...])
    tm = h.shape[0]
    h_hi = h.astype(_BF16)
    h_lo = (h - h_hi.astype(_F32)).astype(_BF16)
    rw = rwt_ref[...]
    rw_hi = rw.astype(_BF16)
    rw_lo = (rw - rw_hi.astype(_F32)).astype(_BF16)
    part = _dot_nt(jnp.concatenate([rw_hi, rw_lo], axis=0), h_hi)
    logits_t = part[0:N_EXPERTS] + part[N_EXPERTS:2 * N_EXPERTS] + _dot_nt(rw_hi, h_lo)
    cls, g_lo, g_hi = _route(logits_t, rb_ref[...])
    rows = jnp.concatenate([g_lo, g_hi, jnp.zeros((GATE_LANES - 2, tm), _F32)], axis=0)
    d = h.shape[1]
    h_ref[:, 0:d] = h
    h_ref[:, d:d + GATE_LANES] = rows.T
    cls_ref[0] = cls


def _outproj_ln_router_kernel(ot_ref, ob_ref, x_ref, w_ref, g_ref, b_ref, rwt_ref, rb_ref, h_ref, cls_ref):
    heads_t = jnp.concatenate([ot_ref[0], ob_ref[0]], axis=0)
    mixed = lax.dot_general(heads_t, w_ref[...], _TN_DIMS, preferred_element_type=_F32)
    z = ALPHA * x_ref[...] + mixed
    _ln_router_store(z, g_ref, b_ref, rwt_ref, rb_ref, h_ref, cls_ref)


def _outproj_ln_router(ot, ob, x2d, w, g, b, rwt, rb, bsz, seq, tm):
    nst = seq // tm
    d = x2d.shape[1]
    const = lambda shape: pl.BlockSpec(shape, lambda bb, i: (0,) * len(shape))
    return pl.pallas_call(
        _outproj_ln_router_kernel,
        out_shape=(jax.ShapeDtypeStruct((bsz * seq, d + GATE_LANES), _F32),
                   jax.ShapeDtypeStruct((bsz * nst, 1, tm), jnp.int32)),
        grid=(bsz, nst),
        in_specs=[
            pl.BlockSpec((1, ot.shape[1], tm), lambda bb, i: (bb, 0, i)),
            pl.BlockSpec((1, ob.shape[1], tm), lambda bb, i: (bb, 0, i)),
            pl.BlockSpec((tm, d), lambda bb, i: (bb * nst + i, 0)),
            const(w.shape), const(g.shape), const(b.shape), const(rwt.shape), const(rb.shape),
        ],
        out_specs=(pl.BlockSpec((tm, d + GATE_LANES), lambda bb, i: (bb * nst + i, 0)),
                   pl.BlockSpec((1, 1, tm), lambda bb, i: (bb * nst + i, 0, 0))),
        name="outproj_ln_router",
    )(ot, ob, x2d, w, g, b, rwt, rb)


_SUBLANES = 8
_SCAN_RADIX = 4
_DMA_CHUNK = 32


def _moe_kernel(ea_ref, eb_ref, nv_ref, off_ref, src_ref, nsrc_ref, h_hbm, w1a, w3a, w2a, w1b, w3b, w2b, g_ref, b_ref,
                out_hbm, xbuf, obuf, w13a, w2a_bf, w13b, w2b_bf, sem_in, sem_out):
    i = pl.program_id(0)
    ntiles = pl.num_programs(0)
    tm = xbuf.shape[1] * _SUBLANES
    d = obuf.shape[2]
    nv = nv_ref[i]
    nv_prev = jnp.where(i > 0, nv_ref[jnp.maximum(i - 1, 0)], 0)
    nv_next = jnp.where(i + 1 < ntiles, nv_ref[jnp.minimum(i + 1, ntiles - 1)], 0)
    slot = lax.rem(i, 2)
    shift = lax.rem(off_ref[i], LANE)
    shift_next = lax.rem(off_ref[jnp.minimum(i + 1, ntiles - 1)], LANE)

    def hbm_row(ref, idx):
        return ref.at[lax.shift_right_logical(idx, 3), jnp.bitwise_and(idx, _SUBLANES - 1)]

    def gather_start(idx_ref, sh, s):
        for r in range(tm):
            idx = idx_ref[sh + r]
            pltpu.make_async_copy(hbm_row(h_hbm, idx), xbuf.at[s, r // _SUBLANES, r % _SUBLANES],
                                  sem_in.at[s]).start(priority=r % 2)

    def scatter_wait(n):
        for bit in range(tm.bit_length()):
            rows = 1 << bit

            @pl.when(jnp.bitwise_and(n, rows) != 0)
            def _():
                if rows >= _SUBLANES:
                    grp = pl.ds(0, rows // _SUBLANES)
                    pltpu.make_async_copy(obuf.at[grp], out_hbm.at[grp], sem_out).wait()
                else:
                    pltpu.make_async_copy(obuf.at[0, pl.ds(0, rows)], out_hbm.at[0, pl.ds(0, rows)], sem_out).wait()

    @pl.when(jnp.logical_and(i == 0, nv > 0))
    def _prologue():
        gather_start(src_ref, shift, 0)

    @pl.when(nv_next > 0)
    def _prefetch():
        gather_start(nsrc_ref, shift_next, 1 - slot)

    prev_i = jnp.maximum(i - 1, 0)
    dff = w2a.shape[0]

    def refresh(w1, w3, w2, w13_sc, w2_sc):
        w13_sc[:, 0:dff] = w1[...].astype(_BF16)
        w13_sc[:, dff:2 * dff] = w3[...].astype(_BF16)
        w2_sc[...] = w2[...].astype(_BF16)

    @pl.when(jnp.logical_and(nv > 0, jnp.logical_or(i == 0, ea_ref[i] != ea_ref[prev_i])))
    def _refresh_a():
        refresh(w1a, w3a, w2a, w13a, w2a_bf)

    @pl.when(jnp.logical_and(nv > 0, jnp.logical_or(i == 0, eb_ref[i] != eb_ref[prev_i])))
    def _refresh_b():
        refresh(w1b, w3b, w2b, w13b, w2b_bf)

    @pl.when(nv > 0)
    def _tile():
        pltpu.make_async_copy(h_hbm.at[pl.ds(0, tm // _SUBLANES)], xbuf.at[slot], sem_in.at[slot]).wait()
        def compute(rows):
            xg = xbuf[slot, 0:rows // _SUBLANES].reshape(rows, d + GATE_LANES)
            x = xg[:, 0:d]
            gates = xg[:, d:d + GATE_LANES]
            ga = gates[:, 0:1]
            gb = gates[:, 1:2]
            xb = x.astype(_BF16)

            def expert(w13, w2):
                uv = _dot(xb, w13[...])
                u = uv[:, 0:dff]
                hh = u * _sigmoid(u) * uv[:, dff:2 * dff]
                return _dot(hh.astype(_BF16), w2[...])

            y = ga * expert(w13a, w2a_bf) + gb * expert(w13b, w2b_bf)
            z = _layer_norm(ALPHA * x + y, g_ref[...], b_ref[...])

            @pl.when(nv_prev > 0)
            def _drain_prev():
                scatter_wait(nv_prev)

            obuf[0:rows // _SUBLANES] = z.reshape(rows // _SUBLANES, _SUBLANES, d)

        @pl.when(nv > tm // 2)
        def _full():
            compute(tm)

        @pl.when(nv <= tm // 2)
        def _half():
            compute(tm // 2)

        for c in range(tm // _DMA_CHUNK):
            @pl.when(nv >= (c + 1) * _DMA_CHUNK)
            def _chunk():
                for r in range(c * _DMA_CHUNK, (c + 1) * _DMA_CHUNK):
                    idx = src_ref[shift + r]
                    pltpu.make_async_copy(obuf.at[r // _SUBLANES, r % _SUBLANES], hbm_row(out_hbm, idx),
                                          sem_out).start(priority=r % 2)

        def group(gi, c):
            for u in range(_SUBLANES):
                idx = src_ref[shift + gi * _SUBLANES + u]
                pltpu.make_async_copy(obuf.at[gi, u], hbm_row(out_hbm, idx), sem_out).start()
            return c

        def single(r, c):
            idx = src_ref[shift + r]
            pltpu.make_async_copy(hbm_row(obuf, r), hbm_row(out_hbm, idx), sem_out).start()
            return c

        ngroups = lax.shift_right_logical(nv, 3)
        lax.fori_loop((nv // _DMA_CHUNK) * (_DMA_CHUNK // _SUBLANES), ngroups, group, 0)
        lax.fori_loop(ngroups * _SUBLANES, nv, single, 0)

        @pl.when(nv_next == 0)
        def _drain_last():
            scatter_wait(nv)


def _moe(he, ea, eb, nvalid, off, order, w1, w3, w2, g, b, n_tokens, d):
    ntiles = ea.shape[0]
    tm = MOE_TM
    dff = w2.shape[1]
    wspec_a = lambda shape: pl.BlockSpec((None,) + shape, lambda i, ea_r, eb_r, nv_r, off_r: (ea_r[i], 0, 0))
    wspec_b = lambda shape: pl.BlockSpec((None,) + shape, lambda i, ea_r, eb_r, nv_r, off_r: (eb_r[i], 0, 0))

    def window(nxt):
        def start(i, ea_r, eb_r, nv_r, off_r):
            o = off_r[jnp.minimum(i + nxt, ntiles - 1)]
            return (pl.multiple_of((o // LANE) * LANE, LANE),)

        return pl.BlockSpec((pl.Element(2 * tm),), start, memory_space=pltpu.SMEM)

    grid_spec = pltpu.PrefetchScalarGridSpec(
        num_scalar_prefetch=4,
        grid=(ntiles,),
        in_specs=[
            window(0), window(1),
            pl.BlockSpec(memory_space=pl.ANY),
            wspec_a((d, dff)), wspec_a((d, dff)), wspec_a((dff, d)),
            wspec_b((d, dff)), wspec_b((d, dff)), wspec_b((dff, d)),
            pl.BlockSpec((1, d), lambda i, *_: (0, 0)),
            pl.BlockSpec((1, d), lambda i, *_: (0, 0)),
        ],
        out_specs=pl.BlockSpec(memory_space=pl.ANY),
        scratch_shapes=[
            pltpu.VMEM((2, tm // _SUBLANES, _SUBLANES, d + GATE_LANES), _F32),
            pltpu.VMEM((tm // _SUBLANES, _SUBLANES, d), _F32),
            pltpu.VMEM((d, 2 * dff), _BF16), pltpu.VMEM((dff, d), _BF16),
            pltpu.VMEM((d, 2 * dff), _BF16), pltpu.VMEM((dff, d), _BF16),
            pltpu.SemaphoreType.DMA((2,)),
            pltpu.SemaphoreType.DMA,
        ],
    )
    he3 = he.reshape(n_tokens // _SUBLANES, _SUBLANES, d + GATE_LANES)
    out = pl.pallas_call(
        _moe_kernel,
        out_shape=jax.ShapeDtypeStruct((n_tokens // _SUBLANES, _SUBLANES, d), _F32),
        grid_spec=grid_spec,
        compiler_params=pltpu.CompilerParams(dimension_semantics=("arbitrary",)),
        name="moe",
    )(ea, eb, nvalid, off, order, order, he3, w1, w3, w2, w1, w3, w2, g, b)
    return out.reshape(n_tokens, d)


def _moe_plan(cls, n_tokens):
    tm = MOE_TM
    ntiles = n_tokens // tm + N_CLASSES
    cls = cls.reshape(n_tokens)
    order = jnp.argsort(cls, stable=True).astype(jnp.int32)
    counts = jnp.sum((cls[:, None] == jnp.arange(N_CLASSES, dtype=jnp.int32)[None, :]).astype(jnp.int32), axis=0)
    cstart = jnp.cumsum(counts) - counts
    ptiles = (counts + tm - 1) // tm
    tend = jnp.cumsum(ptiles)
    tstart = tend - ptiles
    tile = jnp.arange(ntiles, dtype=jnp.int32)
    tcls = jnp.minimum(jnp.sum((tile[:, None] >= tend[None, :]).astype(jnp.int32), axis=1), N_CLASSES - 1)
    used = tile < tend[-1]
    nvalid = jnp.where(used, jnp.clip(counts[tcls] - (tile - tstart[tcls]) * tm, 0, tm), 0).astype(jnp.int32)
    off = jnp.where(used, cstart[tcls] + (tile - tstart[tcls]) * tm, 0).astype(jnp.int32)
    order = jnp.concatenate([order, jnp.zeros((2 * tm,), jnp.int32)])
    pair_lo = jnp.array([0, 0, 0, 1, 1, 2], jnp.int32)
    pair_hi = jnp.array([1, 2, 3, 2, 3, 3], jnp.int32)
    ea = (4 * (tcls // 6) + pair_lo[tcls % 6]).astype(jnp.int32)
    eb = (4 * (tcls // 6) + pair_hi[tcls % 6]).astype(jnp.int32)
    return ea, eb, nvalid, off, order


def _scan_slabs(a_refs, b_refs, c_refs, h_in, reverse):
    nlev = len(a_refs)
    ks = list(range(_SCAN_RADIX))
    if reverse:
        ks.reverse()
    for l in range(nlev - 1):
        grp = a_refs[l].shape[1] // _SCAN_RADIX
        p = q = None
        for k in ks:
            sl = pl.ds(k, grp, stride=_SCAN_RADIX)
            a = a_refs[l][:, sl, :]
            b = b_refs[l][:, sl, :]
            if p is None:
                p, q = a, b
            else:
                p, q = a * p, a * q + b
                a_refs[l][:, sl, :] = p
                b_refs[l][:, sl, :] = q
        a_refs[l + 1][...] = p
        b_refs[l + 1][...] = q
    rows = a_refs[-1].shape[1]
    a = a_refs[-1][...]
    b = b_refs[-1][...]
    out = [None] * rows
    h = h_in
    for r in (range(rows - 1, -1, -1) if reverse else range(rows)):
        h = a[:, r:r + 1, :] * h + b[:, r:r + 1, :]
        out[r] = h
    h_out = h
    b_refs[-1][...] = jnp.concatenate(out, axis=1)
    edge = jnp.broadcast_to(h_in, (h_in.shape[0], _SUBLANES, h_in.shape[2]))
    for l in range(nlev - 2, -1, -1):
        grp = a_refs[l].shape[1] // _SCAN_RADIX
        c = c_refs[l]
        c[:, _SUBLANES:_SUBLANES + grp, :] = b_refs[l + 1][...]
        if reverse:
            c[:, _SUBLANES + grp:2 * _SUBLANES + grp, :] = edge
            cin = c[:, _SUBLANES + 1:_SUBLANES + 1 + grp, :]
        else:
            c[:, 0:_SUBLANES, :] = edge
            cin = c[:, _SUBLANES - 1:_SUBLANES - 1 + grp, :]
        for k in ks:
            sl = pl.ds(k, grp, stride=_SCAN_RADIX)
            b_refs[l][:, sl, :] = a_refs[l][:, sl, :] * cin + b_refs[l][:, sl, :]
    return h_out


def _rglru_kernel(*refs, reverse, final, ntile):
    if final:
        (src_ref, prev_ref, next_ref, cw_ref, cb_ref, wa_ref, ba_ref, wx_ref, bx_ref, lam_ref,
         hf_ref, wgate_ref, res_ref, wout_ref, g_ref, b_ref, rwt_ref, rb_ref,
         h_ref, cls_ref, xe_sc, carry_sc, *scan_sc) = refs
    else:
        (src_ref, prev_ref, next_ref, cw_ref, cb_ref, wa_ref, ba_ref, wx_ref, bx_ref, lam_ref, wxr_ref,
         hout_ref, xr_out_ref, xe_sc, carry_sc, *scan_sc) = refs
    nlev = (len(scan_sc) + 1) // 3
    a_refs, b_refs, c_refs = scan_sc[:nlev], scan_sc[nlev:2 * nlev], scan_sc[2 * nlev:]
    i = pl.program_id(1)
    ti = (ntile - 1 - i) if reverse else i
    tt = src_ref.shape[0]

    @pl.when(i == 0)
    def _reset():
        carry_sc[...] = jnp.zeros_like(carry_sc)

    prev = jnp.where(ti > 0, prev_ref[...], 0.0)
    nxt = jnp.where(ti < ntile - 1, next_ref[...], 0.0)
    xin = jnp.concatenate([prev, src_ref[...], nxt], axis=0)
    if final:
        xe = xin
    else:
        xe = _dot(xin.astype(_BF16), wxr_ref[...])
        xr_out_ref[...] = xe[_SUBLANES:_SUBLANES + tt, :]
    for n in range(LRU_BLOCKS):
        xe_sc[n] = xe[:, n * LRU_BW:(n + 1) * LRU_BW]
    cw = cw_ref[...]
    cb = cb_ref[...]
    half_decay = (0.5 * LRU_C) * -jnp.log(1.0 + jnp.exp(-lam_ref[...]))
    for n in range(LRU_BLOCKS):
        sl = slice(n * LRU_BW, (n + 1) * LRU_BW)
        xcn = cb[:, sl]
        for k in range(cw.shape[0]):
            xcn = xcn + cw[k:k + 1, sl] * xe_sc[n, _SUBLANES - CONV_LEFT + k:_SUBLANES - CONV_LEFT + k + tt, :]
        xb = xcn.astype(_BF16)
        tr = jnp.tanh(_dot(xb, wa_ref[n]) + ba_ref[:, sl])
        ig = 0.5 * jnp.tanh(_dot(xb, wx_ref[n]) + bx_ref[:, sl]) + 0.5
        log_a = tr * half_decay[:, sl] + half_decay[:, sl]
        th = jnp.tanh(log_a)
        num = -2.0 * th
        scale = jnp.where(num > 0.0, num * lax.rsqrt(num * (1.0 - th)), 0.0)
        a_refs[0][n] = jnp.exp(log_a)
        b_refs[0][n] = scale * (ig * xcn)
    carry_sc[...] = _scan_slabs(a_refs, b_refs, c_refs, carry_sc[...], reverse)
    if final:
        gate = _dot(res_ref[...].astype(_BF16), wgate_ref[...])
    ys = []
    for n in range(LRU_BLOCKS):
        sl = slice(n * LRU_BW, (n + 1) * LRU_BW)
        h = b_refs[0][n]
        if final:
            ys.append((hf_ref[:, sl] + h) * jax.nn.gelu(gate[:, sl], approximate=True))
        else:
            hout_ref[:, sl] = h
    if final:
        y = jnp.concatenate(ys, axis=1).astype(_BF16)
        z = ALPHA * res_ref[...] + _dot(y, wout_ref[...])
        _ln_router_store(z, g_ref, b_ref, rwt_ref, rb_ref, h_ref, cls_ref)


def _rglru(src, cw, cb, wa, ba, wx, bx, lam, bsz, seq, tt, reverse, wxr=None, tail=None):
    ntile = seq // tt
    width = cw.shape[1]
    final = tail is not None
    tidx = (lambda i: ntile - 1 - i) if reverse else (lambda i: i)
    row = lambda b, i: (b * ntile + tidx(i), 0)
    g8 = tt // 8
    nb8 = seq // 8

    def prev_map(b, i):
        return (b * nb8 + jnp.maximum(tidx(i) * g8 - 1, 0), 0)

    def next_map(b, i):
        return (b * nb8 + jnp.minimum((tidx(i) + 1) * g8, nb8 - 1), 0)

    const = lambda shape: pl.BlockSpec(shape, lambda b, i: (0,) * len(shape))
    swidth = src.shape[1]
    in_specs = [
        pl.BlockSpec((tt, swidth), row),
        pl.BlockSpec((8, swidth), prev_map),
        pl.BlockSpec((8, swidth), next_map),
        const(cw.shape), const(cb.shape), const(wa.shape), const(ba.shape), const(wx.shape), const(bx.shape),
        const(lam.shape),
    ]
    args = [src, src, src, cw, cb, wa, ba, wx, bx, lam]
    rows = [tt]
    while rows[-1] > _SUBLANES:
        assert rows[-1] % _SCAN_RADIX == 0
        rows.append(rows[-1] // _SCAN_RADIX)
    assert rows[-1] == _SUBLANES
    slab = lambda r: pltpu.VMEM((LRU_BLOCKS, r, LRU_BW), _F32)
    scratch = [slab(tt + 2 * _SUBLANES), pltpu.VMEM((LRU_BLOCKS, 1, LRU_BW), _F32)]
    scratch += [slab(r) for r in rows] * 2 + [slab(r + 2 * _SUBLANES) for r in rows[1:]]
    if final:
        hf, wgate, res, wout, g, b, rwt, rb = tail
        d = res.shape[1]
        in_specs += [pl.BlockSpec((tt, width), row), const(wgate.shape), pl.BlockSpec((tt, d), row),
                     const(wout.shape), const(g.shape), const(b.shape), const(rwt.shape), const(rb.shape)]
        args += [hf, wgate, res, wout, g, b, rwt, rb]
        out_shape = (jax.ShapeDtypeStruct((bsz * seq, d + GATE_LANES), _F32),
                     jax.ShapeDtypeStruct((bsz * ntile, 1, tt), jnp.int32))
        out_specs = (pl.BlockSpec((tt, d + GATE_LANES), row),
                     pl.BlockSpec((1, 1, tt), lambda b, i: (b * ntile + tidx(i), 0, 0)))
    else:
        in_specs.append(const(wxr.shape))
        args.append(wxr)
        out_shape = (jax.ShapeDtypeStruct((bsz * seq, width), _F32),) * 2
        out_specs = (pl.BlockSpec((tt, width), row),) * 2
    return pl.pallas_call(
        functools.partial(_rglru_kernel, reverse=reverse, final=final, ntile=ntile),
        out_shape=out_shape,
        grid=(bsz, ntile),
        in_specs=in_specs,
        out_specs=out_specs,
        scratch_shapes=scratch,
        compiler_params=pltpu.CompilerParams(dimension_semantics=("arbitrary", "arbitrary")),
        name="rglru_bwd_tail" if final else "rglru_fwd",
    )(*args)


def _rotary_tables(seq):
    half = MLA_ROPE // 2
    inv_freq = ROPE_THETA ** (-jnp.arange(half, dtype=_F32) / half)
    ang = jnp.arange(seq, dtype=_F32)[:, None] * inv_freq[None, :]
    return jnp.cos(ang), jnp.sin(ang)


def _t5_bucket(rel):
    n_side = REL_BUCKETS // 2
    max_exact = n_side // 2
    dist = jnp.abs(rel)
    far = max_exact + (jnp.log(jnp.maximum(dist, 1).astype(_F32) / max_exact)
                       / math.log(REL_MAX_DIST / max_exact) * (n_side - max_exact)).astype(jnp.int32)
    far = jnp.minimum(far, n_side - 1)
    return jnp.where(rel > 0, n_side, 0) + jnp.where(dist < max_exact, dist, far)


def _band_buckets():
    rel = jnp.arange(3 * BLOCK)[None, :] - BLOCK - jnp.arange(BLOCK)[:, None]
    ids = jnp.where(jnp.abs(rel) <= WINDOW, _t5_bucket(rel), REL_BUCKETS).astype(jnp.int32)
    return jnp.bitwise_and(ids, 2 * REL_BUCKETS - 1)


def _pick_tile(seq, want):
    t = min(want, seq)
    while seq % t:
        t //= 2
    return t


def kernel(x, rel_bias, router_w, router_bias, l0_w_in, l0_q_norm, l0_w_uq, l0_kv_norm, l0_w_ukv, l0_sinks, l0_w_out, l0_ln1_g, l0_ln1_b, l0_w1, l0_w3, l0_w2, l0_ln2_g, l0_ln2_b, l1_w_in, l1_conv_w, l1_conv_b, l1_wa_f, l1_ba_f, l1_wx_f, l1_bx_f, l1_lam_f, l1_wa_b, l1_ba_b, l1_wx_b, l1_bx_b, l1_lam_b, l1_w_out, l1_ln1_g, l1_ln1_b, l1_w1, l1_w3, l1_w2, l1_ln2_g, l1_ln2_b):
    bsz, seq, d = x.shape
    n_tok = bsz * seq
    assert seq % BLOCK == 0 and d == LRU_BLOCKS * LRU_BW and n_tok % MOE_TM == 0
    x2d = x.reshape(n_tok, d)
    row = lambda v: v.reshape(1, -1).astype(_F32)
    rwt = router_w.astype(_F32).T
    rb = router_bias.astype(_F32).reshape(N_EXPERTS, 1)

    o = np.cumsum([0, MLA_Q_RANK, MLA_KV_RANK, MLA_ROPE, SWA_HEADS * SWA_D, SWA_KV_HEADS * SWA_D,
                   SWA_KV_HEADS * SWA_D])
    w_q, w_kv, w_kr, w_qb, w_kb, w_vb = [l0_w_in[:, o[i]:o[i + 1]] for i in range(6)]
    w_in0 = jnp.concatenate([w_q, w_kv, jnp.pad(w_kr, ((0, 0), (0, LANE - MLA_ROPE))), w_kb], axis=1).astype(_BF16)
    wuq = l0_w_uq.reshape(MLA_Q_RANK, MLA_HEADS, MLA_QK)
    wuqt = jnp.pad(wuq, ((0, 0), (0, 0), (0, HEAD_PAD - MLA_QK))).reshape(MLA_Q_RANK, MLA_HEADS * HEAD_PAD).T
    wukv = l0_w_ukv.reshape(MLA_KV_RANK, MLA_HEADS, MLA_NOPE + MLA_V)
    wuk = jnp.pad(wukv[:, :, :MLA_NOPE], ((0, 0), (0, 0), (0, HEAD_PAD - MLA_NOPE))).reshape(MLA_KV_RANK, -1)
    wuvt = wukv[:, :, MLA_NOPE:].reshape(MLA_KV_RANK, MLA_HEADS * MLA_V).T
    cos, sin = _rotary_tables(seq)
    ccs = jnp.pad(jnp.concatenate([cos, cos], axis=1), ((0, 0), (0, LANE - MLA_ROPE)))
    sn = jnp.pad(jnp.concatenate([-sin, sin], axis=1), ((0, 0), (0, LANE - MLA_ROPE)))

    tm = _pick_tile(seq, ROW_TILE)
    qt, k, vt, qbt, kb, vbt = _attn_inproj(
        x2d, w_in0, w_qb.T.astype(_BF16), w_vb.T.astype(_BF16), row(l0_q_norm), wuqt.astype(_BF16),
        row(l0_kv_norm), wuk.astype(_BF16), wuvt.astype(_BF16), cos.T, sin.T, ccs, sn, bsz, seq, tm)
    nstream = max(1, min(FLASH_STREAMS, seq // FLASH_TQ))
    ot = _mla_flash(qt, k, vt, _pick_tile(seq // nstream, FLASH_TQ), _pick_tile(seq // 2, FLASH_TK), nstream)
    kpad = jnp.pad(kb, ((0, 0), (BLOCK, BLOCK), (0, 0)))
    vtpad = jnp.pad(vbt, ((0, 0), (0, 0), (BLOCK, BLOCK)))
    obt = _swa(qbt, kpad, vtpad, _band_buckets().T, rel_bias.astype(_F32), l0_sinks.astype(_F32), bsz, seq,
               _pick_tile(seq // BLOCK, SWA_BLOCKS))
    he, cls = _outproj_ln_router(ot, obt, x2d, l0_w_out.astype(_BF16), row(l0_ln1_g), row(l0_ln1_b), rwt, rb,
                                 bsz, seq, tm)
    ea, eb, nvalid, off, order = _moe_plan(cls, n_tok)
    h = _moe(he, ea, eb, nvalid, off, order, l0_w1, l0_w3, l0_w2,
             row(l0_ln2_g), row(l0_ln2_b), n_tok, d)

    tt = _pick_tile(seq, ROW_TILE)
    w_in1 = l1_w_in.astype(_BF16)
    w_gate, w_xr = w_in1[:, :d], w_in1[:, d:]
    cw = l1_conv_w.astype(_F32)
    cb = row(l1_conv_b)
    halfw = lambda w: (0.5 * w).astype(_BF16)
    halfb = lambda v: 0.5 * row(v)
    hf, xr = _rglru(h, cw, cb, halfw(l1_wa_f), halfb(l1_ba_f), halfw(l1_wx_f), halfb(l1_bx_f),
                    row(l1_lam_f), bsz, seq, tt, reverse=False, wxr=w_xr)
    he, cls = _rglru(xr, cw, cb, halfw(l1_wa_b), halfb(l1_ba_b), halfw(l1_wx_b), halfb(l1_bx_b),
                     row(l1_lam_b), bsz, seq, tt, reverse=True,
                     tail=(hf, w_gate, h, l1_w_out.astype(_BF16), row(l1_ln1_g), row(l1_ln1_b), rwt, rb))
    ea, eb, nvalid, off, order = _moe_plan(cls, n_tok)
    h = _moe(he, ea, eb, nvalid, off, order, l1_w1, l1_w3, l1_w2,
             row(l1_ln2_g), row(l1_ln2_b), n_tok, d)
    return h.reshape(bsz, seq, d)
```
